```python
import math
import jax, jax.numpy as jnp
from jax import lax
import numpy as np

D_MODEL = 1024
BATCH = 4
SEQ = 4096
DEPTH = 4
DEC_BATCH = 128
DEC_SEQ = 8
PAST_LEN = 2048
PAGE_SIZE = 128

D_FF = ((8 * D_MODEL // 3) + 127) // 128 * 128
POOL_WINDOWS = (2, 4, 8, 16)
N_POOL_GROUPS = len(POOL_WINDOWS)
POOL_WIDTH = D_MODEL // 2
POOL_GROUP = POOL_WIDTH // N_POOL_GROUPS
POOL_BUF = max(POOL_WINDOWS) - 1
SB_HEAD_DIM = 64
SB_HEADS = (D_MODEL - POOL_WIDTH) // SB_HEAD_DIM
SB_WIDTH = SB_HEADS * SB_HEAD_DIM
AB_IN = POOL_WIDTH + 3 * SB_WIDTH
C_HEAD_DIM = 64
C_HEADS = D_MODEL // C_HEAD_DIM
C_KV_HEADS = 4
C_GROUP = C_HEADS // C_KV_HEADS
C_WIDTH = C_HEADS * C_HEAD_DIM
C_KV_WIDTH = C_KV_HEADS * C_HEAD_DIM
IDX_HEADS = 8
IDX_DIM = 64
TOPK_MAX = 256
C_SPLITS = (C_WIDTH, C_KV_WIDTH, C_KV_WIDTH, IDX_HEADS * IDX_DIM, IDX_DIM, IDX_HEADS)
C_IN = sum(C_SPLITS)
REL_BUCKETS = 32
REL_MAX_EXACT = 16
REL_MAX_DIST = 128
Q_BLOCK = 128
EPS = 1e-6
NEG = -1e30
N_EVEN = (DEPTH + 1) // 2
N_ODD = DEPTH // 2
F32 = jnp.float32

kernel_name = 'hybrid_pool_stickbreak_dsa_macaron_step'


def rmsnorm(x, g):
    xf = x.astype(F32)
    y = xf * lax.rsqrt(jnp.mean(xf * xf, axis=-1, keepdims=True) + EPS)
    return (y * g.astype(F32)).astype(x.dtype)


def swiglu(h, w_gate, w_up, w_down):
    return (jax.nn.silu(h @ w_gate) * (h @ w_up)) @ w_down


def t5_bucket(dist):
    n = jnp.maximum(dist, 0)
    nf = jnp.maximum(n, 1).astype(F32)
    large = REL_MAX_EXACT + (jnp.log(nf / REL_MAX_EXACT) / math.log(REL_MAX_DIST / REL_MAX_EXACT)
                             * (REL_BUCKETS - REL_MAX_EXACT)).astype(jnp.int32)
    large = jnp.minimum(large, REL_BUCKETS - 1)
    return jnp.where(n < REL_MAX_EXACT, n, large)


def pool_mix(u_ext, pos, w_pool, pool_scale):
    b, l, c = u_ext.shape
    t = l - POOL_BUF
    uf = u_ext.astype(F32)
    cs = jnp.concatenate([jnp.zeros((b, 1, c), F32), jnp.cumsum(uf, axis=1)], axis=1)
    u_new = uf[:, POOL_BUF:]
    hi = cs[:, POOL_BUF + 1:]
    outs = []
    for g, w in enumerate(POOL_WINDOWS):
        sl = slice(g * POOL_GROUP, (g + 1) * POOL_GROUP)
        lo = cs[:, POOL_BUF + 1 - w:POOL_BUF + 1 - w + t, sl]
        cnt = jnp.minimum(pos + 1, w).astype(F32)[None, :, None]
        outs.append((hi[..., sl] - lo) / cnt - u_new[..., sl])
    d = jnp.stack(outs, axis=2)
    y = jnp.einsum('btgc,gcd->btgd', d, w_pool.astype(F32)).reshape(b, t, POOL_WIDTH)
    return y * pool_scale.astype(F32)


def stick_breaking(q, k, v, q_pos, k_pos):
    z = jnp.einsum('bqhd,bkhd->bhqk', q.astype(F32), k.astype(F32)) * (SB_HEAD_DIM ** -0.5)
    mask = (k_pos[None, :] < q_pos[:, None])[None, None]
    log_rest = jnp.where(mask, jax.nn.log_sigmoid(-z), 0.0)
    between = lax.cumsum(log_rest, axis=3, reverse=True) - log_rest
    a = jnp.where(mask, jnp.exp(jax.nn.log_sigmoid(z) + between), 0.0)
    return jnp.einsum('bhqk,bkhd->bqhd', a, v.astype(F32))


def ab_project(h, w_in):
    b, t, _ = h.shape
    proj = h @ w_in
    u = proj[..., :POOL_WIDTH]
    qkv = proj[..., POOL_WIDTH:].reshape(b, t, 3, SB_HEADS, SB_HEAD_DIM)
    return u, qkv[:, :, 0], qkv[:, :, 1], qkv[:, :, 2]


def ab_prompt(h, w_in, w_pool, pool_scale, w_out):
    b, t, _ = h.shape
    u, q, k, v = ab_project(h, w_in)
    pos = jnp.arange(t, dtype=jnp.int32)
    u_ext = jnp.concatenate([jnp.zeros((b, POOL_BUF, POOL_WIDTH), u.dtype), u], axis=1)
    y_pool = pool_mix(u_ext, pos, w_pool, pool_scale)

    def blk(i):
        s0 = i * Q_BLOCK
        qb = lax.dynamic_slice_in_dim(q, s0, Q_BLOCK, axis=1)
        return stick_breaking(qb, k, v, s0 + jnp.arange(Q_BLOCK, dtype=jnp.int32), pos)

    y_sb = lax.map(blk, jnp.arange(t // Q_BLOCK, dtype=jnp.int32))
    y_sb = jnp.moveaxis(y_sb, 0, 1).reshape(b, t, SB_WIDTH)
    y = jnp.concatenate([y_pool, y_sb], axis=-1).astype(h.dtype) @ w_out
    return y, u_ext[:, -POOL_BUF:], k, v


def ab_sample(h, pool_buf, k_past, v_past, w_in, w_pool, pool_scale, w_out):
    b, t, _ = h.shape
    u, q, k, v = ab_project(h, w_in)
    n_past = k_past.shape[1]
    q_pos = n_past + jnp.arange(t, dtype=jnp.int32)
    u_ext = jnp.concatenate([pool_buf.astype(u.dtype), u], axis=1)
    y_pool = pool_mix(u_ext, q_pos, w_pool, pool_scale)
    k_all = jnp.concatenate([k_past.astype(k.dtype), k], axis=1)
    v_all = jnp.concatenate([v_past.astype(v.dtype), v], axis=1)
    k_pos = jnp.arange(n_past + t, dtype=jnp.int32)
    y_sb = stick_breaking(q, k_all, v_all, q_pos, k_pos).reshape(b, t, SB_WIDTH)
    y = jnp.concatenate([y_pool, y_sb], axis=-1).astype(h.dtype) @ w_out
    return y, u_ext[:, -POOL_BUF:], k, v


def c_project(h, w_in, g_q, g_k):
    b, t, _ = h.shape
    proj = h @ w_in
    offs = [int(o) for o in np.cumsum(C_SPLITS)[:-1]]
    q, k, v, qi, ki, wi = jnp.split(proj, offs, axis=-1)
    q = rmsnorm(q.reshape(b, t, C_KV_HEADS, C_GROUP, C_HEAD_DIM), g_q)
    k = rmsnorm(k.reshape(b, t, C_KV_HEADS, C_HEAD_DIM), g_k)
    v = v.reshape(b, t, C_KV_HEADS, C_HEAD_DIM)
    qi = qi.reshape(b, t, IDX_HEADS, IDX_DIM)
    return q, k, v, qi, ki, wi


def index_scores(qi, wi, ki):
    s = jnp.einsum('bqhd,bkd->bqhk', qi.astype(F32), ki.astype(F32)) * (IDX_DIM ** -0.5)
    return jnp.einsum('bqhk,bqh->bqk', jax.nn.relu(s), wi.astype(F32) * (IDX_HEADS ** -0.5))


def c_attend(q, k_sel, v_sel, dist, rel_bias):
    logits = jnp.einsum('bqngd,bqknd->bqngk', q.astype(F32), k_sel.astype(F32)) * (C_HEAD_DIM ** -0.5)
    bias = rel_bias.astype(F32)[t5_bucket(dist)]
    b, tq, kk, _ = bias.shape
    bias = bias.reshape(b, tq, kk, C_KV_HEADS, C_GROUP).transpose(0, 1, 3, 4, 2)
    valid = (dist >= 0)[:, :, None, None, :]
    logits = jnp.where(valid, logits + bias, NEG)
    p = jax.nn.softmax(logits, axis=-1)
    o = jnp.einsum('bqngk,bqknd->bqngd', p, v_sel.astype(F32))
    return o.reshape(b, tq, C_WIDTH)


def c_prompt(h, w_in, g_q, g_k, w_out, rel_bias):
    b, t, _ = h.shape
    q, k, v, qi, ki, wi = c_project(h, w_in, g_q, g_k)
    topk = min(TOPK_MAX, t // 4)
    k_pos = jnp.arange(t, dtype=jnp.int32)
    bidx = jnp.arange(b)[:, None, None]

    def blk(i):
        s0 = i * Q_BLOCK
        q_pos = s0 + jnp.arange(Q_BLOCK, dtype=jnp.int32)
        qb = lax.dynamic_slice_in_dim(q, s0, Q_BLOCK, axis=1)
        qib = lax.dynamic_slice_in_dim(qi, s0, Q_BLOCK, axis=1)
        wib = lax.dynamic_slice_in_dim(wi, s0, Q_BLOCK, axis=1)
        sc = index_scores(qib, wib, ki)
        sc = jnp.where((k_pos[None, :] <= q_pos[:, None])[None], sc, -jnp.inf)
        _, idx = lax.top_k(sc, topk)
        dist = q_pos[None, :, None] - idx
        return c_attend(qb, k[bidx, idx], v[bidx, idx], dist, rel_bias)

    o = lax.map(blk, jnp.arange(t // Q_BLOCK, dtype=jnp.int32))
    o = jnp.moveaxis(o, 0, 1).reshape(b, t, C_WIDTH)
    return o.astype(h.dtype) @ w_out, k, v, ki


def c_sample(h, cache_k, cache_v, cache_idx, layer, page_table, w_in, g_q, g_k, w_out, rel_bias):
    b, t, _ = h.shape
    q, k, v, qi, ki, wi = c_project(h, w_in, g_q, g_k)
    n_past = page_table.shape[1] * PAGE_SIZE
    ki_past = cache_idx[layer, page_table].reshape(b, n_past, IDX_DIM)
    ki_all = jnp.concatenate([ki_past.astype(ki.dtype), ki], axis=1)
    n_keys = n_past + t
    topk = min(TOPK_MAX, n_keys // 4)
    q_pos = n_past + jnp.arange(t, dtype=jnp.int32)
    k_pos = jnp.arange(n_keys, dtype=jnp.int32)
    sc = index_scores(qi, wi, ki_all)
    sc = jnp.where((k_pos[None, :] <= q_pos[:, None])[None], sc, -jnp.inf)
    _, idx = lax.top_k(sc, topk)
    bidx = jnp.arange(b)[:, None, None]
    in_past = idx < n_past
    pidx = jnp.minimum(idx, n_past - 1)
    phys = page_table[bidx, pidx // PAGE_SIZE]
    off = pidx % PAGE_SIZE
    nidx = jnp.clip(idx - n_past, 0, t - 1)
    k_sel = jnp.where(in_past[..., None, None], cache_k[layer, phys, off].astype(k.dtype), k[bidx, nidx])
    v_sel = jnp.where(in_past[..., None, None], cache_v[layer, phys, off].astype(v.dtype), v[bidx, nidx])
    o = c_attend(q, k_sel, v_sel, q_pos[None, :, None] - idx, rel_bias)
    return o.astype(h.dtype) @ w_out, k, v, ki


def setup_inputs(seed: int = 0) -> dict:
    key = jax.random.key(seed)
    ks = jax.random.split(key, 24)
    n_pages = PAST_LEN // PAGE_SIZE
    n_used = DEC_BATCH * n_pages
    n_pool = n_used + n_used // 4

    def nrm(k, shape, scale=1.0):
        return scale * jax.random.normal(k, shape, F32)

    page_table = jax.random.permutation(ks[0], n_pool)[:n_used].reshape(DEC_BATCH, n_pages).astype(jnp.int32)
    return {
        'x_prompt': nrm(ks[1], (BATCH, SEQ, D_MODEL)),
        'x_sample': nrm(ks[2], (DEC_BATCH, DEC_SEQ, D_MODEL)),
        'state_pool': nrm(ks[3], (N_EVEN, DEC_BATCH, POOL_BUF, POOL_WIDTH)),
        'cache_b_k': nrm(ks[4], (N_EVEN, n_pool, PAGE_SIZE, SB_HEADS, SB_HEAD_DIM)),
        'cache_b_v': nrm(ks[5], (N_EVEN, n_pool, PAGE_SIZE, SB_HEADS, SB_HEAD_DIM)),
        'cache_c_k': nrm(ks[6], (N_ODD, n_pool, PAGE_SIZE, C_KV_HEADS, C_HEAD_DIM)),
        'cache_c_v': nrm(ks[7], (N_ODD, n_pool, PAGE_SIZE, C_KV_HEADS, C_HEAD_DIM)),
        'cache_c_idx': nrm(ks[8], (N_ODD, n_pool, PAGE_SIZE, IDX_DIM)),
        'page_table': page_table,
        'g_ffn': 1.0 + nrm(ks[9], (DEPTH, 2, D_MODEL), 0.1),
        'w_ffn_gate': nrm(ks[10], (DEPTH, 2, D_MODEL, D_FF), D_MODEL ** -0.5),
        'w_ffn_up': nrm(ks[11], (DEPTH, 2, D_MODEL, D_FF), D_MODEL ** -0.5),
        'w_ffn_down': nrm(ks[12], (DEPTH, 2, D_FF, D_MODEL), D_FF ** -0.5),
        'g_mix': 1.0 + nrm(ks[13], (DEPTH, D_MODEL), 0.1),
        'w_in_ab': nrm(ks[14], (N_EVEN, D_MODEL, AB_IN), D_MODEL ** -0.5),
        'w_pool': nrm(ks[15], (N_EVEN, N_POOL_GROUPS, POOL_GROUP, POOL_GROUP), POOL_GROUP ** -0.5),
        'pool_scale': 1.0 + nrm(ks[16], (N_EVEN, POOL_WIDTH), 0.1),
        'w_out_ab': nrm(ks[17], (N_EVEN, POOL_WIDTH + SB_WIDTH, D_MODEL), (POOL_WIDTH + SB_WIDTH) ** -0.5),
        'w_in_c': nrm(ks[18], (N_ODD, D_MODEL, C_IN), D_MODEL ** -0.5),
        'g_q': 1.0 + nrm(ks[19], (N_ODD, C_HEAD_DIM), 0.1),
        'g_k': 1.0 + nrm(ks[20], (N_ODD, C_HEAD_DIM), 0.1),
        'w_out_c': nrm(ks[21], (N_ODD, C_WIDTH, D_MODEL), C_WIDTH ** -0.5),
        'rel_bias': nrm(ks[22], (REL_BUCKETS, C_HEADS), 0.5),
    }


def reference(x_prompt, x_sample, state_pool, cache_b_k, cache_b_v, cache_c_k, cache_c_v, cache_c_idx, page_table,
              g_ffn, w_ffn_gate, w_ffn_up, w_ffn_down, g_mix, w_in_ab, w_pool, pool_scale, w_out_ab,
              w_in_c, g_q, g_k, w_out_c, rel_bias):
    def half_ffn(x, layer, i):
        h = rmsnorm(x, g_ffn[layer, i])
        return x + 0.5 * swiglu(h, w_ffn_gate[layer, i], w_ffn_up[layer, i], w_ffn_down[layer, i])

    xp, xs = x_prompt, x_sample
    b_s = x_sample.shape[0]
    pool_p, pool_s, kbp, vbp, kbs, vbs = [], [], [], [], [], []
    kcp, vcp, icp, kcs, vcs, ics = [], [], [], [], [], []
    for layer in range(DEPTH):
        xp = half_ffn(xp, layer, 0)
        xs = half_ffn(xs, layer, 0)
        hp = rmsnorm(xp, g_mix[layer])
        hs = rmsnorm(xs, g_mix[layer])
        j = layer // 2
        if layer % 2 == 0:
            yp, sp, kp, vp = ab_prompt(hp, w_in_ab[j], w_pool[j], pool_scale[j], w_out_ab[j])
            kb_past = cache_b_k[j, page_table].reshape(b_s, -1, SB_HEADS, SB_HEAD_DIM)
            vb_past = cache_b_v[j, page_table].reshape(b_s, -1, SB_HEADS, SB_HEAD_DIM)
            ys, ss, kn, vn = ab_sample(hs, state_pool[j], kb_past, vb_past,
                                       w_in_ab[j], w_pool[j], pool_scale[j], w_out_ab[j])
            pool_p.append(sp); pool_s.append(ss)
            kbp.append(kp); vbp.append(vp); kbs.append(kn); vbs.append(vn)
        else:
            yp, kp, vp, ip = c_prompt(hp, w_in_c[j], g_q[j], g_k[j], w_out_c[j], rel_bias)
            ys, kn, vn, inn = c_sample(hs, cache_c_k, cache_c_v, cache_c_idx, j, page_table,
                                       w_in_c[j], g_q[j], g_k[j], w_out_c[j], rel_bias)
            kcp.append(kp); vcp.append(vp); icp.append(ip)
            kcs.append(kn); vcs.append(vn); ics.append(inn)
        xp = xp + yp
        xs = xs + ys
        xp = half_ffn(xp, layer, 1)
        xs = half_ffn(xs, layer, 1)

    pool_prompt = jnp.stack(pool_p)
    pool_sample = jnp.stack(pool_s)
    kb_prompt = jnp.stack(kbp)
    vb_prompt = jnp.stack(vbp)
    kb_sample = jnp.stack(kbs)
    vb_sample = jnp.stack(vbs)
    kc_prompt = jnp.stack(kcp)
    vc_prompt = jnp.stack(vcp)
    ic_prompt = jnp.stack(icp)
    kc_sample = jnp.stack(kcs)
    vc_sample = jnp.stack(vcs)
    ic_sample = jnp.stack(ics)
    return (xp, xs, pool_prompt, pool_sample, kb_prompt, vb_prompt, kb_sample, vb_sample,
            kc_prompt, vc_prompt, ic_prompt, kc_sample, vc_sample, ic_sample)
```

```python
import functools
import math

import jax
import jax.numpy as jnp
import numpy as np
from jax import lax
from jax.experimental import pallas as pl
from jax.experimental.pallas import tpu as pltpu

F32 = jnp.float32
BF16 = jnp.bfloat16

D_MODEL = 1024
DEPTH = 4
D_FF = 2816
POOL_WINDOWS = (2, 4, 8, 16)
POOL_WIDTH = 512
POOL_GROUP = 128
POOL_BUF = 15
SB_HEADS = 8
SB_HEAD_DIM = 64
SB_WIDTH = 512
AB_IN = POOL_WIDTH + 3 * SB_WIDTH
C_HEAD_DIM = 64
C_HEADS = 16
C_KV_HEADS = 4
C_GROUP = 4
C_WIDTH = 1024
C_KV_WIDTH = 256
IDX_HEADS = 8
IDX_DIM = 64
TOPK_MAX = 256
C_IN = 2120
C_IN_PAD = 2176
REL_BUCKETS = 32
REL_MAX_EXACT = 16
REL_MAX_DIST = 128
Q_BLOCK = 128
PAGE_SIZE = 128
EPS = 1e-6
NEG = -1e30

VMEM_LIMIT_BYTES = 56 * 1024 * 1024
FF_CHUNK = 256
TOKEN_TILE = 512


def _rms(x, g):
    ms = jnp.mean(x * x, axis=-1, keepdims=True)
    return x * lax.rsqrt(ms + EPS) * g


def _ffn_kernel(x_ref, g_ref, wg_ref, wu_ref, wd_ref, o_ref):
    x = x_ref[...]
    h = _rms(x, g_ref[...]).astype(BF16)
    acc = x
    for c in range(D_FF // FF_CHUNK):
        sl = slice(c * FF_CHUNK, (c + 1) * FF_CHUNK)
        gate = jnp.dot(h, wg_ref[:, sl], preferred_element_type=F32)
        up = jnp.dot(h, wu_ref[:, sl], preferred_element_type=F32)
        act = (0.5 * gate * jax.nn.sigmoid(gate) * up).astype(BF16)
        acc = acc + jnp.dot(act, wd_ref[sl, :], preferred_element_type=F32)
    o_ref[...] = acc


def _ffn(x, g, wg, wu, wd):
    m = x.shape[0]
    resident = dict(pipeline_mode=pl.Buffered(1))
    return pl.pallas_call(
        _ffn_kernel,
        out_shape=jax.ShapeDtypeStruct((m, D_MODEL), F32),
        grid=(m // TOKEN_TILE,),
        in_specs=[
            pl.BlockSpec((TOKEN_TILE, D_MODEL), lambda i: (i, 0)),
            pl.BlockSpec((1, D_MODEL), lambda i: (0, 0)),
            pl.BlockSpec((D_MODEL, D_FF), lambda i: (0, 0), **resident),
            pl.BlockSpec((D_MODEL, D_FF), lambda i: (0, 0), **resident),
            pl.BlockSpec((D_FF, D_MODEL), lambda i: (0, 0), **resident),
        ],
        out_specs=pl.BlockSpec((TOKEN_TILE, D_MODEL), lambda i: (i, 0)),
        compiler_params=pltpu.CompilerParams(
            dimension_semantics=("parallel",), vmem_limit_bytes=VMEM_LIMIT_BYTES),
        name="ffn_half",
    )(x, g, wg, wu, wd)


def _proj_ab_kernel(x_ref, g_ref, w_ref, u_ref, q_ref, k_ref, v_ref):
    h = _rms(x_ref[...], g_ref[...]).astype(BF16)
    p = jnp.dot(h, w_ref[...], preferred_element_type=F32)
    u_ref[...] = p[:, :POOL_WIDTH]
    q_ref[...] = p[:, POOL_WIDTH:POOL_WIDTH + SB_WIDTH]
    k_ref[...] = p[:, POOL_WIDTH + SB_WIDTH:POOL_WIDTH + 2 * SB_WIDTH]
    v_ref[...] = p[:, POOL_WIDTH + 2 * SB_WIDTH:]


def _proj_ab(x, g, w):
    m = x.shape[0]
    out = jax.ShapeDtypeStruct((m, SB_WIDTH), F32)
    spec = pl.BlockSpec((TOKEN_TILE, SB_WIDTH), lambda i: (i, 0))
    return pl.pallas_call(
        _proj_ab_kernel,
        out_shape=(out, out, out, out),
        grid=(m // TOKEN_TILE,),
        in_specs=[
            pl.BlockSpec((TOKEN_TILE, D_MODEL), lambda i: (i, 0)),
            pl.BlockSpec((1, D_MODEL), lambda i: (0, 0)),
            pl.BlockSpec((D_MODEL, AB_IN), lambda i: (0, 0), pipeline_mode=pl.Buffered(1)),
        ],
        out_specs=(spec, spec, spec, spec),
        compiler_params=pltpu.CompilerParams(
            dimension_semantics=("parallel",), vmem_limit_bytes=VMEM_LIMIT_BYTES),
        name="proj_ab",
    )(x, g, w)


def _split_hi_lo(x):
    hi = x.astype(BF16)
    lo = (x - hi.astype(F32)).astype(BF16)
    return hi, lo


def _head_rms(x, seg_mean_ref, gain):
    hi, lo = _split_hi_lo(x * x)
    ms = (jnp.dot(hi, seg_mean_ref[...], preferred_element_type=F32)
          + jnp.dot(lo, seg_mean_ref[...], preferred_element_type=F32))
    return x * lax.rsqrt(ms + EPS) * gain


def _proj_c_kernel(x_ref, g_ref, w_ref, gq_ref, gk_ref, segq_ref, segk_ref,
                   q_ref, k_ref, v_ref, qi_ref, tail_ref):
    h = _rms(x_ref[...], g_ref[...]).astype(BF16)
    p = jnp.dot(h, w_ref[...], preferred_element_type=F32)
    o_k = C_WIDTH
    o_v = o_k + C_KV_WIDTH
    o_qi = o_v + C_KV_WIDTH
    o_tail = o_qi + IDX_HEADS * IDX_DIM
    q_ref[...] = _head_rms(p[:, :o_k], segq_ref, gq_ref[...])
    k_ref[...] = _head_rms(p[:, o_k:o_v], segk_ref, gk_ref[...])
    v_ref[...] = p[:, o_v:o_qi]
    qi_ref[...] = p[:, o_qi:o_tail]
    tail_ref[...] = p[:, o_tail:]


def _proj_c(x, g, w, gq, gk, segq, segk):
    m = x.shape[0]
    tail = C_IN_PAD - (C_WIDTH + 2 * C_KV_WIDTH + IDX_HEADS * IDX_DIM)
    widths = (C_WIDTH, C_KV_WIDTH, C_KV_WIDTH, IDX_HEADS * IDX_DIM, tail)
    const = lambda i: (0, 0)
    return pl.pallas_call(
        _proj_c_kernel,
        out_shape=tuple(jax.ShapeDtypeStruct((m, n), F32) for n in widths),
        grid=(m // TOKEN_TILE,),
        in_specs=[
            pl.BlockSpec((TOKEN_TILE, D_MODEL), lambda i: (i, 0)),
            pl.BlockSpec((1, D_MODEL), const),
            pl.BlockSpec((D_MODEL, C_IN_PAD), const, pipeline_mode=pl.Buffered(1)),
            pl.BlockSpec((1, C_WIDTH), const),
            pl.BlockSpec((1, C_KV_WIDTH), const),
            pl.BlockSpec((C_WIDTH, C_WIDTH), const, pipeline_mode=pl.Buffered(1)),
            pl.BlockSpec((C_KV_WIDTH, C_KV_WIDTH), const, pipeline_mode=pl.Buffered(1)),
        ],
        out_specs=tuple(pl.BlockSpec((TOKEN_TILE, n), lambda i: (i, 0)) for n in widths),
        compiler_params=pltpu.CompilerParams(
            dimension_semantics=("parallel",), vmem_limit_bytes=VMEM_LIMIT_BYTES),
        name="proj_c",
    )(x, g, w, gq, gk, segq, segk)


def _out_proj_kernel(x_ref, y_ref, w_ref, o_ref):
    o_ref[...] = x_ref[...] + jnp.dot(y_ref[...].astype(BF16), w_ref[...],
                                      preferred_element_type=F32)


def _out_proj(x, y, w):
    m = x.shape[0]
    tile = pl.BlockSpec((TOKEN_TILE, D_MODEL), lambda i: (i, 0))
    return pl.pallas_call(
        _out_proj_kernel,
        out_shape=jax.ShapeDtypeStruct((m, D_MODEL), F32),
        grid=(m // TOKEN_TILE,),
        in_specs=[tile, tile,
                  pl.BlockSpec((D_MODEL, D_MODEL), lambda i: (0, 0), pipeline_mode=pl.Buffered(1))],
        out_specs=tile,
        compiler_params=pltpu.CompilerParams(
            dimension_semantics=("parallel",), vmem_limit_bytes=VMEM_LIMIT_BYTES),
        name="out_proj",
    )(x, y, w)


def _t5_bucket(dist):
    n = jnp.maximum(dist, 0)
    nf = jnp.maximum(n, 1).astype(F32)
    large = REL_MAX_EXACT + (jnp.log(nf / REL_MAX_EXACT) / math.log(REL_MAX_DIST / REL_MAX_EXACT)
                             * (REL_BUCKETS - REL_MAX_EXACT)).astype(jnp.int32)
    large = jnp.minimum(large, REL_BUCKETS - 1)
    return jnp.where(n < REL_MAX_EXACT, n, large)


def _pool_mix(u_ext, pos, w_pool, pool_scale):
    b, l, c = u_ext.shape
    t = l - POOL_BUF
    cs = jnp.concatenate([jnp.zeros((b, 1, c), F32), jnp.cumsum(u_ext, axis=1)], axis=1)
    u_new = u_ext[:, POOL_BUF:]
    hi = cs[:, POOL_BUF + 1:]
    outs = []
    for g, w in enumerate(POOL_WINDOWS):
        sl = slice(g * POOL_GROUP, (g + 1) * POOL_GROUP)
        lo = cs[:, POOL_BUF + 1 - w:POOL_BUF + 1 - w + t, sl]
        cnt = jnp.minimum(pos + 1, w).astype(F32)[None, :, None]
        outs.append((hi[..., sl] - lo) / cnt - u_new[..., sl])
    d = jnp.stack(outs, axis=2)
    y = jnp.einsum('btgc,gcd->btgd', d, w_pool).reshape(b, t, POOL_WIDTH)
    return y * pool_scale


def _stick_breaking(q, k, v, q_pos, k_pos):
    z = jnp.einsum('bqhd,bkhd->bhqk', q, k) * (SB_HEAD_DIM ** -0.5)
    mask = (k_pos[None, :] < q_pos[:, None])[None, None]
    log_rest = jnp.where(mask, jax.nn.log_sigmoid(-z), 0.0)
    between = lax.cumsum(log_rest, axis=3, reverse=True) - log_rest
    a = jnp.where(mask, jnp.exp(jax.nn.log_sigmoid(z) + between), 0.0)
    return jnp.einsum('bhqk,bkhd->bqhd', a, v)


def _index_scores(qi, wi, ki):
    s = jnp.einsum('bqhd,bkd->bqhk', qi, ki) * (IDX_DIM ** -0.5)
    return jnp.einsum('bqhk,bqh->bqk', jax.nn.relu(s), wi * (IDX_HEADS ** -0.5))


def _c_attend(q, k_sel, v_sel, dist, rel_bias):
    logits = jnp.einsum('bqngd,bqknd->bqngk', q, k_sel) * (C_HEAD_DIM ** -0.5)
    bias = rel_bias[_t5_bucket(dist)]
    b, tq, kk, _ = bias.shape
    bias = bias.reshape(b, tq, kk, C_KV_HEADS, C_GROUP).transpose(0, 1, 3, 4, 2)
    valid = (dist >= 0)[:, :, None, None, :]
    logits = jnp.where(valid, logits + bias, NEG)
    p = jax.nn.softmax(logits, axis=-1)
    o = jnp.einsum('bqngk,bqknd->bqngd', p, v_sel)
    return o.reshape(b, tq, C_WIDTH)


def _ab_prompt_mix(u, q, k, v, w_pool, pool_scale):
    b, t, _ = u.shape
    pos = jnp.arange(t, dtype=jnp.int32)
    u_ext = jnp.concatenate([jnp.zeros((b, POOL_BUF, POOL_WIDTH), F32), u], axis=1)
    y_pool = _pool_mix(u_ext, pos, w_pool, pool_scale)
    q4 = q.reshape(b, t, SB_HEADS, SB_HEAD_DIM)
    k4 = k.reshape(b, t, SB_HEADS, SB_HEAD_DIM)
    v4 = v.reshape(b, t, SB_HEADS, SB_HEAD_DIM)

    def blk(i):
        s0 = i * Q_BLOCK
        qb = lax.dynamic_slice_in_dim(q4, s0, Q_BLOCK, axis=1)
        return _stick_breaking(qb, k4, v4, s0 + jnp.arange(Q_BLOCK, dtype=jnp.int32), pos)

    y_sb = lax.map(blk, jnp.arange(t // Q_BLOCK, dtype=jnp.int32))
    y_sb = jnp.moveaxis(y_sb, 0, 1).reshape(b, t, SB_WIDTH)
    return jnp.concatenate([y_pool, y_sb], axis=-1), u_ext[:, -POOL_BUF:]


def _ab_sample_mix(u, q, k, v, pool_buf, k_past, v_past, w_pool, pool_scale):
    b, t, _ = u.shape
    n_past = k_past.shape[1]
    q_pos = n_past + jnp.arange(t, dtype=jnp.int32)
    u_ext = jnp.concatenate([pool_buf, u], axis=1)
    y_pool = _pool_mix(u_ext, q_pos, w_pool, pool_scale)
    q4 = q.reshape(b, t, SB_HEADS, SB_HEAD_DIM)
    k4 = k.reshape(b, t, SB_HEADS, SB_HEAD_DIM)
    v4 = v.reshape(b, t, SB_HEADS, SB_HEAD_DIM)
    k_all = jnp.concatenate([k_past, k4], axis=1)
    v_all = jnp.concatenate([v_past, v4], axis=1)
    k_pos = jnp.arange(n_past + t, dtype=jnp.int32)
    y_sb = _stick_breaking(q4, k_all, v_all, q_pos, k_pos).reshape(b, t, SB_WIDTH)
    return jnp.concatenate([y_pool, y_sb], axis=-1), u_ext[:, -POOL_BUF:]


def _c_prompt_mix(q, k, v, qi, ki, wi, rel_bias):
    b, t, _ = q.shape
    q5 = q.reshape(b, t, C_KV_HEADS, C_GROUP, C_HEAD_DIM)
    k4 = k.reshape(b, t, C_KV_HEADS, C_HEAD_DIM)
    v4 = v.reshape(b, t, C_KV_HEADS, C_HEAD_DIM)
    qi4 = qi.reshape(b, t, IDX_HEADS, IDX_DIM)
    topk = min(TOPK_MAX, t // 4)
    k_pos = jnp.arange(t, dtype=jnp.int32)
    bidx = jnp.arange(b)[:, None, None]

    def blk(i):
        s0 = i * Q_BLOCK
        q_pos = s0 + jnp.arange(Q_BLOCK, dtype=jnp.int32)
        qb = lax.dynamic_slice_in_dim(q5, s0, Q_BLOCK, axis=1)
        qib = lax.dynamic_slice_in_dim(qi4, s0, Q_BLOCK, axis=1)
        wib = lax.dynamic_slice_in_dim(wi, s0, Q_BLOCK, axis=1)
        sc = _index_scores(qib, wib, ki)
        sc = jnp.where((k_pos[None, :] <= q_pos[:, None])[None], sc, -jnp.inf)
        _, idx = lax.top_k(sc, topk)
        dist = q_pos[None, :, None] - idx
        return _c_attend(qb, k4[bidx, idx], v4[bidx, idx], dist, rel_bias)

    o = lax.map(blk, jnp.arange(t // Q_BLOCK, dtype=jnp.int32))
    return jnp.moveaxis(o, 0, 1).reshape(b, t, C_WIDTH)


def _c_sample_mix(q, k, v, qi, ki, wi, cache_k, cache_v, cache_idx, layer, page_table, rel_bias):
    b, t, _ = q.shape
    q5 = q.reshape(b, t, C_KV_HEADS, C_GROUP, C_HEAD_DIM)
    k4 = k.reshape(b, t, C_KV_HEADS, C_HEAD_DIM)
    v4 = v.reshape(b, t, C_KV_HEADS, C_HEAD_DIM)
    qi4 = qi.reshape(b, t, IDX_HEADS, IDX_DIM)
    n_past = page_table.shape[1] * PAGE_SIZE
    ki_past = cache_idx[layer, page_table].reshape(b, n_past, IDX_DIM)
    ki_all = jnp.concatenate([ki_past, ki], axis=1)
    n_keys = n_past + t
    topk = min(TOPK_MAX, n_keys // 4)
    q_pos = n_past + jnp.arange(t, dtype=jnp.int32)
    k_pos = jnp.arange(n_keys, dtype=jnp.int32)
    sc = _index_scores(qi4, wi, ki_all)
    sc = jnp.where((k_pos[None, :] <= q_pos[:, None])[None], sc, -jnp.inf)
    _, idx = lax.top_k(sc, topk)
    bidx = jnp.arange(b)[:, None, None]
    in_past = idx < n_past
    pidx = jnp.minimum(idx, n_past - 1)
    phys = page_table[bidx, pidx // PAGE_SIZE]
    off = pidx % PAGE_SIZE
    nidx = jnp.clip(idx - n_past, 0, t - 1)
    k_sel = jnp.where(in_past[..., None, None], cache_k[layer, phys, off], k4[bidx, nidx])
    v_sel = jnp.where(in_past[..., None, None], cache_v[layer, phys, off], v4[bidx, nidx])
    return _c_attend(q5, k_sel, v_sel, q_pos[None, :, None] - idx, rel_bias)


def _segment_mean_matrix(width):
    seg = np.arange(width) // C_HEAD_DIM
    return jnp.asarray((seg[:, None] == seg[None, :]).astype(np.float32) / C_HEAD_DIM, BF16)


def kernel(x_prompt, x_sample, state_pool, cache_b_k, cache_b_v, cache_c_k, cache_c_v, cache_c_idx, page_table,
           g_ffn, w_ffn_gate, w_ffn_up, w_ffn_down, g_mix, w_in_ab, w_pool, pool_scale, w_out_ab,
           w_in_c, g_q, g_k, w_out_c, rel_bias):
    bp, tp, _ = x_prompt.shape
    bs, ts, _ = x_sample.shape
    mp = bp * tp
    x = jnp.concatenate([x_prompt.reshape(mp, D_MODEL), x_sample.reshape(bs * ts, D_MODEL)], axis=0)

    wg = w_ffn_gate.astype(BF16)
    wu = w_ffn_up.astype(BF16)
    wd = w_ffn_down.astype(BF16)
    w_ab = w_in_ab.astype(BF16)
    w_oab = w_out_ab.astype(BF16)
    w_c = jnp.pad(w_in_c, ((0, 0), (0, 0), (0, C_IN_PAD - C_IN))).astype(BF16)
    w_oc = w_out_c.astype(BF16)
    segq = _segment_mean_matrix(C_WIDTH)
    segk = _segment_mean_matrix(C_KV_WIDTH)

    def split(a):
        return a[:mp].reshape(bp, tp, -1), a[mp:].reshape(bs, ts, -1)

    pool_p, pool_s, kbp, vbp, kbs, vbs = [], [], [], [], [], []
    kcp, vcp, icp, kcs, vcs, ics = [], [], [], [], [], []
    for layer in range(DEPTH):
        j = layer // 2
        x = _ffn(x, g_ffn[layer, 0][None], wg[layer, 0], wu[layer, 0], wd[layer, 0])
        g = g_mix[layer][None]
        if layer % 2 == 0:
            u, q, k, v = _proj_ab(x, g, w_ab[j])
            (u_p, u_s), (q_p, q_s), (k_p, k_s), (v_p, v_s) = split(u), split(q), split(k), split(v)
            y_p, sp = _ab_prompt_mix(u_p, q_p, k_p, v_p, w_pool[j], pool_scale[j])
            kb_past = cache_b_k[j, page_table].reshape(bs, -1, SB_HEADS, SB_HEAD_DIM)
            vb_past = cache_b_v[j, page_table].reshape(bs, -1, SB_HEADS, SB_HEAD_DIM)
            y_s, ss = _ab_sample_mix(u_s, q_s, k_s, v_s, state_pool[j], kb_past, vb_past,
                                     w_pool[j], pool_scale[j])
            pool_p.append(sp); pool_s.append(ss)
            kbp.append(k_p.reshape(bp, tp, SB_HEADS, SB_HEAD_DIM))
            vbp.append(v_p.reshape(bp, tp, SB_HEADS, SB_HEAD_DIM))
            kbs.append(k_s.reshape(bs, ts, SB_HEADS, SB_HEAD_DIM))
            vbs.append(v_s.reshape(bs, ts, SB_HEADS, SB_HEAD_DIM))
            w_out = w_oab[j]
        else:
            gq = jnp.tile(g_q[j], C_HEADS)[None]
            gk = jnp.tile(g_k[j], C_KV_HEADS)[None]
            q, k, v, qi, tail = _proj_c(x, g, w_c[j], gq, gk, segq, segk)
            ki = tail[:, :IDX_DIM]
            wi = tail[:, IDX_DIM:IDX_DIM + IDX_HEADS]
            (q_p, q_s), (k_p, k_s), (v_p, v_s) = split(q), split(k), split(v)
            (qi_p, qi_s), (ki_p, ki_s), (wi_p, wi_s) = split(qi), split(ki), split(wi)
            y_p = _c_prompt_mix(q_p, k_p, v_p, qi_p, ki_p, wi_p, rel_bias)
            y_s = _c_sample_mix(q_s, k_s, v_s, qi_s, ki_s, wi_s, cache_c_k, cache_c_v, cache_c_idx,
                                j, page_table, rel_bias)
            kcp.append(k_p.reshape(bp, tp, C_KV_HEADS, C_HEAD_DIM))
            vcp.append(v_p.reshape(bp, tp, C_KV_HEADS, C_HEAD_DIM))
            icp.append(ki_p)
            kcs.append(k_s.reshape(bs, ts, C_KV_HEADS, C_HEAD_DIM))
            vcs.append(v_s.reshape(bs, ts, C_KV_HEADS, C_HEAD_DIM))
            ics.append(ki_s)
            w_out = w_oc[j]
        y = jnp.concatenate([y_p.reshape(mp, D_MODEL), y_s.reshape(bs * ts, D_MODEL)], axis=0)
        x = _out_proj(x, y, w_out)
        x = _ffn(x, g_ffn[layer, 1][None], wg[layer, 1], wu[layer, 1], wd[layer, 1])

    return (x[:mp].reshape(bp, tp, D_MODEL), x[mp:].reshape(bs, ts, D_MODEL),
            jnp.stack(pool_p), jnp.stack(pool_s),
            jnp.stack(kbp), jnp.stack(vbp), jnp.stack(kbs), jnp.stack(vbs),
            jnp.stack(kcp), jnp.stack(vcp), jnp.stack(icp),
            jnp.stack(kcs), jnp.stack(vcs), jnp.stack(ics))
```

```python
import functools
import math

import jax
import jax.numpy as jnp
import numpy as np
from jax import lax
from jax.experimental import pallas as pl
from jax.experimental.pallas import tpu as pltpu

F32 = jnp.float32
BF16 = jnp.bfloat16

D_MODEL = 1024
DEPTH = 4
D_FF = 2816
POOL_WINDOWS = (2, 4, 8, 16)
POOL_WIDTH = 512
POOL_GROUP = 128
POOL_BUF = 15
SB_HEADS = 8
SB_HEAD_DIM = 64
SB_WIDTH = 512
AB_IN = POOL_WIDTH + 3 * SB_WIDTH
C_HEAD_DIM = 64
C_HEADS = 16
C_KV_HEADS = 4
C_GROUP = 4
C_WIDTH = 1024
C_KV_WIDTH = 256
IDX_HEADS = 8
IDX_DIM = 64
TOPK_MAX = 256
C_IN = 2120
C_IN_PAD = 2176
REL_BUCKETS = 32
REL_MAX_EXACT = 16
REL_MAX_DIST = 128
Q_BLOCK = 128
PAGE_SIZE = 128
EPS = 1e-6
NEG = -1e30

VMEM_LIMIT_BYTES = 56 * 1024 * 1024
FF_CHUNK = 256
TOKEN_TILE = 512


def _rms(x, g):
    ms = jnp.mean(x * x, axis=-1, keepdims=True)
    return x * lax.rsqrt(ms + EPS) * g


def _ffn_kernel(x_ref, g_ref, wg_ref, wu_ref, wd_ref, o_ref):
    x = x_ref[...]
    h = _rms(x, g_ref[...]).astype(BF16)
    acc = x
    for c in range(D_FF // FF_CHUNK):
        sl = slice(c * FF_CHUNK, (c + 1) * FF_CHUNK)
        gate = jnp.dot(h, wg_ref[:, sl], preferred_element_type=F32)
        up = jnp.dot(h, wu_ref[:, sl], preferred_element_type=F32)
        act = (0.5 * gate * jax.nn.sigmoid(gate) * up).astype(BF16)
        acc = acc + jnp.dot(act, wd_ref[sl, :], preferred_element_type=F32)
    o_ref[...] = acc


def _ffn(x, g, wg, wu, wd):
    m = x.shape[0]
    resident = dict(pipeline_mode=pl.Buffered(1))
    return pl.pallas_call(
        _ffn_kernel,
        out_shape=jax.ShapeDtypeStruct((m, D_MODEL), F32),
        grid=(m // TOKEN_TILE,),
        in_specs=[
            pl.BlockSpec((TOKEN_TILE, D_MODEL), lambda i: (i, 0)),
            pl.BlockSpec((1, D_MODEL), lambda i: (0, 0)),
            pl.BlockSpec((D_MODEL, D_FF), lambda i: (0, 0), **resident),
            pl.BlockSpec((D_MODEL, D_FF), lambda i: (0, 0), **resident),
            pl.BlockSpec((D_FF, D_MODEL), lambda i: (0, 0), **resident),
        ],
        out_specs=pl.BlockSpec((TOKEN_TILE, D_MODEL), lambda i: (i, 0)),
        compiler_params=pltpu.CompilerParams(
            dimension_semantics=("parallel",), vmem_limit_bytes=VMEM_LIMIT_BYTES),
        name="ffn_half",
    )(x, g, wg, wu, wd)


def _proj_ab_kernel(x_ref, g_ref, w_ref, u_ref, q_ref, k_ref, v_ref):
    h = _rms(x_ref[...], g_ref[...]).astype(BF16)
    p = jnp.dot(h, w_ref[...], preferred_element_type=F32)
    u_ref[...] = p[:, :POOL_WIDTH]
    q_ref[...] = p[:, POOL_WIDTH:POOL_WIDTH + SB_WIDTH]
    k_ref[...] = p[:, POOL_WIDTH + SB_WIDTH:POOL_WIDTH + 2 * SB_WIDTH]
    v_ref[...] = p[:, POOL_WIDTH + 2 * SB_WIDTH:]


def _proj_ab(x, g, w):
    m = x.shape[0]
    out = jax.ShapeDtypeStruct((m, SB_WIDTH), F32)
    spec = pl.BlockSpec((TOKEN_TILE, SB_WIDTH), lambda i: (i, 0))
    return pl.pallas_call(
        _proj_ab_kernel,
        out_shape=(out, out, out, out),
        grid=(m // TOKEN_TILE,),
        in_specs=[
            pl.BlockSpec((TOKEN_TILE, D_MODEL), lambda i: (i, 0)),
            pl.BlockSpec((1, D_MODEL), lambda i: (0, 0)),
            pl.BlockSpec((D_MODEL, AB_IN), lambda i: (0, 0), pipeline_mode=pl.Buffered(1)),
        ],
        out_specs=(spec, spec, spec, spec),
        compiler_params=pltpu.CompilerParams(
            dimension_semantics=("parallel",), vmem_limit_bytes=VMEM_LIMIT_BYTES),
        name="proj_ab",
    )(x, g, w)


def _split_hi_lo(x):
    hi = x.astype(BF16)
    lo = (x - hi.astype(F32)).astype(BF16)
    return hi, lo


def _head_rms(x, seg_mean_ref, gain):
    hi, lo = _split_hi_lo(x * x)
    ms = (jnp.dot(hi, seg_mean_ref[...], preferred_element_type=F32)
          + jnp.dot(lo, seg_mean_ref[...], preferred_element_type=F32))
    return x * lax.rsqrt(ms + EPS) * gain


def _proj_c_kernel(x_ref, g_ref, w_ref, gq_ref, gk_ref, segq_ref, segk_ref,
                   q_ref, k_ref, v_ref, qi_ref, tail_ref):
    h = _rms(x_ref[...], g_ref[...]).astype(BF16)
    p = jnp.dot(h, w_ref[...], preferred_element_type=F32)
    o_k = C_WIDTH
    o_v = o_k + C_KV_WIDTH
    o_qi = o_v + C_KV_WIDTH
    o_tail = o_qi + IDX_HEADS * IDX_DIM
    q_ref[...] = _head_rms(p[:, :o_k], segq_ref, gq_ref[...])
    k_ref[...] = _head_rms(p[:, o_k:o_v], segk_ref, gk_ref[...])
    v_ref[...] = p[:, o_v:o_qi]
    qi_ref[...] = p[:, o_qi:o_tail]
    tail_ref[...] = p[:, o_tail:]


def _proj_c(x, g, w, gq, gk, segq, segk):
    m = x.shape[0]
    tail = C_IN_PAD - (C_WIDTH + 2 * C_KV_WIDTH + IDX_HEADS * IDX_DIM)
    widths = (C_WIDTH, C_KV_WIDTH, C_KV_WIDTH, IDX_HEADS * IDX_DIM, tail)
    const = lambda i: (0, 0)
    return pl.pallas_call(
        _proj_c_kernel,
        out_shape=tuple(jax.ShapeDtypeStruct((m, n), F32) for n in widths),
        grid=(m // TOKEN_TILE,),
        in_specs=[
            pl.BlockSpec((TOKEN_TILE, D_MODEL), lambda i: (i, 0)),
            pl.BlockSpec((1, D_MODEL), const),
            pl.BlockSpec((D_MODEL, C_IN_PAD), const, pipeline_mode=pl.Buffered(1)),
            pl.BlockSpec((1, C_WIDTH), const),
            pl.BlockSpec((1, C_KV_WIDTH), const),
            pl.BlockSpec((C_WIDTH, C_WIDTH), const, pipeline_mode=pl.Buffered(1)),
            pl.BlockSpec((C_KV_WIDTH, C_KV_WIDTH), const, pipeline_mode=pl.Buffered(1)),
        ],
        out_specs=tuple(pl.BlockSpec((TOKEN_TILE, n), lambda i: (i, 0)) for n in widths),
        compiler_params=pltpu.CompilerParams(
            dimension_semantics=("parallel",), vmem_limit_bytes=VMEM_LIMIT_BYTES),
        name="proj_c",
    )(x, g, w, gq, gk, segq, segk)


def _out_proj_kernel(x_ref, y_ref, w_ref, o_ref):
    o_ref[...] = x_ref[...] + jnp.dot(y_ref[...].astype(BF16), w_ref[...],
                                      preferred_element_type=F32)


def _out_proj(x, y, w):
    m = x.shape[0]
    tile = pl.BlockSpec((TOKEN_TILE, D_MODEL), lambda i: (i, 0))
    return pl.pallas_call(
        _out_proj_kernel,
        out_shape=jax.ShapeDtypeStruct((m, D_MODEL), F32),
        grid=(m // TOKEN_TILE,),
        in_specs=[tile, tile,
                  pl.BlockSpec((D_MODEL, D_MODEL), lambda i: (0, 0), pipeline_mode=pl.Buffered(1))],
        out_specs=tile,
        compiler_params=pltpu.CompilerParams(
            dimension_semantics=("parallel",), vmem_limit_bytes=VMEM_LIMIT_BYTES),
        name="out_proj",
    )(x, y, w)


ATT_TILE = 128
SB_EXIT = -104.0
MXU_DTYPE = BF16


def _split_hi_lo_mxu(x):
    hi = x.astype(MXU_DTYPE)
    lo = (x - hi.astype(F32)).astype(MXU_DTYPE)
    return hi, lo


def _sb_prompt_kernel(qT_ref, k_ref, vT_ref, upper_ref, o_ref):
    tk = tq = ATT_TILE
    qb = pl.program_id(1)
    row = lax.broadcasted_iota(jnp.int32, (tk, tq), 0)
    lane = lax.broadcasted_iota(jnp.int32, (tk, tq), 1)
    q_idx = qb * tq + lane
    upper = upper_ref[...]
    for h in range(SB_HEADS):
        qT = qT_ref[0, h]

        def body(state, h=h, qT=qT):
            j, _, carry, acc = state
            ks = pl.multiple_of(j * tk, tk)
            kb = k_ref[0, h, pl.ds(ks, tk), :]
            z = jnp.dot(kb, qT, preferred_element_type=F32)
            sp = jnp.maximum(z, 0.0) + jnp.log(1.0 + jnp.exp(-jnp.abs(z)))
            valid = (ks + row) < q_idx
            lr = jnp.where(valid, -sp, 0.0)
            hi, lo = _split_hi_lo_mxu(lr)
            between = (jnp.dot(upper, hi, preferred_element_type=F32)
                       + jnp.dot(upper, lo, preferred_element_type=F32) + carry)
            a = jnp.where(valid, jnp.exp(z - sp + between), 0.0)
            vb = vT_ref[0, h, :, pl.ds(ks, tk)]
            acc = acc + jnp.dot(vb, a.astype(MXU_DTYPE), preferred_element_type=F32)
            carry = carry + jnp.sum(lr, axis=0, keepdims=True)
            return j - 1, jnp.max(carry), carry, acc

        def cond(state):
            j, top, _, _ = state
            return jnp.logical_and(j >= 0, top > SB_EXIT)

        init = (qb, jnp.float32(0.0), jnp.zeros((1, tq), F32), jnp.zeros((SB_HEAD_DIM, tq), F32))
        _, _, _, acc = lax.while_loop(cond, body, init)
        o_ref[0, h] = acc


def _strict_upper(n):
    i = np.arange(n)
    return jnp.asarray((i[None, :] > i[:, None]).astype(np.float32), MXU_DTYPE)


def _strict_lower(n):
    i = np.arange(n)
    return jnp.asarray((i[None, :] < i[:, None]).astype(np.float32), MXU_DTYPE)


def _sb_prompt(q, k, v):
    b, t, _ = q.shape
    scale = SB_HEAD_DIM ** -0.5
    qT = (q * scale).reshape(b, t, SB_HEADS, SB_HEAD_DIM).transpose(0, 2, 3, 1).astype(MXU_DTYPE)
    kh = k.reshape(b, t, SB_HEADS, SB_HEAD_DIM).transpose(0, 2, 1, 3).astype(MXU_DTYPE)
    vT = v.reshape(b, t, SB_HEADS, SB_HEAD_DIM).transpose(0, 2, 3, 1).astype(MXU_DTYPE)
    oT = pl.pallas_call(
        _sb_prompt_kernel,
        out_shape=jax.ShapeDtypeStruct((b, SB_HEADS, SB_HEAD_DIM, t), F32),
        grid=(b, t // ATT_TILE),
        in_specs=[
            pl.BlockSpec((1, SB_HEADS, SB_HEAD_DIM, ATT_TILE), lambda i, j: (i, 0, 0, j)),
            pl.BlockSpec((1, SB_HEADS, t, SB_HEAD_DIM), lambda i, j: (i, 0, 0, 0)),
            pl.BlockSpec((1, SB_HEADS, SB_HEAD_DIM, t), lambda i, j: (i, 0, 0, 0)),
            pl.BlockSpec((ATT_TILE, ATT_TILE), lambda i, j: (0, 0)),
        ],
        out_specs=pl.BlockSpec((1, SB_HEADS, SB_HEAD_DIM, ATT_TILE), lambda i, j: (i, 0, 0, j)),
        compiler_params=pltpu.CompilerParams(
            dimension_semantics=("parallel", "arbitrary"), vmem_limit_bytes=VMEM_LIMIT_BYTES),
        name="sb_prompt",
    )(qT, kh, vT, _strict_upper(ATT_TILE))
    return oT.transpose(0, 3, 1, 2).reshape(b, t, SB_WIDTH)


INT_MIN = -2 ** 31
COUNT_CHUNK = 4 * ATT_TILE
REL_NEAR_TILES = 2


def _bias_tile_kernel(relb_ref, o_ref):
    t = ATT_TILE
    row = lax.broadcasted_iota(jnp.int32, (t, t), 0)
    lane = lax.broadcasted_iota(jnp.int32, (t, t), 1)
    for c in range(REL_NEAR_TILES):
        n = jnp.maximum(c * t + lane - row, 0)
        nf = jnp.maximum(n, 1).astype(F32)
        large = REL_MAX_EXACT + (jnp.log(nf / REL_MAX_EXACT) / math.log(REL_MAX_DIST / REL_MAX_EXACT)
                                 * (REL_BUCKETS - REL_MAX_EXACT)).astype(jnp.int32)
        bucket = jnp.where(n < REL_MAX_EXACT, n, jnp.minimum(large, REL_BUCKETS - 1))
        for h in range(C_HEADS):
            tile = jnp.zeros((t, t), F32)
            for b in range(REL_BUCKETS):
                tile = jnp.where(bucket == b, relb_ref[b, h], tile)
            o_ref[h, c] = tile


def _bias_tiles(rel_bias):
    return pl.pallas_call(
        _bias_tile_kernel,
        out_shape=jax.ShapeDtypeStruct((C_HEADS, REL_NEAR_TILES, ATT_TILE, ATT_TILE), F32),
        in_specs=[pl.BlockSpec(memory_space=pltpu.SMEM)],
        name="rel_bias_tiles",
    )(rel_bias)


def _sortable(s):
    bits = lax.bitcast_convert_type(s, jnp.int32)
    return bits ^ ((bits >> 31) & 0x7FFFFFFF)


def _c_prompt_kernel(topk, relb_ref, qiT_ref, wiT_ref, ki_ref, qT_ref, k_ref, vT_ref, btile_ref,
                     lower_ref, o_ref, key_ref, m_ref, l_ref, acc_ref):
    tk = tq = ATT_TILE
    qb = pl.program_id(1)
    n_chunks = qb // (COUNT_CHUNK // tk) + 1
    row = lax.broadcasted_iota(jnp.int32, (tk, tq), 0)
    lane = lax.broadcasted_iota(jnp.int32, (tk, tq), 1)
    q_idx = qb * tq + lane
    w = wiT_ref[0] * (IDX_HEADS ** -0.5)

    def score_block(j, _):
        ks = pl.multiple_of(j * tk, tk)
        kib = ki_ref[0, pl.ds(ks, tk), :]
        s = jnp.zeros((tk, tq), F32)
        for hh in range(IDX_HEADS):
            d = jnp.dot(kib, qiT_ref[0, hh], preferred_element_type=F32)
            s = s + jnp.maximum(d, 0.0) * w[hh:hh + 1, :]
        s = jnp.where(ks + row <= q_idx, s, -jnp.inf)
        key_ref[pl.ds(ks, tk), :] = _sortable(s)
        return 0

    lax.fori_loop(0, n_chunks * (COUNT_CHUNK // tk), score_block, 0)

    def count(pred, thr):
        def chunk(c, cnt):
            base = pl.multiple_of(c * COUNT_CHUNK, COUNT_CHUNK)
            ind = jnp.where(pred(key_ref[pl.ds(base, COUNT_CHUNK), :], thr), 1, 0)
            return cnt + jnp.sum(ind.reshape(COUNT_CHUNK // 8, 8, tq), axis=0)
        cnt = lax.fori_loop(0, n_chunks, chunk, jnp.zeros((8, tq), jnp.int32))
        return jnp.sum(cnt, axis=0, keepdims=True)

    def bit_step(i, t_u):
        cand_u = t_u | lax.shift_left(jnp.int32(1), 31 - i)
        cnt = count(lambda x, thr: x >= thr, cand_u ^ INT_MIN)
        return jnp.where(cnt >= topk, cand_u, t_u)

    thr = lax.fori_loop(0, 32, bit_step, jnp.zeros((1, tq), jnp.int32)) ^ INT_MIN
    need = (topk - count(lambda x, t: x > t, thr)).astype(F32)

    def select_block(j, seen):
        ks = pl.multiple_of(j * tk, tk)
        blk = key_ref[pl.ds(ks, tk), :]
        eq = jnp.where(blk == thr, 1.0, 0.0)
        rank = jnp.dot(lower_ref[...], eq.astype(MXU_DTYPE), preferred_element_type=F32) + seen
        tie = jnp.where(rank < need, eq, 0.0)
        sel = jnp.where(blk > thr, 1.0, tie)
        sel = jnp.where(ks + row <= q_idx, sel, 0.0)
        key_ref[pl.ds(ks, tk), :] = sel.astype(jnp.int32)
        return seen + jnp.sum(eq, axis=0, keepdims=True)

    lax.fori_loop(0, qb + 1, select_block, jnp.zeros((1, tq), F32))

    m_ref[...] = jnp.full(m_ref.shape, NEG, F32)
    l_ref[...] = jnp.zeros(l_ref.shape, F32)
    acc_ref[...] = jnp.zeros(acc_ref.shape, F32)

    def attend(j, bias_of_head):
        ks = pl.multiple_of(j * tk, tk)
        sel = key_ref[pl.ds(ks, tk), :] != 0
        for n in range(C_KV_HEADS):
            kb = k_ref[0, n, pl.ds(ks, tk), :]
            vb = vT_ref[0, n, :, pl.ds(ks, tk)]
            for g in range(C_GROUP):
                h = n * C_GROUP + g
                lg = jnp.dot(kb, qT_ref[0, h], preferred_element_type=F32) + bias_of_head(h)
                lg = jnp.where(sel, lg, NEG)
                m_old = m_ref[h]
                m_new = jnp.maximum(m_old, jnp.max(lg, axis=0, keepdims=True))
                alpha = jnp.exp(m_old - m_new)
                p = jnp.exp(lg - m_new)
                l_ref[h] = alpha * l_ref[h] + jnp.sum(p, axis=0, keepdims=True)
                acc_ref[h] = alpha * acc_ref[h] + jnp.dot(vb, p.astype(MXU_DTYPE),
                                                          preferred_element_type=F32)
                m_ref[h] = m_new

    def far_block(j, _):
        attend(j, lambda h: relb_ref[REL_BUCKETS - 1, h])
        return 0

    lax.fori_loop(0, jnp.maximum(qb - 1, 0), far_block, 0)

    @pl.when(qb >= 1)
    def _():
        attend(qb - 1, lambda h: btile_ref[h, 1])

    attend(qb, lambda h: btile_ref[h, 0])
    for h in range(C_HEADS):
        o_ref[0, h] = acc_ref[h] / l_ref[h]


def _c_prompt(q, k, v, qi, ki, wi, rel_bias, btiles):
    b, t, _ = q.shape
    topk = min(TOPK_MAX, t // 4)
    assert t % COUNT_CHUNK == 0 and COUNT_CHUNK >= topk
    qT = (q * C_HEAD_DIM ** -0.5).reshape(b, t, C_HEADS, C_HEAD_DIM).transpose(0, 2, 3, 1).astype(MXU_DTYPE)
    kh = k.reshape(b, t, C_KV_HEADS, C_HEAD_DIM).transpose(0, 2, 1, 3).astype(MXU_DTYPE)
    vT = v.reshape(b, t, C_KV_HEADS, C_HEAD_DIM).transpose(0, 2, 3, 1).astype(MXU_DTYPE)
    qi_hi, qi_lo = _split_hi_lo_mxu(
        (qi * IDX_DIM ** -0.5).reshape(b, t, IDX_HEADS, IDX_DIM).transpose(0, 2, 3, 1))
    qiT = jnp.concatenate([qi_hi, qi_lo, qi_hi], axis=2)
    ki_hi, ki_lo = _split_hi_lo_mxu(ki)
    ki3 = jnp.concatenate([ki_hi, ki_hi, ki_lo], axis=2)
    wiT = wi.transpose(0, 2, 1)
    tile = ATT_TILE
    oT = pl.pallas_call(
        functools.partial(_c_prompt_kernel, topk),
        out_shape=jax.ShapeDtypeStruct((b, C_HEADS, C_HEAD_DIM, t), F32),
        grid=(b, t // tile),
        in_specs=[
            pl.BlockSpec(memory_space=pltpu.SMEM),
            pl.BlockSpec((1, IDX_HEADS, 3 * IDX_DIM, tile), lambda i, j: (i, 0, 0, j)),
            pl.BlockSpec((1, IDX_HEADS, tile), lambda i, j: (i, 0, j)),
            pl.BlockSpec((1, t, 3 * IDX_DIM), lambda i, j: (i, 0, 0)),
            pl.BlockSpec((1, C_HEADS, C_HEAD_DIM, tile), lambda i, j: (i, 0, 0, j)),
            pl.BlockSpec((1, C_KV_HEADS, t, C_HEAD_DIM), lambda i, j: (i, 0, 0, 0)),
            pl.BlockSpec((1, C_KV_HEADS, C_HEAD_DIM, t), lambda i, j: (i, 0, 0, 0)),
            pl.BlockSpec((C_HEADS, REL_NEAR_TILES, tile, tile), lambda i, j: (0, 0, 0, 0)),
            pl.BlockSpec((tile, tile), lambda i, j: (0, 0)),
        ],
        out_specs=pl.BlockSpec((1, C_HEADS, C_HEAD_DIM, tile), lambda i, j: (i, 0, 0, j)),
        scratch_shapes=[
            pltpu.VMEM((t, tile), jnp.int32),
            pltpu.VMEM((C_HEADS, 1, tile), F32),
            pltpu.VMEM((C_HEADS, 1, tile), F32),
            pltpu.VMEM((C_HEADS, C_HEAD_DIM, tile), F32),
        ],
        compiler_params=pltpu.CompilerParams(
            dimension_semantics=("parallel", "arbitrary"), vmem_limit_bytes=VMEM_LIMIT_BYTES),
        name="c_prompt",
    )(rel_bias, qiT, wiT, ki3, qT, kh, vT, btiles, _strict_lower(tile))
    return oT.transpose(0, 3, 1, 2).reshape(b, t, C_WIDTH)


def _t5_bucket(dist):
    n = jnp.maximum(dist, 0)
    nf = jnp.maximum(n, 1).astype(F32)
    large = REL_MAX_EXACT + (jnp.log(nf / REL_MAX_EXACT) / math.log(REL_MAX_DIST / REL_MAX_EXACT)
                             * (REL_BUCKETS - REL_MAX_EXACT)).astype(jnp.int32)
    large = jnp.minimum(large, REL_BUCKETS - 1)
    return jnp.where(n < REL_MAX_EXACT, n, large)


def _pool_mix(u_ext, pos, w_pool, pool_scale):
    b, l, c = u_ext.shape
    t = l - POOL_BUF
    cs = jnp.concatenate([jnp.zeros((b, 1, c), F32), jnp.cumsum(u_ext, axis=1)], axis=1)
    u_new = u_ext[:, POOL_BUF:]
    hi = cs[:, POOL_BUF + 1:]
    outs = []
    for g, w in enumerate(POOL_WINDOWS):
        sl = slice(g * POOL_GROUP, (g + 1) * POOL_GROUP)
        lo = cs[:, POOL_BUF + 1 - w:POOL_BUF + 1 - w + t, sl]
        cnt = jnp.minimum(pos + 1, w).astype(F32)[None, :, None]
        outs.append((hi[..., sl] - lo) / cnt - u_new[..., sl])
    d = jnp.stack(outs, axis=2)
    y = jnp.einsum('btgc,gcd->btgd', d, w_pool).reshape(b, t, POOL_WIDTH)
    return y * pool_scale


def _stick_breaking(q, k, v, q_pos, k_pos):
    z = jnp.einsum('bqhd,bkhd->bhqk', q, k) * (SB_HEAD_DIM ** -0.5)
    mask = (k_pos[None, :] < q_pos[:, None])[None, None]
    log_rest = jnp.where(mask, jax.nn.log_sigmoid(-z), 0.0)
    between = lax.cumsum(log_rest, axis=3, reverse=True) - log_rest
    a = jnp.where(mask, jnp.exp(jax.nn.log_sigmoid(z) + between), 0.0)
    return jnp.einsum('bhqk,bkhd->bqhd', a, v)


def _index_scores(qi, wi, ki):
    s = jnp.einsum('bqhd,bkd->bqhk', qi, ki) * (IDX_DIM ** -0.5)
    return jnp.einsum('bqhk,bqh->bqk', jax.nn.relu(s), wi * (IDX_HEADS ** -0.5))


def _c_attend(q, k_sel, v_sel, dist, rel_bias):
    logits = jnp.einsum('bqngd,bqknd->bqngk', q, k_sel) * (C_HEAD_DIM ** -0.5)
    bias = rel_bias[_t5_bucket(dist)]
    b, tq, kk, _ = bias.shape
    bias = bias.reshape(b, tq, kk, C_KV_HEADS, C_GROUP).transpose(0, 1, 3, 4, 2)
    valid = (dist >= 0)[:, :, None, None, :]
    logits = jnp.where(valid, logits + bias, NEG)
    p = jax.nn.softmax(logits, axis=-1)
    o = jnp.einsum('bqngk,bqknd->bqngd', p, v_sel)
    return o.reshape(b, tq, C_WIDTH)


def _ab_prompt_mix(u, q, k, v, w_pool, pool_scale):
    b, t, _ = u.shape
    pos = jnp.arange(t, dtype=jnp.int32)
    u_ext = jnp.concatenate([jnp.zeros((b, POOL_BUF, POOL_WIDTH), F32), u], axis=1)
    y_pool = _pool_mix(u_ext, pos, w_pool, pool_scale)
    y_sb = _sb_prompt(q, k, v)
    return jnp.concatenate([y_pool, y_sb], axis=-1), u_ext[:, -POOL_BUF:]


def _ab_sample_mix(u, q, k, v, pool_buf, k_past, v_past, w_pool, pool_scale):
    b, t, _ = u.shape
    n_past = k_past.shape[1]
    q_pos = n_past + jnp.arange(t, dtype=jnp.int32)
    u_ext = jnp.concatenate([pool_buf, u], axis=1)
    y_pool = _pool_mix(u_ext, q_pos, w_pool, pool_scale)
    q4 = q.reshape(b, t, SB_HEADS, SB_HEAD_DIM)
    k4 = k.reshape(b, t, SB_HEADS, SB_HEAD_DIM)
    v4 = v.reshape(b, t, SB_HEADS, SB_HEAD_DIM)
    k_all = jnp.concatenate([k_past, k4], axis=1)
    v_all = jnp.concatenate([v_past, v4], axis=1)
    k_pos = jnp.arange(n_past + t, dtype=jnp.int32)
    y_sb = _stick_breaking(q4, k_all, v_all, q_pos, k_pos).reshape(b, t, SB_WIDTH)
    return jnp.concatenate([y_pool, y_sb], axis=-1), u_ext[:, -POOL_BUF:]


def _c_sample_mix(q, k, v, qi, ki, wi, cache_k, cache_v, cache_idx, layer, page_table, rel_bias):
    b, t, _ = q.shape
    q5 = q.reshape(b, t, C_KV_HEADS, C_GROUP, C_HEAD_DIM)
    k4 = k.reshape(b, t, C_KV_HEADS, C_HEAD_DIM)
    v4 = v.reshape(b, t, C_KV_HEADS, C_HEAD_DIM)
    qi4 = qi.reshape(b, t, IDX_HEADS, IDX_DIM)
    n_past = page_table.shape[1] * PAGE_SIZE
    ki_past = cache_idx[layer, page_table].reshape(b, n_past, IDX_DIM)
    ki_all = jnp.concatenate([ki_past, ki], axis=1)
    n_keys = n_past + t
    topk = min(TOPK_MAX, n_keys // 4)
    q_pos = n_past + jnp.arange(t, dtype=jnp.int32)
    k_pos = jnp.arange(n_keys, dtype=jnp.int32)
    sc = _index_scores(qi4, wi, ki_all)
    sc = jnp.where((k_pos[None, :] <= q_pos[:, None])[None], sc, -jnp.inf)
    _, idx = lax.top_k(sc, topk)
    bidx = jnp.arange(b)[:, None, None]
    in_past = idx < n_past
    pidx = jnp.minimum(idx, n_past - 1)
    phys = page_table[bidx, pidx // PAGE_SIZE]
    off = pidx % PAGE_SIZE
    nidx = jnp.clip(idx - n_past, 0, t - 1)
    k_sel = jnp.where(in_past[..., None, None], cache_k[layer, phys, off], k4[bidx, nidx])
    v_sel = jnp.where(in_past[..., None, None], cache_v[layer, phys, off], v4[bidx, nidx])
    return _c_attend(q5, k_sel, v_sel, q_pos[None, :, None] - idx, rel_bias)


def _segment_mean_matrix(width):
    seg = np.arange(width) // C_HEAD_DIM
    return jnp.asarray((seg[:, None] == seg[None, :]).astype(np.float32) / C_HEAD_DIM, BF16)


def kernel(x_prompt, x_sample, state_pool, cache_b_k, cache_b_v, cache_c_k, cache_c_v, cache_c_idx, page_table,
           g_ffn, w_ffn_gate, w_ffn_up, w_ffn_down, g_mix, w_in_ab, w_pool, pool_scale, w_out_ab,
           w_in_c, g_q, g_k, w_out_c, rel_bias):
    bp, tp, _ = x_prompt.shape
    bs, ts, _ = x_sample.shape
    mp = bp * tp
    x = jnp.concatenate([x_prompt.reshape(mp, D_MODEL), x_sample.reshape(bs * ts, D_MODEL)], axis=0)

    wg = w_ffn_gate.astype(BF16)
    wu = w_ffn_up.astype(BF16)
    wd = w_ffn_down.astype(BF16)
    w_ab = w_in_ab.astype(BF16)
    w_oab = w_out_ab.astype(BF16)
    w_c = jnp.pad(w_in_c, ((0, 0), (0, 0), (0, C_IN_PAD - C_IN))).astype(BF16)
    w_oc = w_out_c.astype(BF16)
    segq = _segment_mean_matrix(C_WIDTH)
    segk = _segment_mean_matrix(C_KV_WIDTH)
    btiles = _bias_tiles(rel_bias)

    def split(a):
        return a[:mp].reshape(bp, tp, -1), a[mp:].reshape(bs, ts, -1)

    pool_p, pool_s, kbp, vbp, kbs, vbs = [], [], [], [], [], []
    kcp, vcp, icp, kcs, vcs, ics = [], [], [], [], [], []
    for layer in range(DEPTH):
        j = layer // 2
        x = _ffn(x, g_ffn[layer, 0][None], wg[layer, 0], wu[layer, 0], wd[layer, 0])
        g = g_mix[layer][None]
        if layer % 2 == 0:
            u, q, k, v = _proj_ab(x, g, w_ab[j])
            (u_p, u_s), (q_p, q_s), (k_p, k_s), (v_p, v_s) = split(u), split(q), split(k), split(v)
            y_p, sp = _ab_prompt_mix(u_p, q_p, k_p, v_p, w_pool[j], pool_scale[j])
            kb_past = cache_b_k[j, page_table].reshape(bs, -1, SB_HEADS, SB_HEAD_DIM)
            vb_past = cache_b_v[j, page_table].reshape(bs, -1, SB_HEADS, SB_HEAD_DIM)
            y_s, ss = _ab_sample_mix(u_s, q_s, k_s, v_s, state_pool[j], kb_past, vb_past,
                                     w_pool[j], pool_scale[j])
            pool_p.append(sp); pool_s.append(ss)
            kbp.append(k_p.reshape(bp, tp, SB_HEADS, SB_HEAD_DIM))
            vbp.append(v_p.reshape(bp, tp, SB_HEADS, SB_HEAD_DIM))
            kbs.append(k_s.reshape(bs, ts, SB_HEADS, SB_HEAD_DIM))
            vbs.append(v_s.reshape(bs, ts, SB_HEADS, SB_HEAD_DIM))
            w_out = w_oab[j]
        else:
            gq = jnp.tile(g_q[j], C_HEADS)[None]
            gk = jnp.tile(g_k[j], C_KV_HEADS)[None]
            q, k, v, qi, tail = _proj_c(x, g, w_c[j], gq, gk, segq, segk)
            ki = tail[:, :IDX_DIM]
            wi = tail[:, IDX_DIM:IDX_DIM + IDX_HEADS]
            (q_p, q_s), (k_p, k_s), (v_p, v_s) = split(q), split(k), split(v)
            (qi_p, qi_s), (ki_p, ki_s), (wi_p, wi_s) = split(qi), split(ki), split(wi)
            y_p = _c_prompt(q_p, k_p, v_p, qi_p, ki_p, wi_p, rel_bias, btiles)
            y_s = _c_sample_mix(q_s, k_s, v_s, qi_s, ki_s, wi_s, cache_c_k, cache_c_v, cache_c_idx,
                                j, page_table, rel_bias)
            kcp.append(k_p.reshape(bp, tp, C_KV_HEADS, C_HEAD_DIM))
            vcp.append(v_p.reshape(bp, tp, C_KV_HEADS, C_HEAD_DIM))
            icp.append(ki_p)
            kcs.append(k_s.reshape(bs, ts, C_KV_HEADS, C_HEAD_DIM))
            vcs.append(v_s.reshape(bs, ts, C_KV_HEADS, C_HEAD_DIM))
            ics.append(ki_s)
            w_out = w_oc[j]
        y = jnp.concatenate([y_p.reshape(mp, D_MODEL), y_s.reshape(bs * ts, D_MODEL)], axis=0)
        x = _out_proj(x, y, w_out)
        x = _ffn(x, g_ffn[layer, 1][None], wg[layer, 1], wu[layer, 1], wd[layer, 1])

    return (x[:mp].reshape(bp, tp, D_MODEL), x[mp:].reshape(bs, ts, D_MODEL),
            jnp.stack(pool_p), jnp.stack(pool_s),
            jnp.stack(kbp), jnp.stack(vbp), jnp.stack(kbs), jnp.stack(vbs),
            jnp.stack(kcp), jnp.stack(vcp), jnp.stack(icp),
            jnp.stack(kcs), jnp.stack(vcs), jnp.stack(ics))
```

```python
import functools
import math

import jax
import jax.numpy as jnp
import numpy as np
from jax import lax
from jax.experimental import pallas as pl
from jax.experimental.pallas import tpu as pltpu

F32 = jnp.float32
BF16 = jnp.bfloat16

D_MODEL = 1024
DEPTH = 4
D_FF = 2816
POOL_WINDOWS = (2, 4, 8, 16)
POOL_WIDTH = 512
POOL_GROUP = 128
POOL_BUF = 15
SB_HEADS = 8
SB_HEAD_DIM = 64
SB_WIDTH = 512
AB_IN = POOL_WIDTH + 3 * SB_WIDTH
C_HEAD_DIM = 64
C_HEADS = 16
C_KV_HEADS = 4
C_GROUP = 4
C_WIDTH = 1024
C_KV_WIDTH = 256
IDX_HEADS = 8
IDX_DIM = 64
TOPK_MAX = 256
C_IN = 2120
C_IN_PAD = 2176
REL_BUCKETS = 32
REL_MAX_EXACT = 16
REL_MAX_DIST = 128
Q_BLOCK = 128
PAGE_SIZE = 128
EPS = 1e-6
NEG = -1e30

VMEM_LIMIT_BYTES = 56 * 1024 * 1024
FF_CHUNK = 256
TOKEN_TILE = 512


def _rms(x, g):
    ms = jnp.mean(x * x, axis=-1, keepdims=True)
    return x * lax.rsqrt(ms + EPS) * g


def _ffn_kernel(x_ref, g_ref, wg_ref, wu_ref, wd_ref, o_ref):
    x = x_ref[...]
    h = _rms(x, g_ref[...]).astype(BF16)
    acc = x
    for c in range(D_FF // FF_CHUNK):
        sl = slice(c * FF_CHUNK, (c + 1) * FF_CHUNK)
        gate = jnp.dot(h, wg_ref[:, sl], preferred_element_type=F32)
        up = jnp.dot(h, wu_ref[:, sl], preferred_element_type=F32)
        act = (0.5 * gate * jax.nn.sigmoid(gate) * up).astype(BF16)
        acc = acc + jnp.dot(act, wd_ref[sl, :], preferred_element_type=F32)
    o_ref[...] = acc


def _ffn(x, g, wg, wu, wd):
    m = x.shape[0]
    resident = dict(pipeline_mode=pl.Buffered(1))
    return pl.pallas_call(
        _ffn_kernel,
        out_shape=jax.ShapeDtypeStruct((m, D_MODEL), F32),
        grid=(m // TOKEN_TILE,),
        in_specs=[
            pl.BlockSpec((TOKEN_TILE, D_MODEL), lambda i: (i, 0)),
            pl.BlockSpec((1, D_MODEL), lambda i: (0, 0)),
            pl.BlockSpec((D_MODEL, D_FF), lambda i: (0, 0), **resident),
            pl.BlockSpec((D_MODEL, D_FF), lambda i: (0, 0), **resident),
            pl.BlockSpec((D_FF, D_MODEL), lambda i: (0, 0), **resident),
        ],
        out_specs=pl.BlockSpec((TOKEN_TILE, D_MODEL), lambda i: (i, 0)),
        compiler_params=pltpu.CompilerParams(
            dimension_semantics=("parallel",), vmem_limit_bytes=VMEM_LIMIT_BYTES),
        name="ffn_half",
    )(x, g, wg, wu, wd)


def _proj_ab_kernel(x_ref, g_ref, w_ref, u_ref, q_ref, k_ref, v_ref):
    h = _rms(x_ref[...], g_ref[...]).astype(BF16)
    p = jnp.dot(h, w_ref[...], preferred_element_type=F32)
    u_ref[...] = p[:, :POOL_WIDTH]
    q_ref[...] = p[:, POOL_WIDTH:POOL_WIDTH + SB_WIDTH]
    k_ref[...] = p[:, POOL_WIDTH + SB_WIDTH:POOL_WIDTH + 2 * SB_WIDTH]
    v_ref[...] = p[:, POOL_WIDTH + 2 * SB_WIDTH:]


def _proj_ab(x, g, w):
    m = x.shape[0]
    out = jax.ShapeDtypeStruct((m, SB_WIDTH), F32)
    spec = pl.BlockSpec((TOKEN_TILE, SB_WIDTH), lambda i: (i, 0))
    return pl.pallas_call(
        _proj_ab_kernel,
        out_shape=(out, out, out, out),
        grid=(m // TOKEN_TILE,),
        in_specs=[
            pl.BlockSpec((TOKEN_TILE, D_MODEL), lambda i: (i, 0)),
            pl.BlockSpec((1, D_MODEL), lambda i: (0, 0)),
            pl.BlockSpec((D_MODEL, AB_IN), lambda i: (0, 0), pipeline_mode=pl.Buffered(1)),
        ],
        out_specs=(spec, spec, spec, spec),
        compiler_params=pltpu.CompilerParams(
            dimension_semantics=("parallel",), vmem_limit_bytes=VMEM_LIMIT_BYTES),
        name="proj_ab",
    )(x, g, w)


def _split_hi_lo(x):
    hi = x.astype(BF16)
    lo = (x - hi.astype(F32)).astype(BF16)
    return hi, lo


def _head_rms(x, seg_mean_ref, gain):
    hi, lo = _split_hi_lo(x * x)
    ms = (jnp.dot(hi, seg_mean_ref[...], preferred_element_type=F32)
          + jnp.dot(lo, seg_mean_ref[...], preferred_element_type=F32))
    return x * lax.rsqrt(ms + EPS) * gain


def _proj_c_kernel(x_ref, g_ref, w_ref, gq_ref, gk_ref, segq_ref, segk_ref,
                   q_ref, k_ref, v_ref, qi_ref, tail_ref):
    h = _rms(x_ref[...], g_ref[...]).astype(BF16)
    p = jnp.dot(h, w_ref[...], preferred_element_type=F32)
    o_k = C_WIDTH
    o_v = o_k + C_KV_WIDTH
    o_qi = o_v + C_KV_WIDTH
    o_tail = o_qi + IDX_HEADS * IDX_DIM
    q_ref[...] = _head_rms(p[:, :o_k], segq_ref, gq_ref[...])
    k_ref[...] = _head_rms(p[:, o_k:o_v], segk_ref, gk_ref[...])
    v_ref[...] = p[:, o_v:o_qi]
    qi_ref[...] = p[:, o_qi:o_tail]
    tail_ref[...] = p[:, o_tail:]


def _proj_c(x, g, w, gq, gk, segq, segk):
    m = x.shape[0]
    tail = C_IN_PAD - (C_WIDTH + 2 * C_KV_WIDTH + IDX_HEADS * IDX_DIM)
    widths = (C_WIDTH, C_KV_WIDTH, C_KV_WIDTH, IDX_HEADS * IDX_DIM, tail)
    const = lambda i: (0, 0)
    return pl.pallas_call(
        _proj_c_kernel,
        out_shape=tuple(jax.ShapeDtypeStruct((m, n), F32) for n in widths),
        grid=(m // TOKEN_TILE,),
        in_specs=[
            pl.BlockSpec((TOKEN_TILE, D_MODEL), lambda i: (i, 0)),
            pl.BlockSpec((1, D_MODEL), const),
            pl.BlockSpec((D_MODEL, C_IN_PAD), const, pipeline_mode=pl.Buffered(1)),
            pl.BlockSpec((1, C_WIDTH), const),
            pl.BlockSpec((1, C_KV_WIDTH), const),
            pl.BlockSpec((C_WIDTH, C_WIDTH), const, pipeline_mode=pl.Buffered(1)),
            pl.BlockSpec((C_KV_WIDTH, C_KV_WIDTH), const, pipeline_mode=pl.Buffered(1)),
        ],
        out_specs=tuple(pl.BlockSpec((TOKEN_TILE, n), lambda i: (i, 0)) for n in widths),
        compiler_params=pltpu.CompilerParams(
            dimension_semantics=("parallel",), vmem_limit_bytes=VMEM_LIMIT_BYTES),
        name="proj_c",
    )(x, g, w, gq, gk, segq, segk)


def _out_proj_kernel(x_ref, y_ref, w_ref, o_ref):
    o_ref[...] = x_ref[...] + jnp.dot(y_ref[...].astype(BF16), w_ref[...],
                                      preferred_element_type=F32)


def _out_proj(x, y, w):
    m = x.shape[0]
    tile = pl.BlockSpec((TOKEN_TILE, D_MODEL), lambda i: (i, 0))
    return pl.pallas_call(
        _out_proj_kernel,
        out_shape=jax.ShapeDtypeStruct((m, D_MODEL), F32),
        grid=(m // TOKEN_TILE,),
        in_specs=[tile, tile,
                  pl.BlockSpec((D_MODEL, D_MODEL), lambda i: (0, 0), pipeline_mode=pl.Buffered(1))],
        out_specs=tile,
        compiler_params=pltpu.CompilerParams(
            dimension_semantics=("parallel",), vmem_limit_bytes=VMEM_LIMIT_BYTES),
        name="out_proj",
    )(x, y, w)


ATT_TILE = 128
SB_EXIT = -104.0
MXU_DTYPE = BF16


def _split_hi_lo_mxu(x):
    hi = x.astype(MXU_DTYPE)
    lo = (x - hi.astype(F32)).astype(MXU_DTYPE)
    return hi, lo


def _sb_prompt_kernel(qT_ref, k_ref, vT_ref, upper_ref, o_ref):
    tk = tq = ATT_TILE
    qb = pl.program_id(1)
    row = lax.broadcasted_iota(jnp.int32, (tk, tq), 0)
    lane = lax.broadcasted_iota(jnp.int32, (tk, tq), 1)
    q_idx = qb * tq + lane
    upper = upper_ref[...]
    for h in range(SB_HEADS):
        qT = qT_ref[0, h]

        def body(state, h=h, qT=qT):
            j, _, carry, acc = state
            ks = pl.multiple_of(j * tk, tk)
            kb = k_ref[0, h, pl.ds(ks, tk), :]
            z = jnp.dot(kb, qT, preferred_element_type=F32)
            sp = jnp.maximum(z, 0.0) + jnp.log(1.0 + jnp.exp(-jnp.abs(z)))
            valid = (ks + row) < q_idx
            lr = jnp.where(valid, -sp, 0.0)
            hi, lo = _split_hi_lo_mxu(lr)
            between = (jnp.dot(upper, hi, preferred_element_type=F32)
                       + jnp.dot(upper, lo, preferred_element_type=F32) + carry)
            a = jnp.where(valid, jnp.exp(z - sp + between), 0.0)
            vb = vT_ref[0, h, :, pl.ds(ks, tk)]
            acc = acc + jnp.dot(vb, a.astype(MXU_DTYPE), preferred_element_type=F32)
            carry = carry + jnp.sum(lr, axis=0, keepdims=True)
            return j - 1, jnp.max(carry), carry, acc

        def cond(state):
            j, top, _, _ = state
            return jnp.logical_and(j >= 0, top > SB_EXIT)

        init = (qb, jnp.float32(0.0), jnp.zeros((1, tq), F32), jnp.zeros((SB_HEAD_DIM, tq), F32))
        _, _, _, acc = lax.while_loop(cond, body, init)
        o_ref[0, h] = acc


def _strict_upper(n):
    i = np.arange(n)
    return jnp.asarray((i[None, :] > i[:, None]).astype(np.float32), MXU_DTYPE)


def _strict_lower(n):
    i = np.arange(n)
    return jnp.asarray((i[None, :] < i[:, None]).astype(np.float32), MXU_DTYPE)


def _sb_prompt(q, k, v):
    b, t, _ = q.shape
    scale = SB_HEAD_DIM ** -0.5
    qT = (q * scale).reshape(b, t, SB_HEADS, SB_HEAD_DIM).transpose(0, 2, 3, 1).astype(MXU_DTYPE)
    kh = k.reshape(b, t, SB_HEADS, SB_HEAD_DIM).transpose(0, 2, 1, 3).astype(MXU_DTYPE)
    vT = v.reshape(b, t, SB_HEADS, SB_HEAD_DIM).transpose(0, 2, 3, 1).astype(MXU_DTYPE)
    oT = pl.pallas_call(
        _sb_prompt_kernel,
        out_shape=jax.ShapeDtypeStruct((b, SB_HEADS, SB_HEAD_DIM, t), F32),
        grid=(b, t // ATT_TILE),
        in_specs=[
            pl.BlockSpec((1, SB_HEADS, SB_HEAD_DIM, ATT_TILE), lambda i, j: (i, 0, 0, j)),
            pl.BlockSpec((1, SB_HEADS, t, SB_HEAD_DIM), lambda i, j: (i, 0, 0, 0)),
            pl.BlockSpec((1, SB_HEADS, SB_HEAD_DIM, t), lambda i, j: (i, 0, 0, 0)),
            pl.BlockSpec((ATT_TILE, ATT_TILE), lambda i, j: (0, 0)),
        ],
        out_specs=pl.BlockSpec((1, SB_HEADS, SB_HEAD_DIM, ATT_TILE), lambda i, j: (i, 0, 0, j)),
        compiler_params=pltpu.CompilerParams(
            dimension_semantics=("parallel", "arbitrary"), vmem_limit_bytes=VMEM_LIMIT_BYTES),
        name="sb_prompt",
    )(qT, kh, vT, _strict_upper(ATT_TILE))
    return oT.transpose(0, 3, 1, 2).reshape(b, t, SB_WIDTH)


INT_MIN = -2 ** 31
COUNT_CHUNK = 4 * ATT_TILE
REL_NEAR_TILES = 2


def _bucket_of(dist):
    n = jnp.maximum(dist, 0)
    nf = jnp.maximum(n, 1).astype(F32)
    large = REL_MAX_EXACT + (jnp.log(nf / REL_MAX_EXACT) / math.log(REL_MAX_DIST / REL_MAX_EXACT)
                             * (REL_BUCKETS - REL_MAX_EXACT)).astype(jnp.int32)
    return jnp.where(n < REL_MAX_EXACT, n, jnp.minimum(large, REL_BUCKETS - 1))


def _bias_tile_kernel(dec_seq, relb_ref, o_ref, s_ref):
    t = ATT_TILE
    row = lax.broadcasted_iota(jnp.int32, (t, t), 0)
    lane = lax.broadcasted_iota(jnp.int32, (t, t), 1)
    for c in range(REL_NEAR_TILES):
        bucket = _bucket_of(c * t + lane - row)
        for h in range(C_HEADS):
            tile = jnp.zeros((t, t), F32)
            for b in range(REL_BUCKETS):
                tile = jnp.where(bucket == b, relb_ref[b, h], tile)
            o_ref[h, c] = tile
    row_head = row >> int(math.log2(dec_seq))
    row_t = row & (dec_seq - 1)
    buckets = (_bucket_of(t + row_t - lane), _bucket_of(row_t - lane),
               jnp.full((t, t), REL_BUCKETS - 1, jnp.int32))
    tiles = [jnp.zeros((t, t), F32) for _ in buckets]
    for b in range(REL_BUCKETS):
        by_head = jnp.zeros((t, t), F32)
        for h in range(C_HEADS):
            by_head = jnp.where(row_head == h, relb_ref[b, h], by_head)
        tiles = [jnp.where(bk == b, by_head, tl) for bk, tl in zip(buckets, tiles)]
    for c, tl in enumerate(tiles):
        s_ref[c] = tl


def _bias_tiles(rel_bias, dec_seq):
    assert C_HEADS * dec_seq == ATT_TILE
    return pl.pallas_call(
        functools.partial(_bias_tile_kernel, dec_seq),
        out_shape=(jax.ShapeDtypeStruct((C_HEADS, REL_NEAR_TILES, ATT_TILE, ATT_TILE), F32),
                   jax.ShapeDtypeStruct((3, ATT_TILE, ATT_TILE), F32)),
        in_specs=[pl.BlockSpec(memory_space=pltpu.SMEM)],
        name="rel_bias_tiles",
    )(rel_bias)


def _sortable(s):
    bits = lax.bitcast_convert_type(s, jnp.int32)
    return bits ^ ((bits >> 31) & 0x7FFFFFFF)


def _c_prompt_kernel(topk, relb_ref, qiT_ref, wiT_ref, ki_ref, qT_ref, k_ref, vT_ref, btile_ref,
                     lower_ref, o_ref, key_ref, m_ref, l_ref, acc_ref):
    tk = tq = ATT_TILE
    qb = pl.program_id(1)
    n_chunks = qb // (COUNT_CHUNK // tk) + 1
    row = lax.broadcasted_iota(jnp.int32, (tk, tq), 0)
    lane = lax.broadcasted_iota(jnp.int32, (tk, tq), 1)
    q_idx = qb * tq + lane
    w = wiT_ref[0] * (IDX_HEADS ** -0.5)

    def score_block(j, _):
        ks = pl.multiple_of(j * tk, tk)
        kib = ki_ref[0, pl.ds(ks, tk), :]
        s = jnp.zeros((tk, tq), F32)
        for hh in range(IDX_HEADS):
            d = jnp.dot(kib, qiT_ref[0, hh], preferred_element_type=F32)
            s = s + jnp.maximum(d, 0.0) * w[hh:hh + 1, :]
        s = jnp.where(ks + row <= q_idx, s, -jnp.inf)
        key_ref[pl.ds(ks, tk), :] = _sortable(s)
        return 0

    lax.fori_loop(0, n_chunks * (COUNT_CHUNK // tk), score_block, 0)

    def count(pred, thr):
        def chunk(c, cnt):
            base = pl.multiple_of(c * COUNT_CHUNK, COUNT_CHUNK)
            ind = jnp.where(pred(key_ref[pl.ds(base, COUNT_CHUNK), :], thr), 1, 0)
            return cnt + jnp.sum(ind.reshape(COUNT_CHUNK // 8, 8, tq), axis=0)
        cnt = lax.fori_loop(0, n_chunks, chunk, jnp.zeros((8, tq), jnp.int32))
        return jnp.sum(cnt, axis=0, keepdims=True)

    def bit_step(i, t_u):
        cand_u = t_u | lax.shift_left(jnp.int32(1), 31 - i)
        cnt = count(lambda x, thr: x >= thr, cand_u ^ INT_MIN)
        return jnp.where(cnt >= topk, cand_u, t_u)

    thr = lax.fori_loop(0, 32, bit_step, jnp.zeros((1, tq), jnp.int32)) ^ INT_MIN
    need = (topk - count(lambda x, t: x > t, thr)).astype(F32)

    def select_block(j, seen):
        ks = pl.multiple_of(j * tk, tk)
        blk = key_ref[pl.ds(ks, tk), :]
        eq = jnp.where(blk == thr, 1.0, 0.0)
        rank = jnp.dot(lower_ref[...], eq.astype(MXU_DTYPE), preferred_element_type=F32) + seen
        tie = jnp.where(rank < need, eq, 0.0)
        sel = jnp.where(blk > thr, 1.0, tie)
        sel = jnp.where(ks + row <= q_idx, sel, 0.0)
        key_ref[pl.ds(ks, tk), :] = sel.astype(jnp.int32)
        return seen + jnp.sum(eq, axis=0, keepdims=True)

    lax.fori_loop(0, qb + 1, select_block, jnp.zeros((1, tq), F32))

    m_ref[...] = jnp.full(m_ref.shape, NEG, F32)
    l_ref[...] = jnp.zeros(l_ref.shape, F32)
    acc_ref[...] = jnp.zeros(acc_ref.shape, F32)

    def attend(j, bias_of_head):
        ks = pl.multiple_of(j * tk, tk)
        sel = key_ref[pl.ds(ks, tk), :] != 0
        for n in range(C_KV_HEADS):
            kb = k_ref[0, n, pl.ds(ks, tk), :]
            vb = vT_ref[0, n, :, pl.ds(ks, tk)]
            for g in range(C_GROUP):
                h = n * C_GROUP + g
                lg = jnp.dot(kb, qT_ref[0, h], preferred_element_type=F32) + bias_of_head(h)
                lg = jnp.where(sel, lg, NEG)
                m_old = m_ref[h]
                m_new = jnp.maximum(m_old, jnp.max(lg, axis=0, keepdims=True))
                alpha = jnp.exp(m_old - m_new)
                p = jnp.exp(lg - m_new)
                l_ref[h] = alpha * l_ref[h] + jnp.sum(p, axis=0, keepdims=True)
                acc_ref[h] = alpha * acc_ref[h] + jnp.dot(vb, p.astype(MXU_DTYPE),
                                                          preferred_element_type=F32)
                m_ref[h] = m_new

    def far_block(j, _):
        attend(j, lambda h: relb_ref[REL_BUCKETS - 1, h])
        return 0

    lax.fori_loop(0, jnp.maximum(qb - 1, 0), far_block, 0)

    @pl.when(qb >= 1)
    def _():
        attend(qb - 1, lambda h: btile_ref[h, 1])

    attend(qb, lambda h: btile_ref[h, 0])
    for h in range(C_HEADS):
        o_ref[0, h] = acc_ref[h] / l_ref[h]


def _c_prompt(q, k, v, qi, ki, wi, rel_bias, btiles):
    b, t, _ = q.shape
    topk = min(TOPK_MAX, t // 4)
    assert t % COUNT_CHUNK == 0 and COUNT_CHUNK >= topk
    qT = (q * C_HEAD_DIM ** -0.5).reshape(b, t, C_HEADS, C_HEAD_DIM).transpose(0, 2, 3, 1).astype(MXU_DTYPE)
    kh = k.reshape(b, t, C_KV_HEADS, C_HEAD_DIM).transpose(0, 2, 1, 3).astype(MXU_DTYPE)
    vT = v.reshape(b, t, C_KV_HEADS, C_HEAD_DIM).transpose(0, 2, 3, 1).astype(MXU_DTYPE)
    qi_hi, qi_lo = _split_hi_lo_mxu(
        (qi * IDX_DIM ** -0.5).reshape(b, t, IDX_HEADS, IDX_DIM).transpose(0, 2, 3, 1))
    qiT = jnp.concatenate([qi_hi, qi_lo, qi_hi], axis=2)
    ki_hi, ki_lo = _split_hi_lo_mxu(ki)
    ki3 = jnp.concatenate([ki_hi, ki_hi, ki_lo], axis=2)
    wiT = wi.transpose(0, 2, 1)
    tile = ATT_TILE
    oT = pl.pallas_call(
        functools.partial(_c_prompt_kernel, topk),
        out_shape=jax.ShapeDtypeStruct((b, C_HEADS, C_HEAD_DIM, t), F32),
        grid=(b, t // tile),
        in_specs=[
            pl.BlockSpec(memory_space=pltpu.SMEM),
            pl.BlockSpec((1, IDX_HEADS, 3 * IDX_DIM, tile), lambda i, j: (i, 0, 0, j)),
            pl.BlockSpec((1, IDX_HEADS, tile), lambda i, j: (i, 0, j)),
            pl.BlockSpec((1, t, 3 * IDX_DIM), lambda i, j: (i, 0, 0)),
            pl.BlockSpec((1, C_HEADS, C_HEAD_DIM, tile), lambda i, j: (i, 0, 0, j)),
            pl.BlockSpec((1, C_KV_HEADS, t, C_HEAD_DIM), lambda i, j: (i, 0, 0, 0)),
            pl.BlockSpec((1, C_KV_HEADS, C_HEAD_DIM, t), lambda i, j: (i, 0, 0, 0)),
            pl.BlockSpec((C_HEADS, REL_NEAR_TILES, tile, tile), lambda i, j: (0, 0, 0, 0)),
            pl.BlockSpec((tile, tile), lambda i, j: (0, 0)),
        ],
        out_specs=pl.BlockSpec((1, C_HEADS, C_HEAD_DIM, tile), lambda i, j: (i, 0, 0, j)),
        scratch_shapes=[
            pltpu.VMEM((t, tile), jnp.int32),
            pltpu.VMEM((C_HEADS, 1, tile), F32),
            pltpu.VMEM((C_HEADS, 1, tile), F32),
            pltpu.VMEM((C_HEADS, C_HEAD_DIM, tile), F32),
        ],
        compiler_params=pltpu.CompilerParams(
            dimension_semantics=("parallel", "arbitrary"), vmem_limit_bytes=VMEM_LIMIT_BYTES),
        name="c_prompt",
    )(rel_bias, qiT, wiT, ki3, qT, kh, vT, btiles, _strict_lower(tile))
    return oT.transpose(0, 3, 1, 2).reshape(b, t, C_WIDTH)


_NT = (((1,), (1,)), ((), ()))


def _pad_rows(x, rows):
    return jnp.concatenate([x, jnp.zeros((rows - x.shape[0], x.shape[1]), x.dtype)], axis=0)


def _prefix_and_total(n, strict_before):
    i = np.arange(n)
    tri = (i[:, None] < i[None, :]) if strict_before else (i[:, None] > i[None, :])
    return jnp.asarray(np.concatenate([tri, np.ones((n, n), bool)], axis=1).astype(np.float32), MXU_DTYPE)


def _sb_sample_kernel(n_pages, pt_ref, q_ref, kn_ref, vn_ref, hmask_ref, sufx_ref, *rest):
    k_pages, v_pages = rest[:n_pages], rest[n_pages:2 * n_pages]
    o_ref, carry_ref, acc_ref = rest[2 * n_pages:]
    t = q_ref.shape[1]
    rows, tk = SB_HEADS * t, PAGE_SIZE
    hmask = hmask_ref[...]
    q = q_ref[0] * (SB_HEAD_DIM ** -0.5)
    qbd = (jnp.concatenate([q] * SB_HEADS, axis=0) * hmask).astype(MXU_DTYPE)
    carry_ref[...] = jnp.zeros(carry_ref.shape, F32)
    acc_ref[...] = jnp.zeros(acc_ref.shape, F32)

    def step(kb, vb, valid):
        z = lax.dot_general(qbd, kb, _NT, preferred_element_type=F32)
        sp = jnp.maximum(z, 0.0) + jnp.log(1.0 + jnp.exp(-jnp.abs(z)))
        lr = -sp if valid is None else jnp.where(valid, -sp, 0.0)
        hi, lo = _split_hi_lo_mxu(lr)
        both = (jnp.dot(hi, sufx_ref[...], preferred_element_type=F32)
                + jnp.dot(lo, sufx_ref[...], preferred_element_type=F32))
        a = jnp.exp(z - sp + both[:, :tk] + carry_ref[...])
        if valid is not None:
            a = jnp.where(valid, a, 0.0)
        acc_ref[...] += jnp.dot(a.astype(MXU_DTYPE), vb, preferred_element_type=F32)
        carry_ref[...] += both[:, tk:]

    row_t = lax.broadcasted_iota(jnp.int32, (rows, tk), 0) & (t - 1)
    lane = lax.broadcasted_iota(jnp.int32, (rows, tk), 1)
    step(_pad_rows(kn_ref[0], tk).astype(MXU_DTYPE), _pad_rows(vn_ref[0], tk).astype(MXU_DTYPE),
         lane < row_t)
    for p in reversed(range(n_pages)):
        @pl.when(jnp.max(carry_ref[...]) > SB_EXIT)
        def _(p=p):
            step(k_pages[p][...].astype(MXU_DTYPE), v_pages[p][...].astype(MXU_DTYPE), None)

    acc = acc_ref[...] * hmask
    y = acc[0:t]
    for h in range(1, SB_HEADS):
        y = y + acc[h * t:(h + 1) * t]
    o_ref[0] = y


def _page_specs(n_pages, layer, width):
    return [pl.BlockSpec((None, None, PAGE_SIZE, width), lambda i, pt, p=p: (layer, pt[i, p], 0, 0))
            for p in range(n_pages)]


def _sb_sample(q, k, v, cache_k, cache_v, layer, page_table):
    b, t, _ = q.shape
    assert t & (t - 1) == 0 and t <= PAGE_SIZE
    n_pages = page_table.shape[1]
    rows = SB_HEADS * t
    ck = cache_k.reshape(cache_k.shape[:2] + (PAGE_SIZE, SB_WIDTH))
    cv = cache_v.reshape(cache_v.shape[:2] + (PAGE_SIZE, SB_WIDTH))
    hmask = jnp.asarray((np.arange(rows)[:, None] // t == np.arange(SB_WIDTH)[None, :] // SB_HEAD_DIM)
                        .astype(np.float32))
    tok = pl.BlockSpec((1, t, SB_WIDTH), lambda i, pt: (i, 0, 0))
    const = lambda i, pt: (0, 0)
    return pl.pallas_call(
        functools.partial(_sb_sample_kernel, n_pages),
        out_shape=jax.ShapeDtypeStruct((b, t, SB_WIDTH), F32),
        grid_spec=pltpu.PrefetchScalarGridSpec(
            num_scalar_prefetch=1,
            grid=(b,),
            in_specs=[tok, tok, tok,
                      pl.BlockSpec((rows, SB_WIDTH), const),
                      pl.BlockSpec((PAGE_SIZE, 2 * PAGE_SIZE), const)]
            + _page_specs(n_pages, layer, SB_WIDTH) + _page_specs(n_pages, layer, SB_WIDTH),
            out_specs=tok,
            scratch_shapes=[pltpu.VMEM((rows, PAGE_SIZE), F32), pltpu.VMEM((rows, SB_WIDTH), F32)],
        ),
        compiler_params=pltpu.CompilerParams(
            dimension_semantics=("arbitrary",), vmem_limit_bytes=VMEM_LIMIT_BYTES),
        name="sb_sample",
    )(page_table, q, k, v, hmask, _prefix_and_total(PAGE_SIZE, False),
      *([ck] * n_pages), *([cv] * n_pages))


def _c_sample_kernel(n_pages, topk, pt_ref, qi_ref, w_ref, q_ref, kin_ref, kn_ref, vn_ref,
                     expand_ref, nmask_ref, sbias_ref, prex_ref, *rest):
    idx_pages, k_pages, v_pages = rest[:n_pages], rest[n_pages:2 * n_pages], rest[2 * n_pages:3 * n_pages]
    o_ref, sc_ref, lg_ref = rest[3 * n_pages:]
    t = kin_ref.shape[1]
    tk = PAGE_SIZE
    nb = n_pages + 1
    ih = IDX_HEADS * t
    rows = C_HEADS * t
    row8 = lax.broadcasted_iota(jnp.int32, (t, tk), 0)
    lane8 = lax.broadcasted_iota(jnp.int32, (t, tk), 1)
    new_valid = lane8 <= row8
    qi_cat = qi_ref[0]
    w = w_ref[0] * (IDX_HEADS ** -0.5)

    def scores(idx_blk, valid):
        hi, lo = _split_hi_lo_mxu(idx_blk)
        s = lax.dot_general(qi_cat, hi, _NT, preferred_element_type=F32)
        s = s[:ih] + s[ih:] + lax.dot_general(qi_cat[:ih], lo, _NT, preferred_element_type=F32)
        r = jnp.maximum(s, 0.0) * w
        sc = r[0:t]
        for hh in range(1, IDX_HEADS):
            sc = sc + r[hh * t:(hh + 1) * t]
        if valid is not None:
            sc = jnp.where(valid, sc, -jnp.inf)
        return _sortable(sc)

    for p in range(n_pages):
        sc_ref[:, p * tk:(p + 1) * tk] = scores(idx_pages[p][...], None)
    sc_ref[:, n_pages * tk:] = scores(_pad_rows(kin_ref[0], tk), new_valid)

    def count(pred, thr):
        return jnp.sum(jnp.where(pred(sc_ref[...], thr), 1.0, 0.0), axis=1, keepdims=True)

    def bit_step(i, t_u):
        cand_u = t_u | lax.shift_left(jnp.int32(1), 31 - i)
        cnt = count(lambda x, thr: x >= thr, cand_u ^ INT_MIN)
        return jnp.where(cnt >= topk, cand_u, t_u)

    thr = lax.fori_loop(0, 32, bit_step, jnp.zeros((t, 1), jnp.int32)) ^ INT_MIN
    need = topk - count(lambda x, th: x > th, thr)

    seen = jnp.zeros((t, tk), F32)
    for blk in range(nb):
        x = sc_ref[:, blk * tk:(blk + 1) * tk]
        eq = jnp.where(x == thr, 1.0, 0.0)
        both = jnp.dot(eq.astype(MXU_DTYPE), prex_ref[...], preferred_element_type=F32)
        tie = jnp.where(both[:, :tk] + seen < need, eq, 0.0)
        sel = jnp.where(x > thr, 1.0, tie)
        if blk == n_pages:
            sel = jnp.where(new_valid, sel, 0.0)
        sc_ref[:, blk * tk:(blk + 1) * tk] = sel.astype(jnp.int32)
        seen = seen + both[:, tk:]

    qbd = (jnp.dot(q_ref[0], expand_ref[...], preferred_element_type=F32) * nmask_ref[...]).astype(MXU_DTYPE)

    def kv_block(pages, new_ref, blk):
        x = pages[blk][...] if blk < n_pages else _pad_rows(new_ref[0], tk)
        return x.astype(MXU_DTYPE)

    m = jnp.full((rows, tk), NEG, F32)
    for blk in range(nb):
        bias = sbias_ref[2] if blk < n_pages - 1 else sbias_ref[blk - (n_pages - 1)]
        lg = lax.dot_general(qbd, kv_block(k_pages, kn_ref, blk), _NT, preferred_element_type=F32) + bias
        sel = jnp.concatenate([sc_ref[:, blk * tk:(blk + 1) * tk]] * C_HEADS, axis=0) != 0
        lg = jnp.where(sel, lg, NEG)
        lg_ref[:, blk * tk:(blk + 1) * tk] = lg
        m = jnp.maximum(m, lg)
    m_row = jnp.max(m, axis=1, keepdims=True)
    lsum = jnp.zeros((rows, tk), F32)
    acc = jnp.zeros((rows, C_KV_WIDTH), F32)
    for blk in range(nb):
        p = jnp.exp(lg_ref[:, blk * tk:(blk + 1) * tk] - m_row)
        lsum = lsum + p
        acc = acc + jnp.dot(p.astype(MXU_DTYPE), kv_block(v_pages, vn_ref, blk),
                            preferred_element_type=F32)
    o = acc / jnp.sum(lsum, axis=1, keepdims=True)
    for h in range(C_HEADS):
        n = h // C_GROUP
        o_ref[0, h] = o[h * t:(h + 1) * t, n * C_HEAD_DIM:(n + 1) * C_HEAD_DIM]


def _c_sample(q, k, v, qi, ki, wi, cache_k, cache_v, cache_idx, layer, page_table, sbias):
    b, t, _ = q.shape
    n_pages = page_table.shape[1]
    topk = min(TOPK_MAX, (n_pages * PAGE_SIZE + t) // 4)
    rows = C_HEADS * t
    ih = IDX_HEADS * t
    assert rows == ATT_TILE and t <= PAGE_SIZE
    q_rows = ((q * C_HEAD_DIM ** -0.5).reshape(b, t, C_HEADS, C_HEAD_DIM).transpose(0, 2, 1, 3)
              .reshape(b, rows, C_HEAD_DIM).astype(MXU_DTYPE))
    qi_rows = ((qi * IDX_DIM ** -0.5).reshape(b, t, IDX_HEADS, IDX_DIM).transpose(0, 2, 1, 3)
               .reshape(b, ih, IDX_DIM))
    qi_cat = jnp.concatenate(_split_hi_lo_mxu(qi_rows), axis=1)
    w_col = wi.transpose(0, 2, 1).reshape(b, ih, 1)
    ck = cache_k.reshape(cache_k.shape[:2] + (PAGE_SIZE, C_KV_WIDTH))
    cv = cache_v.reshape(cache_v.shape[:2] + (PAGE_SIZE, C_KV_WIDTH))
    expand = jnp.asarray(np.tile(np.eye(C_HEAD_DIM, dtype=np.float32), (1, C_KV_HEADS)), MXU_DTYPE)
    nmask = jnp.asarray((np.arange(rows)[:, None] // (t * C_GROUP)
                         == np.arange(C_KV_WIDTH)[None, :] // C_HEAD_DIM).astype(np.float32))
    per_seq = lambda r, c: pl.BlockSpec((1, r, c), lambda i, pt: (i, 0, 0))
    const2 = lambda i, pt: (0, 0)
    o = pl.pallas_call(
        functools.partial(_c_sample_kernel, n_pages, topk),
        out_shape=jax.ShapeDtypeStruct((b, C_HEADS, t, C_HEAD_DIM), F32),
        grid_spec=pltpu.PrefetchScalarGridSpec(
            num_scalar_prefetch=1,
            grid=(b,),
            in_specs=[per_seq(2 * ih, IDX_DIM), per_seq(ih, 1), per_seq(rows, C_HEAD_DIM),
                      per_seq(t, IDX_DIM), per_seq(t, C_KV_WIDTH), per_seq(t, C_KV_WIDTH),
                      pl.BlockSpec((C_HEAD_DIM, C_KV_WIDTH), const2),
                      pl.BlockSpec((rows, C_KV_WIDTH), const2),
                      pl.BlockSpec((3, ATT_TILE, ATT_TILE), lambda i, pt: (0, 0, 0)),
                      pl.BlockSpec((PAGE_SIZE, 2 * PAGE_SIZE), const2)]
            + _page_specs(n_pages, layer, IDX_DIM) + _page_specs(n_pages, layer, C_KV_WIDTH)
            + _page_specs(n_pages, layer, C_KV_WIDTH),
            out_specs=pl.BlockSpec((1, C_HEADS, t, C_HEAD_DIM), lambda i, pt: (i, 0, 0, 0)),
            scratch_shapes=[pltpu.VMEM((t, (n_pages + 1) * PAGE_SIZE), jnp.int32),
                            pltpu.VMEM((rows, (n_pages + 1) * PAGE_SIZE), F32)],
        ),
        compiler_params=pltpu.CompilerParams(
            dimension_semantics=("arbitrary",), vmem_limit_bytes=VMEM_LIMIT_BYTES),
        name="c_sample",
    )(page_table, qi_cat, w_col, q_rows, ki, k, v, expand, nmask, sbias,
      _prefix_and_total(PAGE_SIZE, True),
      *([cache_idx] * n_pages), *([ck] * n_pages), *([cv] * n_pages))
    return o.transpose(0, 2, 1, 3).reshape(b, t, C_WIDTH)


def _pool_mix(u_ext, pos, w_pool, pool_scale):
    b, l, c = u_ext.shape
    t = l - POOL_BUF
    cs = jnp.concatenate([jnp.zeros((b, 1, c), F32), jnp.cumsum(u_ext, axis=1)], axis=1)
    u_new = u_ext[:, POOL_BUF:]
    hi = cs[:, POOL_BUF + 1:]
    outs = []
    for g, w in enumerate(POOL_WINDOWS):
        sl = slice(g * POOL_GROUP, (g + 1) * POOL_GROUP)
        lo = cs[:, POOL_BUF + 1 - w:POOL_BUF + 1 - w + t, sl]
        cnt = jnp.minimum(pos + 1, w).astype(F32)[None, :, None]
        outs.append((hi[..., sl] - lo) / cnt - u_new[..., sl])
    d = jnp.stack(outs, axis=2)
    y = jnp.einsum('btgc,gcd->btgd', d, w_pool).reshape(b, t, POOL_WIDTH)
    return y * pool_scale


def _ab_prompt_mix(u, q, k, v, w_pool, pool_scale):
    b, t, _ = u.shape
    pos = jnp.arange(t, dtype=jnp.int32)
    u_ext = jnp.concatenate([jnp.zeros((b, POOL_BUF, POOL_WIDTH), F32), u], axis=1)
    y_pool = _pool_mix(u_ext, pos, w_pool, pool_scale)
    y_sb = _sb_prompt(q, k, v)
    return jnp.concatenate([y_pool, y_sb], axis=-1), u_ext[:, -POOL_BUF:]


def _ab_sample_mix(u, q, k, v, pool_buf, cache_k, cache_v, layer, page_table, w_pool, pool_scale):
    t = u.shape[1]
    q_pos = page_table.shape[1] * PAGE_SIZE + jnp.arange(t, dtype=jnp.int32)
    u_ext = jnp.concatenate([pool_buf, u], axis=1)
    y_pool = _pool_mix(u_ext, q_pos, w_pool, pool_scale)
    y_sb = _sb_sample(q, k, v, cache_k, cache_v, layer, page_table)
    return jnp.concatenate([y_pool, y_sb], axis=-1), u_ext[:, -POOL_BUF:]


def _segment_mean_matrix(width):
    seg = np.arange(width) // C_HEAD_DIM
    return jnp.asarray((seg[:, None] == seg[None, :]).astype(np.float32) / C_HEAD_DIM, BF16)


def kernel(x_prompt, x_sample, state_pool, cache_b_k, cache_b_v, cache_c_k, cache_c_v, cache_c_idx, page_table,
           g_ffn, w_ffn_gate, w_ffn_up, w_ffn_down, g_mix, w_in_ab, w_pool, pool_scale, w_out_ab,
           w_in_c, g_q, g_k, w_out_c, rel_bias):
    bp, tp, _ = x_prompt.shape
    bs, ts, _ = x_sample.shape
    mp = bp * tp
    x = jnp.concatenate([x_prompt.reshape(mp, D_MODEL), x_sample.reshape(bs * ts, D_MODEL)], axis=0)

    wg = w_ffn_gate.astype(BF16)
    wu = w_ffn_up.astype(BF16)
    wd = w_ffn_down.astype(BF16)
    w_ab = w_in_ab.astype(BF16)
    w_oab = w_out_ab.astype(BF16)
    w_c = jnp.pad(w_in_c, ((0, 0), (0, 0), (0, C_IN_PAD - C_IN))).astype(BF16)
    w_oc = w_out_c.astype(BF16)
    segq = _segment_mean_matrix(C_WIDTH)
    segk = _segment_mean_matrix(C_KV_WIDTH)
    btiles, sbias = _bias_tiles(rel_bias, ts)

    def split(a):
        return a[:mp].reshape(bp, tp, -1), a[mp:].reshape(bs, ts, -1)

    pool_p, pool_s, kbp, vbp, kbs, vbs = [], [], [], [], [], []
    kcp, vcp, icp, kcs, vcs, ics = [], [], [], [], [], []
    for layer in range(DEPTH):
        j = layer // 2
        x = _ffn(x, g_ffn[layer, 0][None], wg[layer, 0], wu[layer, 0], wd[layer, 0])
        g = g_mix[layer][None]
        if layer % 2 == 0:
            u, q, k, v = _proj_ab(x, g, w_ab[j])
            (u_p, u_s), (q_p, q_s), (k_p, k_s), (v_p, v_s) = split(u), split(q), split(k), split(v)
            y_p, sp = _ab_prompt_mix(u_p, q_p, k_p, v_p, w_pool[j], pool_scale[j])
            y_s, ss = _ab_sample_mix(u_s, q_s, k_s, v_s, state_pool[j], cache_b_k, cache_b_v, j,
                                     page_table, w_pool[j], pool_scale[j])
            pool_p.append(sp); pool_s.append(ss)
            kbp.append(k_p.reshape(bp, tp, SB_HEADS, SB_HEAD_DIM))
            vbp.append(v_p.reshape(bp, tp, SB_HEADS, SB_HEAD_DIM))
            kbs.append(k_s.reshape(bs, ts, SB_HEADS, SB_HEAD_DIM))
            vbs.append(v_s.reshape(bs, ts, SB_HEADS, SB_HEAD_DIM))
            w_out = w_oab[j]
        else:
            gq = jnp.tile(g_q[j], C_HEADS)[None]
            gk = jnp.tile(g_k[j], C_KV_HEADS)[None]
            q, k, v, qi, tail = _proj_c(x, g, w_c[j], gq, gk, segq, segk)
            ki = tail[:, :IDX_DIM]
            wi = tail[:, IDX_DIM:IDX_DIM + IDX_HEADS]
            (q_p, q_s), (k_p, k_s), (v_p, v_s) = split(q), split(k), split(v)
            (qi_p, qi_s), (ki_p, ki_s), (wi_p, wi_s) = split(qi), split(ki), split(wi)
            y_p = _c_prompt(q_p, k_p, v_p, qi_p, ki_p, wi_p, rel_bias, btiles)
            y_s = _c_sample(q_s, k_s, v_s, qi_s, ki_s, wi_s, cache_c_k, cache_c_v, cache_c_idx,
                            j, page_table, sbias)
            kcp.append(k_p.reshape(bp, tp, C_KV_HEADS, C_HEAD_DIM))
            vcp.append(v_p.reshape(bp, tp, C_KV_HEADS, C_HEAD_DIM))
            icp.append(ki_p)
            kcs.append(k_s.reshape(bs, ts, C_KV_HEADS, C_HEAD_DIM))
            vcs.append(v_s.reshape(bs, ts, C_KV_HEADS, C_HEAD_DIM))
            ics.append(ki_s)
            w_out = w_oc[j]
        y = jnp.concatenate([y_p.reshape(mp, D_MODEL), y_s.reshape(bs * ts, D_MODEL)], axis=0)
        x = _out_proj(x, y, w_out)
        x = _ffn(x, g_ffn[layer, 1][None], wg[layer, 1], wu[layer, 1], wd[layer, 1])

    return (x[:mp].reshape(bp, tp, D_MODEL), x[mp:].reshape(bs, ts, D_MODEL),
            jnp.stack(pool_p), jnp.stack(pool_s),
            jnp.stack(kbp), jnp.stack(vbp), jnp.stack(kbs), jnp.stack(vbs),
            jnp.stack(kcp), jnp.stack(vcp), jnp.stack(icp),
            jnp.stack(kcs), jnp.stack(vcs), jnp.stack(ics))
```

```python
import functools
import math

import jax
import jax.numpy as jnp
import numpy as np
from jax import lax
from jax.experimental import pallas as pl
from jax.experimental.pallas import tpu as pltpu

F32 = jnp.float32
BF16 = jnp.bfloat16

D_MODEL = 1024
DEPTH = 4
D_FF = 2816
POOL_WINDOWS = (2, 4, 8, 16)
POOL_WIDTH = 512
POOL_GROUP = 128
POOL_BUF = 15
SB_HEADS = 8
SB_HEAD_DIM = 64
SB_WIDTH = 512
AB_IN = POOL_WIDTH + 3 * SB_WIDTH
C_HEAD_DIM = 64
C_HEADS = 16
C_KV_HEADS = 4
C_GROUP = 4
C_WIDTH = 1024
C_KV_WIDTH = 256
IDX_HEADS = 8
IDX_DIM = 64
TOPK_MAX = 256
C_IN = 2120
C_IN_PAD = 2176
REL_BUCKETS = 32
REL_MAX_EXACT = 16
REL_MAX_DIST = 128
Q_BLOCK = 128
PAGE_SIZE = 128
EPS = 1e-6
NEG = -1e30

VMEM_LIMIT_BYTES = 56 * 1024 * 1024
FF_CHUNK = 256
TOKEN_TILE = 512


def _rms(x, g):
    ms = jnp.mean(x * x, axis=-1, keepdims=True)
    return x * lax.rsqrt(ms + EPS) * g


def _ffn_kernel(x_ref, g_ref, wg_ref, wu_ref, wd_ref, o_ref):
    x = x_ref[...]
    h = _rms(x, g_ref[...]).astype(BF16)
    acc = x
    for c in range(D_FF // FF_CHUNK):
        sl = slice(c * FF_CHUNK, (c + 1) * FF_CHUNK)
        gate = jnp.dot(h, wg_ref[:, sl], preferred_element_type=F32)
        up = jnp.dot(h, wu_ref[:, sl], preferred_element_type=F32)
        act = (0.5 * gate * jax.nn.sigmoid(gate) * up).astype(BF16)
        acc = acc + jnp.dot(act, wd_ref[sl, :], preferred_element_type=F32)
    o_ref[...] = acc


def _ffn(x, g, wg, wu, wd):
    m = x.shape[0]
    resident = dict(pipeline_mode=pl.Buffered(1))
    return pl.pallas_call(
        _ffn_kernel,
        out_shape=jax.ShapeDtypeStruct((m, D_MODEL), F32),
        grid=(m // TOKEN_TILE,),
        in_specs=[
            pl.BlockSpec((TOKEN_TILE, D_MODEL), lambda i: (i, 0)),
            pl.BlockSpec((1, D_MODEL), lambda i: (0, 0)),
            pl.BlockSpec((D_MODEL, D_FF), lambda i: (0, 0), **resident),
            pl.BlockSpec((D_MODEL, D_FF), lambda i: (0, 0), **resident),
            pl.BlockSpec((D_FF, D_MODEL), lambda i: (0, 0), **resident),
        ],
        out_specs=pl.BlockSpec((TOKEN_TILE, D_MODEL), lambda i: (i, 0)),
        compiler_params=pltpu.CompilerParams(
            dimension_semantics=("parallel",), vmem_limit_bytes=VMEM_LIMIT_BYTES),
        name="ffn_half",
    )(x, g, wg, wu, wd)


def _proj_ab_kernel(x_ref, g_ref, w_ref, u_ref, q_ref, k_ref, v_ref):
    h = _rms(x_ref[...], g_ref[...]).astype(BF16)
    p = jnp.dot(h, w_ref[...], preferred_element_type=F32)
    u_ref[...] = p[:, :POOL_WIDTH]
    q_ref[...] = p[:, POOL_WIDTH:POOL_WIDTH + SB_WIDTH]
    k_ref[...] = p[:, POOL_WIDTH + SB_WIDTH:POOL_WIDTH + 2 * SB_WIDTH]
    v_ref[...] = p[:, POOL_WIDTH + 2 * SB_WIDTH:]


def _proj_ab(x, g, w):
    m = x.shape[0]
    out = jax.ShapeDtypeStruct((m, SB_WIDTH), F32)
    spec = pl.BlockSpec((TOKEN_TILE, SB_WIDTH), lambda i: (i, 0))
    return pl.pallas_call(
        _proj_ab_kernel,
        out_shape=(out, out, out, out),
        grid=(m // TOKEN_TILE,),
        in_specs=[
            pl.BlockSpec((TOKEN_TILE, D_MODEL), lambda i: (i, 0)),
            pl.BlockSpec((1, D_MODEL), lambda i: (0, 0)),
            pl.BlockSpec((D_MODEL, AB_IN), lambda i: (0, 0), pipeline_mode=pl.Buffered(1)),
        ],
        out_specs=(spec, spec, spec, spec),
        compiler_params=pltpu.CompilerParams(
            dimension_semantics=("parallel",), vmem_limit_bytes=VMEM_LIMIT_BYTES),
        name="proj_ab",
    )(x, g, w)


def _split_hi_lo(x):
    hi = x.astype(BF16)
    lo = (x - hi.astype(F32)).astype(BF16)
    return hi, lo


def _head_rms(x, seg_mean_ref, gain):
    hi, lo = _split_hi_lo(x * x)
    ms = (jnp.dot(hi, seg_mean_ref[...], preferred_element_type=F32)
          + jnp.dot(lo, seg_mean_ref[...], preferred_element_type=F32))
    return x * lax.rsqrt(ms + EPS) * gain


def _proj_c_kernel(x_ref, g_ref, w_ref, gq_ref, gk_ref, segq_ref, segk_ref,
                   q_ref, k_ref, v_ref, qi_ref, tail_ref):
    h = _rms(x_ref[...], g_ref[...]).astype(BF16)
    p = jnp.dot(h, w_ref[...], preferred_element_type=F32)
    o_k = C_WIDTH
    o_v = o_k + C_KV_WIDTH
    o_qi = o_v + C_KV_WIDTH
    o_tail = o_qi + IDX_HEADS * IDX_DIM
    q_ref[...] = _head_rms(p[:, :o_k], segq_ref, gq_ref[...])
    k_ref[...] = _head_rms(p[:, o_k:o_v], segk_ref, gk_ref[...])
    v_ref[...] = p[:, o_v:o_qi]
    qi_ref[...] = p[:, o_qi:o_tail]
    tail_ref[...] = p[:, o_tail:]


def _proj_c(x, g, w, gq, gk, segq, segk):
    m = x.shape[0]
    tail = C_IN_PAD - (C_WIDTH + 2 * C_KV_WIDTH + IDX_HEADS * IDX_DIM)
    widths = (C_WIDTH, C_KV_WIDTH, C_KV_WIDTH, IDX_HEADS * IDX_DIM, tail)
    const = lambda i: (0, 0)
    return pl.pallas_call(
        _proj_c_kernel,
        out_shape=tuple(jax.ShapeDtypeStruct((m, n), F32) for n in widths),
        grid=(m // TOKEN_TILE,),
        in_specs=[
            pl.BlockSpec((TOKEN_TILE, D_MODEL), lambda i: (i, 0)),
            pl.BlockSpec((1, D_MODEL), const),
            pl.BlockSpec((D_MODEL, C_IN_PAD), const, pipeline_mode=pl.Buffered(1)),
            pl.BlockSpec((1, C_WIDTH), const),
            pl.BlockSpec((1, C_KV_WIDTH), const),
            pl.BlockSpec((C_WIDTH, C_WIDTH), const, pipeline_mode=pl.Buffered(1)),
            pl.BlockSpec((C_KV_WIDTH, C_KV_WIDTH), const, pipeline_mode=pl.Buffered(1)),
        ],
        out_specs=tuple(pl.BlockSpec((TOKEN_TILE, n), lambda i: (i, 0)) for n in widths),
        compiler_params=pltpu.CompilerParams(
            dimension_semantics=("parallel",), vmem_limit_bytes=VMEM_LIMIT_BYTES),
        name="proj_c",
    )(x, g, w, gq, gk, segq, segk)


def _out_proj_kernel(x_ref, y_ref, w_ref, o_ref):
    o_ref[...] = x_ref[...] + jnp.dot(y_ref[...].astype(BF16), w_ref[...],
                                      preferred_element_type=F32)


def _out_proj(x, y, w):
    m = x.shape[0]
    tile = pl.BlockSpec((TOKEN_TILE, D_MODEL), lambda i: (i, 0))
    return pl.pallas_call(
        _out_proj_kernel,
        out_shape=jax.ShapeDtypeStruct((m, D_MODEL), F32),
        grid=(m // TOKEN_TILE,),
        in_specs=[tile, tile,
                  pl.BlockSpec((D_MODEL, D_MODEL), lambda i: (0, 0), pipeline_mode=pl.Buffered(1))],
        out_specs=tile,
        compiler_params=pltpu.CompilerParams(
            dimension_semantics=("parallel",), vmem_limit_bytes=VMEM_LIMIT_BYTES),
        name="out_proj",
    )(x, y, w)


ATT_TILE = 128
ATT_TK = 128
ATT_TQ = 256
SB_LOCKSTEP = 2
SB_EXIT = -104.0
MXU_DTYPE = BF16


def _split_hi_lo_mxu(x):
    hi = x.astype(MXU_DTYPE)
    lo = (x - hi.astype(F32)).astype(MXU_DTYPE)
    return hi, lo


def _sb_prompt_kernel(qT_ref, k_ref, vT_ref, upper_ref, o_ref):
    tk, tq = ATT_TK, ATT_TQ
    qb = pl.program_id(1)
    row = lax.broadcasted_iota(jnp.int32, (tk, tq), 0)
    lane = lax.broadcasted_iota(jnp.int32, (tk, tq), 1)
    q_idx = qb * tq + lane
    upper = upper_ref[...]
    last_tile = qb * (tq // tk) + (tq // tk - 1)

    def head_step(h, ks, valid, carry, acc):
        kb = k_ref[0, h, pl.ds(ks, tk), :]
        z = jnp.dot(kb, qT_ref[0, h], preferred_element_type=F32)
        sp = jnp.maximum(z, 0.0) + jnp.log(1.0 + jnp.exp(-jnp.abs(z)))
        lr = jnp.where(valid, -sp, 0.0)
        hi, lo = _split_hi_lo_mxu(lr)
        between = (jnp.dot(upper, hi, preferred_element_type=F32)
                   + jnp.dot(upper, lo, preferred_element_type=F32) + carry)
        a = jnp.where(valid, jnp.exp(z - sp + between), 0.0)
        vb = vT_ref[0, h, :, pl.ds(ks, tk)]
        acc = acc + jnp.dot(vb, a.astype(MXU_DTYPE), preferred_element_type=F32)
        return carry + jnp.sum(lr, axis=0, keepdims=True), acc

    for h0 in range(0, SB_HEADS, SB_LOCKSTEP):
        heads = range(h0, h0 + SB_LOCKSTEP)

        def body(state, heads=heads):
            j, _, carries, accs = state
            ks = pl.multiple_of(j * tk, tk)
            valid = (ks + row) < q_idx
            out = [head_step(h, ks, valid, c, a) for h, c, a in zip(heads, carries, accs)]
            carries = tuple(o[0] for o in out)
            top = functools.reduce(jnp.maximum, [jnp.max(c) for c in carries])
            return j - 1, top, carries, tuple(o[1] for o in out)

        def cond(state):
            j, top, _, _ = state
            return jnp.logical_and(j >= 0, top > SB_EXIT)

        init = (last_tile, jnp.float32(0.0),
                tuple(jnp.zeros((1, tq), F32) for _ in heads),
                tuple(jnp.zeros((SB_HEAD_DIM, tq), F32) for _ in heads))
        _, _, _, accs = lax.while_loop(cond, body, init)
        for h, acc in zip(heads, accs):
            o_ref[0, h] = acc


def _strict_upper(n):
    i = np.arange(n)
    return jnp.asarray((i[None, :] > i[:, None]).astype(np.float32), MXU_DTYPE)


def _strict_lower(n):
    i = np.arange(n)
    return jnp.asarray((i[None, :] < i[:, None]).astype(np.float32), MXU_DTYPE)


def _sb_prompt(q, k, v):
    b, t, _ = q.shape
    scale = SB_HEAD_DIM ** -0.5
    qT = (q * scale).reshape(b, t, SB_HEADS, SB_HEAD_DIM).transpose(0, 2, 3, 1).astype(MXU_DTYPE)
    kh = k.reshape(b, t, SB_HEADS, SB_HEAD_DIM).transpose(0, 2, 1, 3).astype(MXU_DTYPE)
    vT = v.reshape(b, t, SB_HEADS, SB_HEAD_DIM).transpose(0, 2, 3, 1).astype(MXU_DTYPE)
    oT = pl.pallas_call(
        _sb_prompt_kernel,
        out_shape=jax.ShapeDtypeStruct((b, SB_HEADS, SB_HEAD_DIM, t), F32),
        grid=(b, t // ATT_TQ),
        in_specs=[
            pl.BlockSpec((1, SB_HEADS, SB_HEAD_DIM, ATT_TQ), lambda i, j: (i, 0, 0, j)),
            pl.BlockSpec((1, SB_HEADS, t, SB_HEAD_DIM), lambda i, j: (i, 0, 0, 0)),
            pl.BlockSpec((1, SB_HEADS, SB_HEAD_DIM, t), lambda i, j: (i, 0, 0, 0)),
            pl.BlockSpec((ATT_TK, ATT_TK), lambda i, j: (0, 0)),
        ],
        out_specs=pl.BlockSpec((1, SB_HEADS, SB_HEAD_DIM, ATT_TQ), lambda i, j: (i, 0, 0, j)),
        compiler_params=pltpu.CompilerParams(
            dimension_semantics=("parallel", "arbitrary"), vmem_limit_bytes=VMEM_LIMIT_BYTES),
        name="sb_prompt",
    )(qT, kh, vT, _strict_upper(ATT_TK))
    return oT.transpose(0, 3, 1, 2).reshape(b, t, SB_WIDTH)


INT_MIN = -2 ** 31
COUNT_CHUNK = 4 * ATT_TK
LOG2E = math.log2(math.e)
REL_NEAR_TILES = ATT_TQ // ATT_TK + 1
REL_LAST_BUCKET_FROM = math.ceil(REL_MAX_EXACT * (REL_MAX_DIST / REL_MAX_EXACT) ** (
    (REL_BUCKETS - 1 - REL_MAX_EXACT) / (REL_BUCKETS - REL_MAX_EXACT)))
assert ATT_TK + 1 >= REL_LAST_BUCKET_FROM
V_ROWS = C_HEAD_DIM + 16


def _bucket_of(dist):
    n = jnp.maximum(dist, 0)
    nf = jnp.maximum(n, 1).astype(F32)
    large = REL_MAX_EXACT + (jnp.log(nf / REL_MAX_EXACT) / math.log(REL_MAX_DIST / REL_MAX_EXACT)
                             * (REL_BUCKETS - REL_MAX_EXACT)).astype(jnp.int32)
    return jnp.where(n < REL_MAX_EXACT, n, jnp.minimum(large, REL_BUCKETS - 1))


def _bias_tile_kernel(dec_seq, relb_ref, o_ref, s_ref):
    krow = lax.broadcasted_iota(jnp.int32, (ATT_TK, ATT_TQ), 0)
    qlane = lax.broadcasted_iota(jnp.int32, (ATT_TK, ATT_TQ), 1)
    for c in range(REL_NEAR_TILES):
        bucket = _bucket_of((1 - c) * ATT_TK + qlane - krow)
        for h in range(C_HEADS):
            tile = jnp.zeros((ATT_TK, ATT_TQ), F32)
            for b in range(REL_BUCKETS):
                tile = jnp.where(bucket == b, relb_ref[b, h], tile)
            o_ref[h, c] = (tile - relb_ref[REL_BUCKETS - 1, h]) * LOG2E
    t = ATT_TILE
    row = lax.broadcasted_iota(jnp.int32, (t, t), 0)
    lane = lax.broadcasted_iota(jnp.int32, (t, t), 1)
    row_head = row >> int(math.log2(dec_seq))
    row_t = row & (dec_seq - 1)
    buckets = (_bucket_of(t + row_t - lane), _bucket_of(row_t - lane),
               jnp.full((t, t), REL_BUCKETS - 1, jnp.int32))
    tiles = [jnp.zeros((t, t), F32) for _ in buckets]
    for b in range(REL_BUCKETS):
        by_head = jnp.zeros((t, t), F32)
        for h in range(C_HEADS):
            by_head = jnp.where(row_head == h, relb_ref[b, h], by_head)
        tiles = [jnp.where(bk == b, by_head, tl) for bk, tl in zip(buckets, tiles)]
    for c, tl in enumerate(tiles):
        s_ref[c] = tl


def _bias_tiles(rel_bias, dec_seq):
    assert C_HEADS * dec_seq == ATT_TILE
    return pl.pallas_call(
        functools.partial(_bias_tile_kernel, dec_seq),
        out_shape=(jax.ShapeDtypeStruct((C_HEADS, REL_NEAR_TILES, ATT_TK, ATT_TQ), F32),
                   jax.ShapeDtypeStruct((3, ATT_TILE, ATT_TILE), F32)),
        in_specs=[pl.BlockSpec(memory_space=pltpu.SMEM)],
        name="rel_bias_tiles",
    )(rel_bias)


def _sortable(s):
    bits = lax.bitcast_convert_type(s, jnp.int32)
    return bits ^ ((bits >> 31) & 0x7FFFFFFF)


def _c_prompt_kernel(topk, qiT_ref, wiT_ref, ki_ref, qT_ref, k_ref, vT_ref, btile_ref,
                     lower_ref, o_ref, key_ref, thr_ref, m_ref, acc_ref):
    tk, tq = ATT_TK, ATT_TQ
    qb = pl.program_id(1)
    n_tiles = (qb + 1) * (tq // tk)
    n_chunks = (n_tiles * tk + COUNT_CHUNK - 1) // COUNT_CHUNK
    row = lax.broadcasted_iota(jnp.int32, (tk, tq), 0)
    lane = lax.broadcasted_iota(jnp.int32, (tk, tq), 1)
    q_idx = qb * tq + lane
    w = wiT_ref[0] * (IDX_HEADS ** -0.5)

    def score_block(j, _):
        ks = pl.multiple_of(j * tk, tk)
        kib = ki_ref[0, pl.ds(ks, tk), :]
        s = jnp.zeros((tk, tq), F32)
        for hh in range(IDX_HEADS):
            d = jnp.dot(kib, qiT_ref[0, hh], preferred_element_type=F32)
            s = s + jnp.maximum(d, 0.0) * w[hh:hh + 1, :]
        s = jnp.where(ks + row <= q_idx, s, -jnp.inf)
        key_ref[pl.ds(ks, tk), :] = _sortable(s)
        return 0

    lax.fori_loop(0, n_chunks * (COUNT_CHUNK // tk), score_block, 0)

    def count(pred, thr):
        def chunk(c, cnt):
            base = pl.multiple_of(c * COUNT_CHUNK, COUNT_CHUNK)
            ind = jnp.where(pred(key_ref[pl.ds(base, COUNT_CHUNK), :], thr), 1, 0)
            return cnt + jnp.sum(ind.reshape(COUNT_CHUNK // 8, 8, tq), axis=0)
        cnt = lax.fori_loop(0, n_chunks, chunk, jnp.zeros((8, tq), jnp.int32))
        return jnp.sum(cnt, axis=0, keepdims=True)

    def bit_step(i, t_u):
        cand_u = t_u | lax.shift_left(jnp.int32(1), 31 - i)
        cnt = count(lambda x, thr: x >= thr, cand_u ^ INT_MIN)
        return jnp.where(cnt >= topk, cand_u, t_u)

    thr = lax.fori_loop(0, 32, bit_step, jnp.zeros((1, tq), jnp.int32)) ^ INT_MIN
    n_ge = count(lambda x, t: x >= t, thr)
    thr_ref[...] = thr

    @pl.when(jnp.max(jnp.abs(n_ge - topk)) > 0)
    def _():
        need = (topk - count(lambda x, t: x > t, thr)).astype(F32)

        def select_block(j, seen):
            ks = pl.multiple_of(j * tk, tk)
            blk = key_ref[pl.ds(ks, tk), :]
            eq = jnp.where(blk == thr, 1.0, 0.0)
            rank = jnp.dot(lower_ref[...], eq.astype(MXU_DTYPE), preferred_element_type=F32) + seen
            tie = jnp.where(rank < need, eq, 0.0)
            sel = jnp.where(blk > thr, 1.0, tie)
            sel = jnp.where(ks + row <= q_idx, sel, 0.0)
            key_ref[pl.ds(ks, tk), :] = sel.astype(jnp.int32)
            return seen + jnp.sum(eq, axis=0, keepdims=True)

        lax.fori_loop(0, n_tiles, select_block, jnp.zeros((1, tq), F32))
        thr_ref[...] = jnp.ones((1, tq), jnp.int32)

    m_ref[...] = jnp.full(m_ref.shape, NEG, F32)
    acc_ref[...] = jnp.zeros(acc_ref.shape, F32)

    def attend(j, near):
        ks = pl.multiple_of(j * tk, tk)
        sel = key_ref[pl.ds(ks, tk), :] >= thr_ref[...]
        for n in range(C_KV_HEADS):
            kb = k_ref[0, n, pl.ds(ks, tk), :]
            vb = vT_ref[0, n, :, pl.ds(ks, tk)]
            for g in range(C_GROUP):
                h = n * C_GROUP + g
                lg = jnp.dot(kb, qT_ref[0, h], preferred_element_type=F32)
                if near is not None:
                    lg = lg + btile_ref[h, near]
                lg = jnp.where(sel, lg, NEG)
                m_old = m_ref[h]
                m_new = jnp.maximum(m_old, jnp.max(lg, axis=0, keepdims=True))
                p = jnp.exp2(lg - m_new)
                acc_ref[h] = jnp.exp2(m_old - m_new) * acc_ref[h] + jnp.dot(
                    vb, p.astype(MXU_DTYPE), preferred_element_type=F32)
                m_ref[h] = m_new

    def far_tile(j, _):
        attend(j, None)
        return 0

    first_near = n_tiles - REL_NEAR_TILES
    lax.fori_loop(0, jnp.maximum(first_near, 0), far_tile, 0)
    for c in range(REL_NEAR_TILES):
        if c == 0:
            pl.when(first_near >= 0)(functools.partial(attend, first_near, 0))
        else:
            attend(first_near + c, c)
    for h in range(C_HEADS):
        acc = acc_ref[h]
        o_ref[0, h] = acc[:C_HEAD_DIM] / acc[C_HEAD_DIM:C_HEAD_DIM + 1]


def _c_prompt(q, k, v, qi, ki, wi, btiles):
    b, t, _ = q.shape
    topk = min(TOPK_MAX, t // 4)
    assert t % COUNT_CHUNK == 0 and COUNT_CHUNK > topk and t % ATT_TQ == 0
    qT = ((q * (C_HEAD_DIM ** -0.5 * LOG2E)).reshape(b, t, C_HEADS, C_HEAD_DIM)
          .transpose(0, 2, 3, 1).astype(MXU_DTYPE))
    kh = k.reshape(b, t, C_KV_HEADS, C_HEAD_DIM).transpose(0, 2, 1, 3).astype(MXU_DTYPE)
    vT = v.reshape(b, t, C_KV_HEADS, C_HEAD_DIM).transpose(0, 2, 3, 1)
    ones_rows = jnp.zeros((b, C_KV_HEADS, V_ROWS - C_HEAD_DIM, t), F32).at[:, :, 0].set(1.0)
    vT = jnp.concatenate([vT, ones_rows], axis=2).astype(MXU_DTYPE)
    qi_hi, qi_lo = _split_hi_lo_mxu(
        (qi * IDX_DIM ** -0.5).reshape(b, t, IDX_HEADS, IDX_DIM).transpose(0, 2, 3, 1))
    qiT = jnp.concatenate([qi_hi, qi_lo, qi_hi], axis=2)
    ki_hi, ki_lo = _split_hi_lo_mxu(ki)
    ki3 = jnp.concatenate([ki_hi, ki_hi, ki_lo], axis=2)
    wiT = wi.transpose(0, 2, 1)
    tq, tk = ATT_TQ, ATT_TK
    oT = pl.pallas_call(
        functools.partial(_c_prompt_kernel, topk),
        out_shape=jax.ShapeDtypeStruct((b, C_HEADS, C_HEAD_DIM, t), F32),
        grid=(b, t // tq),
        in_specs=[
            pl.BlockSpec((1, IDX_HEADS, 3 * IDX_DIM, tq), lambda i, j: (i, 0, 0, j)),
            pl.BlockSpec((1, IDX_HEADS, tq), lambda i, j: (i, 0, j)),
            pl.BlockSpec((1, t, 3 * IDX_DIM), lambda i, j: (i, 0, 0)),
            pl.BlockSpec((1, C_HEADS, C_HEAD_DIM, tq), lambda i, j: (i, 0, 0, j)),
            pl.BlockSpec((1, C_KV_HEADS, t, C_HEAD_DIM), lambda i, j: (i, 0, 0, 0)),
            pl.BlockSpec((1, C_KV_HEADS, V_ROWS, t), lambda i, j: (i, 0, 0, 0)),
            pl.BlockSpec((C_HEADS, REL_NEAR_TILES, tk, tq), lambda i, j: (0, 0, 0, 0),
                         pipeline_mode=pl.Buffered(1)),
            pl.BlockSpec((tk, tk), lambda i, j: (0, 0)),
        ],
        out_specs=pl.BlockSpec((1, C_HEADS, C_HEAD_DIM, tq), lambda i, j: (i, 0, 0, j)),
        scratch_shapes=[
            pltpu.VMEM((t, tq), jnp.int32),
            pltpu.VMEM((1, tq), jnp.int32),
            pltpu.VMEM((C_HEADS, 1, tq), F32),
            pltpu.VMEM((C_HEADS, V_ROWS, tq), F32),
        ],
        compiler_params=pltpu.CompilerParams(
            dimension_semantics=("parallel", "arbitrary"), vmem_limit_bytes=VMEM_LIMIT_BYTES),
        name="c_prompt",
    )(qiT, wiT, ki3, qT, kh, vT, btiles, _strict_lower(tk))
    return oT.transpose(0, 3, 1, 2).reshape(b, t, C_WIDTH)


_NT = (((1,), (1,)), ((), ()))


def _pad_rows(x, rows):
    return jnp.concatenate([x, jnp.zeros((rows - x.shape[0], x.shape[1]), x.dtype)], axis=0)


def _prefix_and_total(n, strict_before):
    i = np.arange(n)
    tri = (i[:, None] < i[None, :]) if strict_before else (i[:, None] > i[None, :])
    return jnp.asarray(np.concatenate([tri, np.ones((n, n), bool)], axis=1).astype(np.float32), MXU_DTYPE)


def _sb_sample_kernel(n_pages, pt_ref, q_ref, kn_ref, vn_ref, sufx_ref, *rest):
    k_pages, v_pages = rest[:n_pages], rest[n_pages:2 * n_pages]
    o_ref, carry_ref, acc_ref = rest[2 * n_pages:]
    t = q_ref.shape[1]
    rows, tk, hd = SB_HEADS * t, PAGE_SIZE, SB_HEAD_DIM
    q = (q_ref[0] * (SB_HEAD_DIM ** -0.5)).astype(MXU_DTYPE)
    q_heads = [q[:, h * hd:(h + 1) * hd] for h in range(SB_HEADS)]
    carry_ref[...] = jnp.zeros(carry_ref.shape, F32)
    acc_ref[...] = jnp.zeros(acc_ref.shape, F32)

    def step(k_head, v_head, valid):
        z = jnp.concatenate([lax.dot_general(q_heads[h], k_head(h), _NT, preferred_element_type=F32)
                             for h in range(SB_HEADS)], axis=0)
        sp = jnp.maximum(z, 0.0) + jnp.log(1.0 + jnp.exp(-jnp.abs(z)))
        lr = -sp if valid is None else jnp.where(valid, -sp, 0.0)
        hi, lo = _split_hi_lo_mxu(lr)
        both = (jnp.dot(hi, sufx_ref[...], preferred_element_type=F32)
                + jnp.dot(lo, sufx_ref[...], preferred_element_type=F32))
        a = jnp.exp(z - sp + both[:, :tk] + carry_ref[...])
        if valid is not None:
            a = jnp.where(valid, a, 0.0)
        a = a.astype(MXU_DTYPE)
        for h in range(SB_HEADS):
            acc_ref[h] += jnp.dot(a[h * t:(h + 1) * t], v_head(h), preferred_element_type=F32)
        carry_ref[...] += both[:, tk:]

    row_t = lax.broadcasted_iota(jnp.int32, (rows, tk), 0) & (t - 1)
    lane = lax.broadcasted_iota(jnp.int32, (rows, tk), 1)
    kn, vn = kn_ref[0], vn_ref[0]
    step(lambda h: _pad_rows(kn[:, h * hd:(h + 1) * hd], tk).astype(MXU_DTYPE),
         lambda h: _pad_rows(vn[:, h * hd:(h + 1) * hd], tk).astype(MXU_DTYPE),
         lane < row_t)
    for p in reversed(range(n_pages)):
        @pl.when(jnp.max(carry_ref[...]) > SB_EXIT)
        def _(p=p):
            step(lambda h: k_pages[p][:, h, :].astype(MXU_DTYPE),
                 lambda h: v_pages[p][:, h, :].astype(MXU_DTYPE), None)

    for h in range(SB_HEADS):
        o_ref[0, :, h * hd:(h + 1) * hd] = acc_ref[h]


def _page_specs(n_pages, layer, page_shape):
    zeros = (0,) * len(page_shape)
    return [pl.BlockSpec((None, None) + page_shape, lambda i, pt, p=p: (layer, pt[i, p]) + zeros)
            for p in range(n_pages)]


def _sb_sample(q, k, v, cache_k, cache_v, layer, page_table):
    b, t, _ = q.shape
    assert t & (t - 1) == 0 and t <= PAGE_SIZE
    n_pages = page_table.shape[1]
    rows = SB_HEADS * t
    page = (PAGE_SIZE, SB_HEADS, SB_HEAD_DIM)
    tok = pl.BlockSpec((1, t, SB_WIDTH), lambda i, pt: (i, 0, 0))
    return pl.pallas_call(
        functools.partial(_sb_sample_kernel, n_pages),
        out_shape=jax.ShapeDtypeStruct((b, t, SB_WIDTH), F32),
        grid_spec=pltpu.PrefetchScalarGridSpec(
            num_scalar_prefetch=1,
            grid=(b,),
            in_specs=[tok, tok, tok,
                      pl.BlockSpec((PAGE_SIZE, 2 * PAGE_SIZE), lambda i, pt: (0, 0))]
            + _page_specs(n_pages, layer, page) + _page_specs(n_pages, layer, page),
            out_specs=tok,
            scratch_shapes=[pltpu.VMEM((rows, PAGE_SIZE), F32),
                            pltpu.VMEM((SB_HEADS, t, SB_HEAD_DIM), F32)],
        ),
        compiler_params=pltpu.CompilerParams(
            dimension_semantics=("arbitrary",), vmem_limit_bytes=VMEM_LIMIT_BYTES),
        name="sb_sample",
    )(page_table, q, k, v, _prefix_and_total(PAGE_SIZE, False),
      *([cache_k] * n_pages), *([cache_v] * n_pages))


def _c_sample_kernel(n_pages, topk, pt_ref, qi_ref, w_ref, q_ref, kin_ref, kn_ref, vn_ref,
                     sbias_ref, prex_ref, *rest):
    idx_pages, k_pages, v_pages = rest[:n_pages], rest[n_pages:2 * n_pages], rest[2 * n_pages:3 * n_pages]
    o_ref, sc_ref, lg_ref = rest[3 * n_pages:]
    t = kin_ref.shape[1]
    tk = PAGE_SIZE
    nb = n_pages + 1
    ih = IDX_HEADS * t
    rows = C_HEADS * t
    row8 = lax.broadcasted_iota(jnp.int32, (t, tk), 0)
    lane8 = lax.broadcasted_iota(jnp.int32, (t, tk), 1)
    new_valid = lane8 <= row8
    qi_cat = qi_ref[0]
    w = w_ref[0] * (IDX_HEADS ** -0.5)

    def scores(idx_blk, valid):
        hi, lo = _split_hi_lo_mxu(idx_blk)
        s = lax.dot_general(qi_cat, hi, _NT, preferred_element_type=F32)
        s = s[:ih] + s[ih:] + lax.dot_general(qi_cat[:ih], lo, _NT, preferred_element_type=F32)
        r = jnp.maximum(s, 0.0) * w
        sc = r[0:t]
        for hh in range(1, IDX_HEADS):
            sc = sc + r[hh * t:(hh + 1) * t]
        if valid is not None:
            sc = jnp.where(valid, sc, -jnp.inf)
        return _sortable(sc)

    for p in range(n_pages):
        sc_ref[:, p * tk:(p + 1) * tk] = scores(idx_pages[p][...], None)
    sc_ref[:, n_pages * tk:] = scores(_pad_rows(kin_ref[0], tk), new_valid)

    def count(pred, thr):
        return jnp.sum(jnp.where(pred(sc_ref[...], thr), 1.0, 0.0), axis=1, keepdims=True)

    def bit_step(i, t_u):
        cand_u = t_u | lax.shift_left(jnp.int32(1), 31 - i)
        cnt = count(lambda x, thr: x >= thr, cand_u ^ INT_MIN)
        return jnp.where(cnt >= topk, cand_u, t_u)

    thr = lax.fori_loop(0, 32, bit_step, jnp.zeros((t, 1), jnp.int32)) ^ INT_MIN
    need = topk - count(lambda x, th: x > th, thr)

    seen = jnp.zeros((t, tk), F32)
    for blk in range(nb):
        x = sc_ref[:, blk * tk:(blk + 1) * tk]
        eq = jnp.where(x == thr, 1.0, 0.0)
        both = jnp.dot(eq.astype(MXU_DTYPE), prex_ref[...], preferred_element_type=F32)
        tie = jnp.where(both[:, :tk] + seen < need, eq, 0.0)
        sel = jnp.where(x > thr, 1.0, tie)
        if blk == n_pages:
            sel = jnp.where(new_valid, sel, 0.0)
        sc_ref[:, blk * tk:(blk + 1) * tk] = sel.astype(jnp.int32)
        seen = seen + both[:, tk:]

    q_rows = q_ref[0]
    grp = C_GROUP * t
    hd = C_HEAD_DIM

    def kv_block(pages, new_ref, blk, n):
        if blk < n_pages:
            return pages[blk][:, n, :].astype(MXU_DTYPE)
        return _pad_rows(new_ref[0][:, n * hd:(n + 1) * hd], tk).astype(MXU_DTYPE)

    m = jnp.full((rows, tk), NEG, F32)
    for blk in range(nb):
        bias = sbias_ref[2] if blk < n_pages - 1 else sbias_ref[blk - (n_pages - 1)]
        lg = jnp.concatenate(
            [lax.dot_general(q_rows[n * grp:(n + 1) * grp], kv_block(k_pages, kn_ref, blk, n), _NT,
                             preferred_element_type=F32) for n in range(C_KV_HEADS)], axis=0) + bias
        sel = jnp.concatenate([sc_ref[:, blk * tk:(blk + 1) * tk]] * C_HEADS, axis=0) != 0
        lg = jnp.where(sel, lg, NEG)
        lg_ref[:, blk * tk:(blk + 1) * tk] = lg
        m = jnp.maximum(m, lg)
    m_row = jnp.max(m, axis=1, keepdims=True)
    lsum = jnp.zeros((rows, tk), F32)
    accs = [jnp.zeros((grp, hd), F32) for _ in range(C_KV_HEADS)]
    for blk in range(nb):
        p = jnp.exp(lg_ref[:, blk * tk:(blk + 1) * tk] - m_row)
        lsum = lsum + p
        p = p.astype(MXU_DTYPE)
        accs = [acc + jnp.dot(p[n * grp:(n + 1) * grp], kv_block(v_pages, vn_ref, blk, n),
                              preferred_element_type=F32) for n, acc in enumerate(accs)]
    l_row = jnp.sum(lsum, axis=1, keepdims=True)
    for n, acc in enumerate(accs):
        o = acc / l_row[n * grp:(n + 1) * grp]
        for g in range(C_GROUP):
            o_ref[0, n * C_GROUP + g] = o[g * t:(g + 1) * t]


def _c_sample(q, k, v, qi, ki, wi, cache_k, cache_v, cache_idx, layer, page_table, sbias):
    b, t, _ = q.shape
    n_pages = page_table.shape[1]
    topk = min(TOPK_MAX, (n_pages * PAGE_SIZE + t) // 4)
    rows = C_HEADS * t
    ih = IDX_HEADS * t
    assert rows == ATT_TILE and t <= PAGE_SIZE
    q_rows = ((q * C_HEAD_DIM ** -0.5).reshape(b, t, C_HEADS, C_HEAD_DIM).transpose(0, 2, 1, 3)
              .reshape(b, rows, C_HEAD_DIM).astype(MXU_DTYPE))
    qi_rows = ((qi * IDX_DIM ** -0.5).reshape(b, t, IDX_HEADS, IDX_DIM).transpose(0, 2, 1, 3)
               .reshape(b, ih, IDX_DIM))
    qi_cat = jnp.concatenate(_split_hi_lo_mxu(qi_rows), axis=1)
    w_col = wi.transpose(0, 2, 1).reshape(b, ih, 1)
    kv_page = (PAGE_SIZE, C_KV_HEADS, C_HEAD_DIM)
    per_seq = lambda r, c: pl.BlockSpec((1, r, c), lambda i, pt: (i, 0, 0))
    const2 = lambda i, pt: (0, 0)
    o = pl.pallas_call(
        functools.partial(_c_sample_kernel, n_pages, topk),
        out_shape=jax.ShapeDtypeStruct((b, C_HEADS, t, C_HEAD_DIM), F32),
        grid_spec=pltpu.PrefetchScalarGridSpec(
            num_scalar_prefetch=1,
            grid=(b,),
            in_specs=[per_seq(2 * ih, IDX_DIM), per_seq(ih, 1), per_seq(rows, C_HEAD_DIM),
                      per_seq(t, IDX_DIM), per_seq(t, C_KV_WIDTH), per_seq(t, C_KV_WIDTH),
                      pl.BlockSpec((3, ATT_TILE, ATT_TILE), lambda i, pt: (0, 0, 0)),
                      pl.BlockSpec((PAGE_SIZE, 2 * PAGE_SIZE), const2)]
            + _page_specs(n_pages, layer, (PAGE_SIZE, IDX_DIM)) + _page_specs(n_pages, layer, kv_page)
            + _page_specs(n_pages, layer, kv_page),
            out_specs=pl.BlockSpec((1, C_HEADS, t, C_HEAD_DIM), lambda i, pt: (i, 0, 0, 0)),
            scratch_shapes=[pltpu.VMEM((t, (n_pages + 1) * PAGE_SIZE), jnp.int32),
                            pltpu.VMEM((rows, (n_pages + 1) * PAGE_SIZE), F32)],
        ),
        compiler_params=pltpu.CompilerParams(
            dimension_semantics=("arbitrary",), vmem_limit_bytes=VMEM_LIMIT_BYTES),
        name="c_sample",
    )(page_table, qi_cat, w_col, q_rows, ki, k, v, sbias, _prefix_and_total(PAGE_SIZE, True),
      *([cache_idx] * n_pages), *([cache_k] * n_pages), *([cache_v] * n_pages))
    return o.transpose(0, 2, 1, 3).reshape(b, t, C_WIDTH)


def _pool_mix(u_ext, pos, w_pool, pool_scale):
    b, l, c = u_ext.shape
    t = l - POOL_BUF
    cs = jnp.concatenate([jnp.zeros((b, 1, c), F32), jnp.cumsum(u_ext, axis=1)], axis=1)
    u_new = u_ext[:, POOL_BUF:]
    hi = cs[:, POOL_BUF + 1:]
    outs = []
    for g, w in enumerate(POOL_WINDOWS):
        sl = slice(g * POOL_GROUP, (g + 1) * POOL_GROUP)
        lo = cs[:, POOL_BUF + 1 - w:POOL_BUF + 1 - w + t, sl]
        cnt = jnp.minimum(pos + 1, w).astype(F32)[None, :, None]
        outs.append((hi[..., sl] - lo) / cnt - u_new[..., sl])
    d = jnp.stack(outs, axis=2)
    y = jnp.einsum('btgc,gcd->btgd', d, w_pool).reshape(b, t, POOL_WIDTH)
    return y * pool_scale


def _ab_prompt_mix(u, q, k, v, w_pool, pool_scale):
    b, t, _ = u.shape
    pos = jnp.arange(t, dtype=jnp.int32)
    u_ext = jnp.concatenate([jnp.zeros((b, POOL_BUF, POOL_WIDTH), F32), u], axis=1)
    y_pool = _pool_mix(u_ext, pos, w_pool, pool_scale)
    y_sb = _sb_prompt(q, k, v)
    return jnp.concatenate([y_pool, y_sb], axis=-1), u_ext[:, -POOL_BUF:]


def _ab_sample_mix(u, q, k, v, pool_buf, cache_k, cache_v, layer, page_table, w_pool, pool_scale):
    t = u.shape[1]
    q_pos = page_table.shape[1] * PAGE_SIZE + jnp.arange(t, dtype=jnp.int32)
    u_ext = jnp.concatenate([pool_buf, u], axis=1)
    y_pool = _pool_mix(u_ext, q_pos, w_pool, pool_scale)
    y_sb = _sb_sample(q, k, v, cache_k, cache_v, layer, page_table)
    return jnp.concatenate([y_pool, y_sb], axis=-1), u_ext[:, -POOL_BUF:]


def _segment_mean_matrix(width):
    seg = np.arange(width) // C_HEAD_DIM
    return jnp.asarray((seg[:, None] == seg[None, :]).astype(np.float32) / C_HEAD_DIM, BF16)


def kernel(x_prompt, x_sample, state_pool, cache_b_k, cache_b_v, cache_c_k, cache_c_v, cache_c_idx, page_table,
           g_ffn, w_ffn_gate, w_ffn_up, w_ffn_down, g_mix, w_in_ab, w_pool, pool_scale, w_out_ab,
           w_in_c, g_q, g_k, w_out_c, rel_bias):
    bp, tp, _ = x_prompt.shape
    bs, ts, _ = x_sample.shape
    mp = bp * tp
    x = jnp.concatenate([x_prompt.reshape(mp, D_MODEL), x_sample.reshape(bs * ts, D_MODEL)], axis=0)

    wg = w_ffn_gate.astype(BF16)
    wu = w_ffn_up.astype(BF16)
    wd = w_ffn_down.astype(BF16)
    w_ab = w_in_ab.astype(BF16)
    w_oab = w_out_ab.astype(BF16)
    w_c = jnp.pad(w_in_c, ((0, 0), (0, 0), (0, C_IN_PAD - C_IN))).astype(BF16)
    w_oc = w_out_c.astype(BF16)
    segq = _segment_mean_matrix(C_WIDTH)
    segk = _segment_mean_matrix(C_KV_WIDTH)
    btiles, sbias = _bias_tiles(rel_bias, ts)

    def split(a):
        return a[:mp].reshape(bp, tp, -1), a[mp:].reshape(bs, ts, -1)

    pool_p, pool_s, kbp, vbp, kbs, vbs = [], [], [], [], [], []
    kcp, vcp, icp, kcs, vcs, ics = [], [], [], [], [], []
    for layer in range(DEPTH):
        j = layer // 2
        x = _ffn(x, g_ffn[layer, 0][None], wg[layer, 0], wu[layer, 0], wd[layer, 0])
        g = g_mix[layer][None]
        if layer % 2 == 0:
            u, q, k, v = _proj_ab(x, g, w_ab[j])
            (u_p, u_s), (q_p, q_s), (k_p, k_s), (v_p, v_s) = split(u), split(q), split(k), split(v)
            y_p, sp = _ab_prompt_mix(u_p, q_p, k_p, v_p, w_pool[j], pool_scale[j])
            y_s, ss = _ab_sample_mix(u_s, q_s, k_s, v_s, state_pool[j], cache_b_k, cache_b_v, j,
                                     page_table, w_pool[j], pool_scale[j])
            pool_p.append(sp); pool_s.append(ss)
            kbp.append(k_p.reshape(bp, tp, SB_HEADS, SB_HEAD_DIM))
            vbp.append(v_p.reshape(bp, tp, SB_HEADS, SB_HEAD_DIM))
            kbs.append(k_s.reshape(bs, ts, SB_HEADS, SB_HEAD_DIM))
            vbs.append(v_s.reshape(bs, ts, SB_HEADS, SB_HEAD_DIM))
            w_out = w_oab[j]
        else:
            gq = jnp.tile(g_q[j], C_HEADS)[None]
            gk = jnp.tile(g_k[j], C_KV_HEADS)[None]
            q, k, v, qi, tail = _proj_c(x, g, w_c[j], gq, gk, segq, segk)
            ki = tail[:, :IDX_DIM]
            wi = tail[:, IDX_DIM:IDX_DIM + IDX_HEADS]
            (q_p, q_s), (k_p, k_s), (v_p, v_s) = split(q), split(k), split(v)
            (qi_p, qi_s), (ki_p, ki_s), (wi_p, wi_s) = split(qi), split(ki), split(wi)
            y_p = _c_prompt(q_p, k_p, v_p, qi_p, ki_p, wi_p, btiles)
            y_s = _c_sample(q_s, k_s, v_s, qi_s, ki_s, wi_s, cache_c_k, cache_c_v, cache_c_idx,
                            j, page_table, sbias)
            kcp.append(k_p.reshape(bp, tp, C_KV_HEADS, C_HEAD_DIM))
            vcp.append(v_p.reshape(bp, tp, C_KV_HEADS, C_HEAD_DIM))
            icp.append(ki_p)
            kcs.append(k_s.reshape(bs, ts, C_KV_HEADS, C_HEAD_DIM))
            vcs.append(v_s.reshape(bs, ts, C_KV_HEADS, C_HEAD_DIM))
            ics.append(ki_s)
            w_out = w_oc[j]
        y = jnp.concatenate([y_p.reshape(mp, D_MODEL), y_s.reshape(bs * ts, D_MODEL)], axis=0)
        x = _out_proj(x, y, w_out)
        x = _ffn(x, g_ffn[layer, 1][None], wg[layer, 1], wu[layer, 1], wd[layer, 1])

    return (x[:mp].reshape(bp, tp, D_MODEL), x[mp:].reshape(bs, ts, D_MODEL),
            jnp.stack(pool_p), jnp.stack(pool_s),
            jnp.stack(kbp), jnp.stack(vbp), jnp.stack(kbs), jnp.stack(vbs),
            jnp.stack(kcp), jnp.stack(vcp), jnp.stack(icp),
            jnp.stack(kcs), jnp.stack(vcs), jnp.stack(ics))
```

```python
import functools
import math

import jax
import jax.numpy as jnp
import numpy as np
from jax import lax
from jax.experimental import pallas as pl
from jax.experimental.pallas import tpu as pltpu

F32 = jnp.float32
BF16 = jnp.bfloat16

D_MODEL = 1024
DEPTH = 4
D_FF = 2816
POOL_WINDOWS = (2, 4, 8, 16)
POOL_WIDTH = 512
POOL_GROUP = 128
POOL_BUF = 15
SB_HEADS = 8
SB_HEAD_DIM = 64
SB_WIDTH = 512
AB_IN = POOL_WIDTH + 3 * SB_WIDTH
C_HEAD_DIM = 64
C_HEADS = 16
C_KV_HEADS = 4
C_GROUP = 4
C_WIDTH = 1024
C_KV_WIDTH = 256
IDX_HEADS = 8
IDX_DIM = 64
TOPK_MAX = 256
C_IN = 2120
C_IN_PAD = 2176
REL_BUCKETS = 32
REL_MAX_EXACT = 16
REL_MAX_DIST = 128
Q_BLOCK = 128
PAGE_SIZE = 128
EPS = 1e-6
NEG = -1e30

VMEM_LIMIT_BYTES = 56 * 1024 * 1024
FF_CHUNK = 256
TOKEN_TILE = 512


def _rms(x, g):
    ms = jnp.mean(x * x, axis=-1, keepdims=True)
    return x * lax.rsqrt(ms + EPS) * g


def _ffn_kernel(x_ref, g_ref, wg_ref, wu_ref, wd_ref, o_ref):
    x = x_ref[...]
    h = _rms(x, g_ref[...]).astype(BF16)
    acc = x
    for c in range(D_FF // FF_CHUNK):
        sl = slice(c * FF_CHUNK, (c + 1) * FF_CHUNK)
        gate = jnp.dot(h, wg_ref[:, sl], preferred_element_type=F32)
        up = jnp.dot(h, wu_ref[:, sl], preferred_element_type=F32)
        act = (0.5 * gate * jax.nn.sigmoid(gate) * up).astype(BF16)
        acc = acc + jnp.dot(act, wd_ref[sl, :], preferred_element_type=F32)
    o_ref[...] = acc


def _ffn(x, g, wg, wu, wd):
    m = x.shape[0]
    resident = dict(pipeline_mode=pl.Buffered(1))
    return pl.pallas_call(
        _ffn_kernel,
        out_shape=jax.ShapeDtypeStruct((m, D_MODEL), F32),
        grid=(m // TOKEN_TILE,),
        in_specs=[
            pl.BlockSpec((TOKEN_TILE, D_MODEL), lambda i: (i, 0)),
            pl.BlockSpec((1, D_MODEL), lambda i: (0, 0)),
            pl.BlockSpec((D_MODEL, D_FF), lambda i: (0, 0), **resident),
            pl.BlockSpec((D_MODEL, D_FF), lambda i: (0, 0), **resident),
            pl.BlockSpec((D_FF, D_MODEL), lambda i: (0, 0), **resident),
        ],
        out_specs=pl.BlockSpec((TOKEN_TILE, D_MODEL), lambda i: (i, 0)),
        compiler_params=pltpu.CompilerParams(
            dimension_semantics=("parallel",), vmem_limit_bytes=VMEM_LIMIT_BYTES),
        name="ffn_half",
    )(x, g, wg, wu, wd)


def _proj_ab_kernel(x_ref, g_ref, w_ref, u_ref, q_ref, k_ref, v_ref):
    h = _rms(x_ref[...], g_ref[...]).astype(BF16)
    p = jnp.dot(h, w_ref[...], preferred_element_type=F32)
    u_ref[...] = p[:, :POOL_WIDTH]
    q_ref[...] = p[:, POOL_WIDTH:POOL_WIDTH + SB_WIDTH]
    k_ref[...] = p[:, POOL_WIDTH + SB_WIDTH:POOL_WIDTH + 2 * SB_WIDTH]
    v_ref[...] = p[:, POOL_WIDTH + 2 * SB_WIDTH:]


def _proj_ab(x, g, w):
    m = x.shape[0]
    out = jax.ShapeDtypeStruct((m, SB_WIDTH), F32)
    spec = pl.BlockSpec((TOKEN_TILE, SB_WIDTH), lambda i: (i, 0))
    return pl.pallas_call(
        _proj_ab_kernel,
        out_shape=(out, out, out, out),
        grid=(m // TOKEN_TILE,),
        in_specs=[
            pl.BlockSpec((TOKEN_TILE, D_MODEL), lambda i: (i, 0)),
            pl.BlockSpec((1, D_MODEL), lambda i: (0, 0)),
            pl.BlockSpec((D_MODEL, AB_IN), lambda i: (0, 0), pipeline_mode=pl.Buffered(1)),
        ],
        out_specs=(spec, spec, spec, spec),
        compiler_params=pltpu.CompilerParams(
            dimension_semantics=("parallel",), vmem_limit_bytes=VMEM_LIMIT_BYTES),
        name="proj_ab",
    )(x, g, w)


def _split_hi_lo(x):
    hi = x.astype(BF16)
    lo = (x - hi.astype(F32)).astype(BF16)
    return hi, lo


def _head_rms(x, seg_mean_ref, gain):
    hi, lo = _split_hi_lo(x * x)
    ms = (jnp.dot(hi, seg_mean_ref[...], preferred_element_type=F32)
          + jnp.dot(lo, seg_mean_ref[...], preferred_element_type=F32))
    return x * lax.rsqrt(ms + EPS) * gain


def _proj_c_kernel(x_ref, g_ref, w_ref, gq_ref, gk_ref, segq_ref, segk_ref,
                   q_ref, k_ref, v_ref, qi_ref, tail_ref):
    h = _rms(x_ref[...], g_ref[...]).astype(BF16)
    p = jnp.dot(h, w_ref[...], preferred_element_type=F32)
    o_k = C_WIDTH
    o_v = o_k + C_KV_WIDTH
    o_qi = o_v + C_KV_WIDTH
    o_tail = o_qi + IDX_HEADS * IDX_DIM
    q_ref[...] = _head_rms(p[:, :o_k], segq_ref, gq_ref[...])
    k_ref[...] = _head_rms(p[:, o_k:o_v], segk_ref, gk_ref[...])
    v_ref[...] = p[:, o_v:o_qi]
    qi_ref[...] = p[:, o_qi:o_tail]
    tail_ref[...] = p[:, o_tail:]


def _proj_c(x, g, w, gq, gk, segq, segk):
    m = x.shape[0]
    tail = C_IN_PAD - (C_WIDTH + 2 * C_KV_WIDTH + IDX_HEADS * IDX_DIM)
    widths = (C_WIDTH, C_KV_WIDTH, C_KV_WIDTH, IDX_HEADS * IDX_DIM, tail)
    const = lambda i: (0, 0)
    return pl.pallas_call(
        _proj_c_kernel,
        out_shape=tuple(jax.ShapeDtypeStruct((m, n), F32) for n in widths),
        grid=(m // TOKEN_TILE,),
        in_specs=[
            pl.BlockSpec((TOKEN_TILE, D_MODEL), lambda i: (i, 0)),
            pl.BlockSpec((1, D_MODEL), const),
            pl.BlockSpec((D_MODEL, C_IN_PAD), const, pipeline_mode=pl.Buffered(1)),
            pl.BlockSpec((1, C_WIDTH), const),
            pl.BlockSpec((1, C_KV_WIDTH), const),
            pl.BlockSpec((C_WIDTH, C_WIDTH), const, pipeline_mode=pl.Buffered(1)),
            pl.BlockSpec((C_KV_WIDTH, C_KV_WIDTH), const, pipeline_mode=pl.Buffered(1)),
        ],
        out_specs=tuple(pl.BlockSpec((TOKEN_TILE, n), lambda i: (i, 0)) for n in widths),
        compiler_params=pltpu.CompilerParams(
            dimension_semantics=("parallel",), vmem_limit_bytes=VMEM_LIMIT_BYTES),
        name="proj_c",
    )(x, g, w, gq, gk, segq, segk)


def _out_proj_kernel(x_ref, y_ref, w_ref, o_ref):
    o_ref[...] = x_ref[...] + jnp.dot(y_ref[...].astype(BF16), w_ref[...],
                                      preferred_element_type=F32)


def _out_proj(x, y, w):
    m = x.shape[0]
    tile = pl.BlockSpec((TOKEN_TILE, D_MODEL), lambda i: (i, 0))
    return pl.pallas_call(
        _out_proj_kernel,
        out_shape=jax.ShapeDtypeStruct((m, D_MODEL), F32),
        grid=(m // TOKEN_TILE,),
        in_specs=[tile, tile,
                  pl.BlockSpec((D_MODEL, D_MODEL), lambda i: (0, 0), pipeline_mode=pl.Buffered(1))],
        out_specs=tile,
        compiler_params=pltpu.CompilerParams(
            dimension_semantics=("parallel",), vmem_limit_bytes=VMEM_LIMIT_BYTES),
        name="out_proj",
    )(x, y, w)


ATT_TILE = 128
ATT_TK = 128
ATT_TQ = 256
SB_LOCKSTEP = 2
SB_EXIT = -104.0
MXU_DTYPE = BF16


def _split_hi_lo_mxu(x):
    hi = x.astype(MXU_DTYPE)
    lo = (x - hi.astype(F32)).astype(MXU_DTYPE)
    return hi, lo


def _sb_prompt_kernel(qT_ref, k_ref, vT_ref, upper_ref, o_ref):
    tk, tq = ATT_TK, ATT_TQ
    qb = pl.program_id(1)
    row = lax.broadcasted_iota(jnp.int32, (tk, tq), 0)
    lane = lax.broadcasted_iota(jnp.int32, (tk, tq), 1)
    q_idx = qb * tq + lane
    upper = upper_ref[...]
    last_tile = qb * (tq // tk) + (tq // tk - 1)

    def head_step(h, ks, valid, carry, acc):
        kb = k_ref[0, h, pl.ds(ks, tk), :]
        z = jnp.dot(kb, qT_ref[0, h], preferred_element_type=F32)
        sp = jnp.maximum(z, 0.0) + jnp.log(1.0 + jnp.exp(-jnp.abs(z)))
        lr = jnp.where(valid, -sp, 0.0)
        hi, lo = _split_hi_lo_mxu(lr)
        between = (jnp.dot(upper, hi, preferred_element_type=F32)
                   + jnp.dot(upper, lo, preferred_element_type=F32) + carry)
        a = jnp.where(valid, jnp.exp(z - sp + between), 0.0)
        vb = vT_ref[0, h, :, pl.ds(ks, tk)]
        acc = acc + jnp.dot(vb, a.astype(MXU_DTYPE), preferred_element_type=F32)
        return carry + jnp.sum(lr, axis=0, keepdims=True), acc

    for h0 in range(0, SB_HEADS, SB_LOCKSTEP):
        heads = range(h0, h0 + SB_LOCKSTEP)

        def body(state, heads=heads):
            j, _, carries, accs = state
            ks = pl.multiple_of(j * tk, tk)
            valid = (ks + row) < q_idx
            out = [head_step(h, ks, valid, c, a) for h, c, a in zip(heads, carries, accs)]
            carries = tuple(o[0] for o in out)
            top = functools.reduce(jnp.maximum, [jnp.max(c) for c in carries])
            return j - 1, top, carries, tuple(o[1] for o in out)

        def cond(state):
            j, top, _, _ = state
            return jnp.logical_and(j >= 0, top > SB_EXIT)

        init = (last_tile, jnp.float32(0.0),
                tuple(jnp.zeros((1, tq), F32) for _ in heads),
                tuple(jnp.zeros((SB_HEAD_DIM, tq), F32) for _ in heads))
        _, _, _, accs = lax.while_loop(cond, body, init)
        for h, acc in zip(heads, accs):
            o_ref[0, h] = acc


def _strict_upper(n):
    i = np.arange(n)
    return jnp.asarray((i[None, :] > i[:, None]).astype(np.float32), MXU_DTYPE)


def _strict_lower(n):
    i = np.arange(n)
    return jnp.asarray((i[None, :] < i[:, None]).astype(np.float32), MXU_DTYPE)


def _sb_prompt(q, k, v):
    b, t, _ = q.shape
    scale = SB_HEAD_DIM ** -0.5
    qT = (q * scale).reshape(b, t, SB_HEADS, SB_HEAD_DIM).transpose(0, 2, 3, 1).astype(MXU_DTYPE)
    kh = k.reshape(b, t, SB_HEADS, SB_HEAD_DIM).transpose(0, 2, 1, 3).astype(MXU_DTYPE)
    vT = v.reshape(b, t, SB_HEADS, SB_HEAD_DIM).transpose(0, 2, 3, 1).astype(MXU_DTYPE)
    oT = pl.pallas_call(
        _sb_prompt_kernel,
        out_shape=jax.ShapeDtypeStruct((b, SB_HEADS, SB_HEAD_DIM, t), F32),
        grid=(b, t // ATT_TQ),
        in_specs=[
            pl.BlockSpec((1, SB_HEADS, SB_HEAD_DIM, ATT_TQ), lambda i, j: (i, 0, 0, j)),
            pl.BlockSpec((1, SB_HEADS, t, SB_HEAD_DIM), lambda i, j: (i, 0, 0, 0)),
            pl.BlockSpec((1, SB_HEADS, SB_HEAD_DIM, t), lambda i, j: (i, 0, 0, 0)),
            pl.BlockSpec((ATT_TK, ATT_TK), lambda i, j: (0, 0)),
        ],
        out_specs=pl.BlockSpec((1, SB_HEADS, SB_HEAD_DIM, ATT_TQ), lambda i, j: (i, 0, 0, j)),
        compiler_params=pltpu.CompilerParams(
            dimension_semantics=("parallel", "arbitrary"), vmem_limit_bytes=VMEM_LIMIT_BYTES),
        name="sb_prompt",
    )(qT, kh, vT, _strict_upper(ATT_TK))
    return oT.transpose(0, 3, 1, 2).reshape(b, t, SB_WIDTH)


INT_MIN = -2 ** 31
COUNT_CHUNK = 4 * ATT_TK
LOG2E = math.log2(math.e)
C_TQ = 128
REL_NEAR_TILES = C_TQ // ATT_TK + 1
REL_LAST_BUCKET_FROM = math.ceil(REL_MAX_EXACT * (REL_MAX_DIST / REL_MAX_EXACT) ** (
    (REL_BUCKETS - 1 - REL_MAX_EXACT) / (REL_BUCKETS - REL_MAX_EXACT)))
assert ATT_TK + 1 >= REL_LAST_BUCKET_FROM
V_ROWS = C_HEAD_DIM + 16


def _bucket_of(dist):
    n = jnp.maximum(dist, 0)
    nf = jnp.maximum(n, 1).astype(F32)
    large = REL_MAX_EXACT + (jnp.log(nf / REL_MAX_EXACT) / math.log(REL_MAX_DIST / REL_MAX_EXACT)
                             * (REL_BUCKETS - REL_MAX_EXACT)).astype(jnp.int32)
    return jnp.where(n < REL_MAX_EXACT, n, jnp.minimum(large, REL_BUCKETS - 1))


def _bias_tile_kernel(dec_seq, relb_ref, o_ref, s_ref):
    krow = lax.broadcasted_iota(jnp.int32, (ATT_TK, C_TQ), 0)
    qlane = lax.broadcasted_iota(jnp.int32, (ATT_TK, C_TQ), 1)
    for c in range(REL_NEAR_TILES):
        bucket = _bucket_of((1 - c) * ATT_TK + qlane - krow)
        for h in range(C_HEADS):
            tile = jnp.zeros((ATT_TK, C_TQ), F32)
            for b in range(REL_BUCKETS):
                tile = jnp.where(bucket == b, relb_ref[b, h], tile)
            o_ref[h, c] = (tile - relb_ref[REL_BUCKETS - 1, h]) * LOG2E
    t = ATT_TILE
    row = lax.broadcasted_iota(jnp.int32, (t, t), 0)
    lane = lax.broadcasted_iota(jnp.int32, (t, t), 1)
    row_head = row >> int(math.log2(dec_seq))
    row_t = row & (dec_seq - 1)
    buckets = (_bucket_of(t + row_t - lane), _bucket_of(row_t - lane),
               jnp.full((t, t), REL_BUCKETS - 1, jnp.int32))
    tiles = [jnp.zeros((t, t), F32) for _ in buckets]
    for b in range(REL_BUCKETS):
        by_head = jnp.zeros((t, t), F32)
        for h in range(C_HEADS):
            by_head = jnp.where(row_head == h, relb_ref[b, h], by_head)
        tiles = [jnp.where(bk == b, by_head, tl) for bk, tl in zip(buckets, tiles)]
    for c, tl in enumerate(tiles):
        s_ref[c] = tl


def _bias_tiles(rel_bias, dec_seq):
    assert C_HEADS * dec_seq == ATT_TILE
    return pl.pallas_call(
        functools.partial(_bias_tile_kernel, dec_seq),
        out_shape=(jax.ShapeDtypeStruct((C_HEADS, REL_NEAR_TILES, ATT_TK, C_TQ), F32),
                   jax.ShapeDtypeStruct((3, ATT_TILE, ATT_TILE), F32)),
        in_specs=[pl.BlockSpec(memory_space=pltpu.SMEM)],
        name="rel_bias_tiles",
    )(rel_bias)


def _sortable(s):
    bits = lax.bitcast_convert_type(s, jnp.int32)
    return bits ^ ((bits >> 31) & 0x7FFFFFFF)


def _c_prompt_kernel(topk, qiT_ref, wiT_ref, ki_ref, qT_ref, k_ref, vT_ref, btile_ref,
                     lower_ref, o_ref, key_ref, thr_ref, m_ref, acc_ref):
    tk, tq = ATT_TK, C_TQ
    qb = pl.program_id(1)
    n_tiles = (qb + 1) * (tq // tk)
    n_chunks = (n_tiles * tk + COUNT_CHUNK - 1) // COUNT_CHUNK
    row = lax.broadcasted_iota(jnp.int32, (tk, tq), 0)
    lane = lax.broadcasted_iota(jnp.int32, (tk, tq), 1)
    q_idx = qb * tq + lane
    w = wiT_ref[0] * (IDX_HEADS ** -0.5)

    def score_block(j, _):
        ks = pl.multiple_of(j * tk, tk)
        kib = ki_ref[0, pl.ds(ks, tk), :]
        s = jnp.zeros((tk, tq), F32)
        for hh in range(IDX_HEADS):
            d = jnp.dot(kib, qiT_ref[0, hh], preferred_element_type=F32)
            s = s + jnp.maximum(d, 0.0) * w[hh:hh + 1, :]
        s = jnp.where(ks + row <= q_idx, s, -jnp.inf)
        key_ref[pl.ds(ks, tk), :] = _sortable(s)
        return 0

    lax.fori_loop(0, n_chunks * (COUNT_CHUNK // tk), score_block, 0)

    def count(pred, thr):
        def chunk(c, cnt):
            base = pl.multiple_of(c * COUNT_CHUNK, COUNT_CHUNK)
            ind = jnp.where(pred(key_ref[pl.ds(base, COUNT_CHUNK), :], thr), 1, 0)
            return cnt + jnp.sum(ind.reshape(COUNT_CHUNK // 8, 8, tq), axis=0)
        cnt = lax.fori_loop(0, n_chunks, chunk, jnp.zeros((8, tq), jnp.int32))
        return jnp.sum(cnt, axis=0, keepdims=True)

    def bit_step(i, t_u):
        cand_u = t_u | lax.shift_left(jnp.int32(1), 31 - i)
        cnt = count(lambda x, thr: x >= thr, cand_u ^ INT_MIN)
        return jnp.where(cnt >= topk, cand_u, t_u)

    thr = lax.fori_loop(0, 32, bit_step, jnp.zeros((1, tq), jnp.int32)) ^ INT_MIN
    n_ge = count(lambda x, t: x >= t, thr)
    thr_ref[...] = thr

    @pl.when(jnp.max(jnp.abs(n_ge - topk)) > 0)
    def _():
        need = (topk - count(lambda x, t: x > t, thr)).astype(F32)

        def select_block(j, seen):
            ks = pl.multiple_of(j * tk, tk)
            blk = key_ref[pl.ds(ks, tk), :]
            eq = jnp.where(blk == thr, 1.0, 0.0)
            rank = jnp.dot(lower_ref[...], eq.astype(MXU_DTYPE), preferred_element_type=F32) + seen
            tie = jnp.where(rank < need, eq, 0.0)
            sel = jnp.where(blk > thr, 1.0, tie)
            sel = jnp.where(ks + row <= q_idx, sel, 0.0)
            key_ref[pl.ds(ks, tk), :] = sel.astype(jnp.int32)
            return seen + jnp.sum(eq, axis=0, keepdims=True)

        lax.fori_loop(0, n_tiles, select_block, jnp.zeros((1, tq), F32))
        thr_ref[...] = jnp.ones((1, tq), jnp.int32)

    m_ref[...] = jnp.full(m_ref.shape, NEG, F32)
    acc_ref[...] = jnp.zeros(acc_ref.shape, F32)

    def attend(j, near):
        ks = pl.multiple_of(j * tk, tk)
        sel = key_ref[pl.ds(ks, tk), :] >= thr_ref[...]
        for n in range(C_KV_HEADS):
            kb = k_ref[0, n, pl.ds(ks, tk), :]
            vb = vT_ref[0, n, :, pl.ds(ks, tk)]
            for g in range(C_GROUP):
                h = n * C_GROUP + g
                lg = jnp.dot(kb, qT_ref[0, h], preferred_element_type=F32)
                if near is not None:
                    lg = lg + btile_ref[h, near]
                lg = jnp.where(sel, lg, NEG)
                m_old = m_ref[h]
                m_new = jnp.maximum(m_old, jnp.max(lg, axis=0, keepdims=True))
                p = jnp.exp2(lg - m_new)
                acc_ref[h] = jnp.exp2(m_old - m_new) * acc_ref[h] + jnp.dot(
                    vb, p.astype(MXU_DTYPE), preferred_element_type=F32)
                m_ref[h] = m_new

    def far_tile(j, _):
        attend(j, None)
        return 0

    first_near = n_tiles - REL_NEAR_TILES
    lax.fori_loop(0, jnp.maximum(first_near, 0), far_tile, 0)
    for c in range(REL_NEAR_TILES):
        if c == 0:
            pl.when(first_near >= 0)(functools.partial(attend, first_near, 0))
        else:
            attend(first_near + c, c)
    for h in range(C_HEADS):
        acc = acc_ref[h]
        o_ref[0, h] = acc[:C_HEAD_DIM] / acc[C_HEAD_DIM:C_HEAD_DIM + 1]


def _c_prompt(q, k, v, qi, ki, wi, btiles):
    b, t, _ = q.shape
    topk = min(TOPK_MAX, t // 4)
    assert t % COUNT_CHUNK == 0 and COUNT_CHUNK > topk and t % C_TQ == 0
    qT = ((q * (C_HEAD_DIM ** -0.5 * LOG2E)).reshape(b, t, C_HEADS, C_HEAD_DIM)
          .transpose(0, 2, 3, 1).astype(MXU_DTYPE))
    kh = k.reshape(b, t, C_KV_HEADS, C_HEAD_DIM).transpose(0, 2, 1, 3).astype(MXU_DTYPE)
    vT = v.reshape(b, t, C_KV_HEADS, C_HEAD_DIM).transpose(0, 2, 3, 1)
    ones_rows = jnp.zeros((b, C_KV_HEADS, V_ROWS - C_HEAD_DIM, t), F32).at[:, :, 0].set(1.0)
    vT = jnp.concatenate([vT, ones_rows], axis=2).astype(MXU_DTYPE)
    qi_hi, qi_lo = _split_hi_lo_mxu(
        (qi * IDX_DIM ** -0.5).reshape(b, t, IDX_HEADS, IDX_DIM).transpose(0, 2, 3, 1))
    qiT = jnp.concatenate([qi_hi, qi_lo, qi_hi], axis=2)
    ki_hi, ki_lo = _split_hi_lo_mxu(ki)
    ki3 = jnp.concatenate([ki_hi, ki_hi, ki_lo], axis=2)
    wiT = wi.transpose(0, 2, 1)
    tq, tk = C_TQ, ATT_TK
    oT = pl.pallas_call(
        functools.partial(_c_prompt_kernel, topk),
        out_shape=jax.ShapeDtypeStruct((b, C_HEADS, C_HEAD_DIM, t), F32),
        grid=(b, t // tq),
        in_specs=[
            pl.BlockSpec((1, IDX_HEADS, 3 * IDX_DIM, tq), lambda i, j: (i, 0, 0, j)),
            pl.BlockSpec((1, IDX_HEADS, tq), lambda i, j: (i, 0, j)),
            pl.BlockSpec((1, t, 3 * IDX_DIM), lambda i, j: (i, 0, 0)),
            pl.BlockSpec((1, C_HEADS, C_HEAD_DIM, tq), lambda i, j: (i, 0, 0, j)),
            pl.BlockSpec((1, C_KV_HEADS, t, C_HEAD_DIM), lambda i, j: (i, 0, 0, 0)),
            pl.BlockSpec((1, C_KV_HEADS, V_ROWS, t), lambda i, j: (i, 0, 0, 0)),
            pl.BlockSpec((C_HEADS, REL_NEAR_TILES, tk, tq), lambda i, j: (0, 0, 0, 0),
                         pipeline_mode=pl.Buffered(1)),
            pl.BlockSpec((tk, tk), lambda i, j: (0, 0)),
        ],
        out_specs=pl.BlockSpec((1, C_HEADS, C_HEAD_DIM, tq), lambda i, j: (i, 0, 0, j)),
        scratch_shapes=[
            pltpu.VMEM((t, tq), jnp.int32),
            pltpu.VMEM((1, tq), jnp.int32),
            pltpu.VMEM((C_HEADS, 1, tq), F32),
            pltpu.VMEM((C_HEADS, V_ROWS, tq), F32),
        ],
        compiler_params=pltpu.CompilerParams(
            dimension_semantics=("parallel", "arbitrary"), vmem_limit_bytes=VMEM_LIMIT_BYTES),
        name="c_prompt",
    )(qiT, wiT, ki3, qT, kh, vT, btiles, _strict_lower(tk))
    return oT.transpose(0, 3, 1, 2).reshape(b, t, C_WIDTH)


_NT = (((1,), (1,)), ((), ()))
_NN = (((1,), (0,)), ((), ()))


def _pad_rows(x, rows):
    return jnp.concatenate([x, jnp.zeros((rows - x.shape[0], x.shape[1]), x.dtype)], axis=0)


def _prefix_and_total(n, strict_before):
    i = np.arange(n)
    tri = (i[:, None] < i[None, :]) if strict_before else (i[:, None] > i[None, :])
    return jnp.asarray(np.concatenate([tri, np.ones((n, n), bool)], axis=1).astype(np.float32), MXU_DTYPE)


def _sb_sample_kernel(n_pages, pt_ref, q_ref, kn_ref, vn_ref, hmask_ref, sufx_ref, *rest):
    k_pages, v_pages = rest[:n_pages], rest[n_pages:2 * n_pages]
    o_ref, carry_ref, acc_ref = rest[2 * n_pages:]
    t = q_ref.shape[1]
    rows, tk = SB_HEADS * t, PAGE_SIZE
    hmask = hmask_ref[...]
    q = q_ref[0] * (SB_HEAD_DIM ** -0.5)
    qbd = (jnp.concatenate([q] * SB_HEADS, axis=0) * hmask).astype(MXU_DTYPE)
    carry_ref[...] = jnp.zeros(carry_ref.shape, F32)
    acc_ref[...] = jnp.zeros(acc_ref.shape, F32)

    def step(z, weighted_values, valid):
        sp = jnp.maximum(z, 0.0) + jnp.log(1.0 + jnp.exp(-jnp.abs(z)))
        lr = -sp if valid is None else jnp.where(valid, -sp, 0.0)
        hi, lo = _split_hi_lo_mxu(lr)
        both = (jnp.dot(hi, sufx_ref[...], preferred_element_type=F32)
                + jnp.dot(lo, sufx_ref[...], preferred_element_type=F32))
        a = jnp.exp(z - sp + both[:, :tk] + carry_ref[...])
        if valid is not None:
            a = jnp.where(valid, a, 0.0)
        acc_ref[...] += weighted_values(a.astype(MXU_DTYPE))
        carry_ref[...] += both[:, tk:]

    row_t = lax.broadcasted_iota(jnp.int32, (rows, tk), 0) & (t - 1)
    lane = lax.broadcasted_iota(jnp.int32, (rows, tk), 1)
    kn = _pad_rows(kn_ref[0], tk).astype(MXU_DTYPE)
    vn = _pad_rows(vn_ref[0], tk).astype(MXU_DTYPE)
    step(lax.dot_general(qbd, kn, _NT, preferred_element_type=F32),
         lambda a: jnp.dot(a, vn, preferred_element_type=F32), lane < row_t)
    for p in reversed(range(n_pages)):
        @pl.when(jnp.max(carry_ref[...]) > SB_EXIT)
        def _(p=p):
            kT = k_pages[p][...].reshape(SB_WIDTH, tk).astype(MXU_DTYPE)
            vT = v_pages[p][...].reshape(SB_WIDTH, tk).astype(MXU_DTYPE)
            step(jnp.dot(qbd, kT, preferred_element_type=F32),
                 lambda a: lax.dot_general(a, vT, _NT, preferred_element_type=F32), None)

    acc = acc_ref[...] * hmask
    y = acc[0:t]
    for h in range(1, SB_HEADS):
        y = y + acc[h * t:(h + 1) * t]
    o_ref[0] = y


def _feature_major(cache):
    nd = cache.ndim
    return cache.transpose((0, 1) + tuple(range(3, nd)) + (2,))


def _page_specs(n_pages, layer, page_shape):
    zeros = (0,) * len(page_shape)
    return [pl.BlockSpec((None, None) + page_shape, lambda i, pt, p=p: (layer, pt[i, p]) + zeros)
            for p in range(n_pages)]


def _sb_sample(q, k, v, cache_k, cache_v, layer, page_table):
    b, t, _ = q.shape
    assert t & (t - 1) == 0 and t <= PAGE_SIZE
    n_pages = page_table.shape[1]
    rows = SB_HEADS * t
    page = (SB_HEADS, SB_HEAD_DIM, PAGE_SIZE)
    ck, cv = _feature_major(cache_k), _feature_major(cache_v)
    hmask = jnp.asarray((np.arange(rows)[:, None] // t == np.arange(SB_WIDTH)[None, :] // SB_HEAD_DIM)
                        .astype(np.float32))
    tok = pl.BlockSpec((1, t, SB_WIDTH), lambda i, pt: (i, 0, 0))
    const = lambda i, pt: (0, 0)
    return pl.pallas_call(
        functools.partial(_sb_sample_kernel, n_pages),
        out_shape=jax.ShapeDtypeStruct((b, t, SB_WIDTH), F32),
        grid_spec=pltpu.PrefetchScalarGridSpec(
            num_scalar_prefetch=1,
            grid=(b,),
            in_specs=[tok, tok, tok,
                      pl.BlockSpec((rows, SB_WIDTH), const),
                      pl.BlockSpec((PAGE_SIZE, 2 * PAGE_SIZE), const)]
            + _page_specs(n_pages, layer, page) + _page_specs(n_pages, layer, page),
            out_specs=tok,
            scratch_shapes=[pltpu.VMEM((rows, PAGE_SIZE), F32), pltpu.VMEM((rows, SB_WIDTH), F32)],
        ),
        compiler_params=pltpu.CompilerParams(
            dimension_semantics=("arbitrary",), vmem_limit_bytes=VMEM_LIMIT_BYTES),
        name="sb_sample",
    )(page_table, q, k, v, hmask, _prefix_and_total(PAGE_SIZE, False),
      *([ck] * n_pages), *([cv] * n_pages))


def _c_sample_kernel(n_pages, topk, pt_ref, qi_ref, w_ref, q_ref, kin_ref, kn_ref, vn_ref,
                     sbias_ref, prex_ref, *rest):
    idx_pages, k_pages, v_pages = rest[:n_pages], rest[n_pages:2 * n_pages], rest[2 * n_pages:3 * n_pages]
    o_ref, sc_ref, lg_ref = rest[3 * n_pages:]
    t = kin_ref.shape[1]
    tk = PAGE_SIZE
    nb = n_pages + 1
    ih = IDX_HEADS * t
    rows = C_HEADS * t
    row8 = lax.broadcasted_iota(jnp.int32, (t, tk), 0)
    lane8 = lax.broadcasted_iota(jnp.int32, (t, tk), 1)
    new_valid = lane8 <= row8
    qi_cat = qi_ref[0]
    w = w_ref[0] * (IDX_HEADS ** -0.5)

    def scores(idx_blk, valid):
        dims = _NT if valid is not None else _NN
        hi, lo = _split_hi_lo_mxu(idx_blk)
        s = lax.dot_general(qi_cat, hi, dims, preferred_element_type=F32)
        s = s[:ih] + s[ih:] + lax.dot_general(qi_cat[:ih], lo, dims, preferred_element_type=F32)
        r = jnp.maximum(s, 0.0) * w
        sc = r[0:t]
        for hh in range(1, IDX_HEADS):
            sc = sc + r[hh * t:(hh + 1) * t]
        if valid is not None:
            sc = jnp.where(valid, sc, -jnp.inf)
        return _sortable(sc)

    for p in range(n_pages):
        sc_ref[:, p * tk:(p + 1) * tk] = scores(idx_pages[p][...], None)
    sc_ref[:, n_pages * tk:] = scores(_pad_rows(kin_ref[0], tk), new_valid)

    def count(pred, thr):
        return jnp.sum(jnp.where(pred(sc_ref[...], thr), 1.0, 0.0), axis=1, keepdims=True)

    def bit_step(i, t_u):
        cand_u = t_u | lax.shift_left(jnp.int32(1), 31 - i)
        cnt = count(lambda x, thr: x >= thr, cand_u ^ INT_MIN)
        return jnp.where(cnt >= topk, cand_u, t_u)

    thr = lax.fori_loop(0, 32, bit_step, jnp.zeros((t, 1), jnp.int32)) ^ INT_MIN
    need = topk - count(lambda x, th: x > th, thr)

    seen = jnp.zeros((t, tk), F32)
    for blk in range(nb):
        x = sc_ref[:, blk * tk:(blk + 1) * tk]
        eq = jnp.where(x == thr, 1.0, 0.0)
        both = jnp.dot(eq.astype(MXU_DTYPE), prex_ref[...], preferred_element_type=F32)
        tie = jnp.where(both[:, :tk] + seen < need, eq, 0.0)
        sel = jnp.where(x > thr, 1.0, tie)
        if blk == n_pages:
            sel = jnp.where(new_valid, sel, 0.0)
        sc_ref[:, blk * tk:(blk + 1) * tk] = sel.astype(jnp.int32)
        seen = seen + both[:, tk:]

    q_rows = q_ref[0]
    grp = C_GROUP * t
    hd = C_HEAD_DIM

    def kv_block(pages, new_ref, blk, n):
        if blk < n_pages:
            return pages[blk][n].astype(MXU_DTYPE)
        return _pad_rows(new_ref[0][:, n * hd:(n + 1) * hd], tk).astype(MXU_DTYPE)

    m = jnp.full((rows, tk), NEG, F32)
    for blk in range(nb):
        bias = sbias_ref[2] if blk < n_pages - 1 else sbias_ref[blk - (n_pages - 1)]
        lg = jnp.concatenate(
            [lax.dot_general(q_rows[n * grp:(n + 1) * grp], kv_block(k_pages, kn_ref, blk, n),
                             _NN if blk < n_pages else _NT, preferred_element_type=F32)
             for n in range(C_KV_HEADS)], axis=0) + bias
        sel = jnp.concatenate([sc_ref[:, blk * tk:(blk + 1) * tk]] * C_HEADS, axis=0) != 0
        lg = jnp.where(sel, lg, NEG)
        lg_ref[:, blk * tk:(blk + 1) * tk] = lg
        m = jnp.maximum(m, lg)
    m_row = jnp.max(m, axis=1, keepdims=True)
    lsum = jnp.zeros((rows, tk), F32)
    accs = [jnp.zeros((grp, hd), F32) for _ in range(C_KV_HEADS)]
    for blk in range(nb):
        p = jnp.exp(lg_ref[:, blk * tk:(blk + 1) * tk] - m_row)
        lsum = lsum + p
        p = p.astype(MXU_DTYPE)
        accs = [acc + lax.dot_general(p[n * grp:(n + 1) * grp], kv_block(v_pages, vn_ref, blk, n),
                                      _NT if blk < n_pages else _NN, preferred_element_type=F32)
                for n, acc in enumerate(accs)]
    l_row = jnp.sum(lsum, axis=1, keepdims=True)
    for n, acc in enumerate(accs):
        o = acc / l_row[n * grp:(n + 1) * grp]
        for g in range(C_GROUP):
            o_ref[0, n * C_GROUP + g] = o[g * t:(g + 1) * t]


def _c_sample(q, k, v, qi, ki, wi, cache_k, cache_v, cache_idx, layer, page_table, sbias):
    b, t, _ = q.shape
    n_pages = page_table.shape[1]
    topk = min(TOPK_MAX, (n_pages * PAGE_SIZE + t) // 4)
    rows = C_HEADS * t
    ih = IDX_HEADS * t
    assert rows == ATT_TILE and t <= PAGE_SIZE
    q_rows = ((q * C_HEAD_DIM ** -0.5).reshape(b, t, C_HEADS, C_HEAD_DIM).transpose(0, 2, 1, 3)
              .reshape(b, rows, C_HEAD_DIM).astype(MXU_DTYPE))
    qi_rows = ((qi * IDX_DIM ** -0.5).reshape(b, t, IDX_HEADS, IDX_DIM).transpose(0, 2, 1, 3)
               .reshape(b, ih, IDX_DIM))
    qi_cat = jnp.concatenate(_split_hi_lo_mxu(qi_rows), axis=1)
    w_col = wi.transpose(0, 2, 1).reshape(b, ih, 1)
    kv_page = (C_KV_HEADS, C_HEAD_DIM, PAGE_SIZE)
    per_seq = lambda r, c: pl.BlockSpec((1, r, c), lambda i, pt: (i, 0, 0))
    const2 = lambda i, pt: (0, 0)
    o = pl.pallas_call(
        functools.partial(_c_sample_kernel, n_pages, topk),
        out_shape=jax.ShapeDtypeStruct((b, C_HEADS, t, C_HEAD_DIM), F32),
        grid_spec=pltpu.PrefetchScalarGridSpec(
            num_scalar_prefetch=1,
            grid=(b,),
            in_specs=[per_seq(2 * ih, IDX_DIM), per_seq(ih, 1), per_seq(rows, C_HEAD_DIM),
                      per_seq(t, IDX_DIM), per_seq(t, C_KV_WIDTH), per_seq(t, C_KV_WIDTH),
                      pl.BlockSpec((3, ATT_TILE, ATT_TILE), lambda i, pt: (0, 0, 0)),
                      pl.BlockSpec((PAGE_SIZE, 2 * PAGE_SIZE), const2)]
            + _page_specs(n_pages, layer, (IDX_DIM, PAGE_SIZE)) + _page_specs(n_pages, layer, kv_page)
            + _page_specs(n_pages, layer, kv_page),
            out_specs=pl.BlockSpec((1, C_HEADS, t, C_HEAD_DIM), lambda i, pt: (i, 0, 0, 0)),
            scratch_shapes=[pltpu.VMEM((t, (n_pages + 1) * PAGE_SIZE), jnp.int32),
                            pltpu.VMEM((rows, (n_pages + 1) * PAGE_SIZE), F32)],
        ),
        compiler_params=pltpu.CompilerParams(
            dimension_semantics=("arbitrary",), vmem_limit_bytes=VMEM_LIMIT_BYTES),
        name="c_sample",
    )(page_table, qi_cat, w_col, q_rows, ki, k, v, sbias, _prefix_and_total(PAGE_SIZE, True),
      *([_feature_major(cache_idx)] * n_pages), *([_feature_major(cache_k)] * n_pages),
      *([_feature_major(cache_v)] * n_pages))
    return o.transpose(0, 2, 1, 3).reshape(b, t, C_WIDTH)


def _pool_mix(u_ext, pos, w_pool, pool_scale):
    b, l, c = u_ext.shape
    t = l - POOL_BUF
    cs = jnp.concatenate([jnp.zeros((b, 1, c), F32), jnp.cumsum(u_ext, axis=1)], axis=1)
    u_new = u_ext[:, POOL_BUF:]
    hi = cs[:, POOL_BUF + 1:]
    outs = []
    for g, w in enumerate(POOL_WINDOWS):
        sl = slice(g * POOL_GROUP, (g + 1) * POOL_GROUP)
        lo = cs[:, POOL_BUF + 1 - w:POOL_BUF + 1 - w + t, sl]
        cnt = jnp.minimum(pos + 1, w).astype(F32)[None, :, None]
        outs.append((hi[..., sl] - lo) / cnt - u_new[..., sl])
    d = jnp.stack(outs, axis=2)
    y = jnp.einsum('btgc,gcd->btgd', d, w_pool).reshape(b, t, POOL_WIDTH)
    return y * pool_scale


def _ab_prompt_mix(u, q, k, v, w_pool, pool_scale):
    b, t, _ = u.shape
    pos = jnp.arange(t, dtype=jnp.int32)
    u_ext = jnp.concatenate([jnp.zeros((b, POOL_BUF, POOL_WIDTH), F32), u], axis=1)
    y_pool = _pool_mix(u_ext, pos, w_pool, pool_scale)
    y_sb = _sb_prompt(q, k, v)
    return jnp.concatenate([y_pool, y_sb], axis=-1), u_ext[:, -POOL_BUF:]


def _ab_sample_mix(u, q, k, v, pool_buf, cache_k, cache_v, layer, page_table, w_pool, pool_scale):
    t = u.shape[1]
    q_pos = page_table.shape[1] * PAGE_SIZE + jnp.arange(t, dtype=jnp.int32)
    u_ext = jnp.concatenate([pool_buf, u], axis=1)
    y_pool = _pool_mix(u_ext, q_pos, w_pool, pool_scale)
    y_sb = _sb_sample(q, k, v, cache_k, cache_v, layer, page_table)
    return jnp.concatenate([y_pool, y_sb], axis=-1), u_ext[:, -POOL_BUF:]


def _segment_mean_matrix(width):
    seg = np.arange(width) // C_HEAD_DIM
    return jnp.asarray((seg[:, None] == seg[None, :]).astype(np.float32) / C_HEAD_DIM, BF16)


def kernel(x_prompt, x_sample, state_pool, cache_b_k, cache_b_v, cache_c_k, cache_c_v, cache_c_idx, page_table,
           g_ffn, w_ffn_gate, w_ffn_up, w_ffn_down, g_mix, w_in_ab, w_pool, pool_scale, w_out_ab,
           w_in_c, g_q, g_k, w_out_c, rel_bias):
    bp, tp, _ = x_prompt.shape
    bs, ts, _ = x_sample.shape
    mp = bp * tp
    x = jnp.concatenate([x_prompt.reshape(mp, D_MODEL), x_sample.reshape(bs * ts, D_MODEL)], axis=0)

    wg = w_ffn_gate.astype(BF16)
    wu = w_ffn_up.astype(BF16)
    wd = w_ffn_down.astype(BF16)
    w_ab = w_in_ab.astype(BF16)
    w_oab = w_out_ab.astype(BF16)
    w_c = jnp.pad(w_in_c, ((0, 0), (0, 0), (0, C_IN_PAD - C_IN))).astype(BF16)
    w_oc = w_out_c.astype(BF16)
    segq = _segment_mean_matrix(C_WIDTH)
    segk = _segment_mean_matrix(C_KV_WIDTH)
    btiles, sbias = _bias_tiles(rel_bias, ts)

    def split(a):
        return a[:mp].reshape(bp, tp, -1), a[mp:].reshape(bs, ts, -1)

    pool_p, pool_s, kbp, vbp, kbs, vbs = [], [], [], [], [], []
    kcp, vcp, icp, kcs, vcs, ics = [], [], [], [], [], []
    for layer in range(DEPTH):
        j = layer // 2
        x = _ffn(x, g_ffn[layer, 0][None], wg[layer, 0], wu[layer, 0], wd[layer, 0])
        g = g_mix[layer][None]
        if layer % 2 == 0:
            u, q, k, v = _proj_ab(x, g, w_ab[j])
            (u_p, u_s), (q_p, q_s), (k_p, k_s), (v_p, v_s) = split(u), split(q), split(k), split(v)
            y_p, sp = _ab_prompt_mix(u_p, q_p, k_p, v_p, w_pool[j], pool_scale[j])
            y_s, ss = _ab_sample_mix(u_s, q_s, k_s, v_s, state_pool[j], cache_b_k, cache_b_v, j,
                                     page_table, w_pool[j], pool_scale[j])
            pool_p.append(sp); pool_s.append(ss)
            kbp.append(k_p.reshape(bp, tp, SB_HEADS, SB_HEAD_DIM))
            vbp.append(v_p.reshape(bp, tp, SB_HEADS, SB_HEAD_DIM))
            kbs.append(k_s.reshape(bs, ts, SB_HEADS, SB_HEAD_DIM))
            vbs.append(v_s.reshape(bs, ts, SB_HEADS, SB_HEAD_DIM))
            w_out = w_oab[j]
        else:
            gq = jnp.tile(g_q[j], C_HEADS)[None]
            gk = jnp.tile(g_k[j], C_KV_HEADS)[None]
            q, k, v, qi, tail = _proj_c(x, g, w_c[j], gq, gk, segq, segk)
            ki = tail[:, :IDX_DIM]
            wi = tail[:, IDX_DIM:IDX_DIM + IDX_HEADS]
            (q_p, q_s), (k_p, k_s), (v_p, v_s) = split(q), split(k), split(v)
            (qi_p, qi_s), (ki_p, ki_s), (wi_p, wi_s) = split(qi), split(ki), split(wi)
            y_p = _c_prompt(q_p, k_p, v_p, qi_p, ki_p, wi_p, btiles)
            y_s = _c_sample(q_s, k_s, v_s, qi_s, ki_s, wi_s, cache_c_k, cache_c_v, cache_c_idx,
                            j, page_table, sbias)
            kcp.append(k_p.reshape(bp, tp, C_KV_HEADS, C_HEAD_DIM))
            vcp.append(v_p.reshape(bp, tp, C_KV_HEADS, C_HEAD_DIM))
            icp.append(ki_p)
            kcs.append(k_s.reshape(bs, ts, C_KV_HEADS, C_HEAD_DIM))
            vcs.append(v_s.reshape(bs, ts, C_KV_HEADS, C_HEAD_DIM))
            ics.append(ki_s)
            w_out = w_oc[j]
        y = jnp.concatenate([y_p.reshape(mp, D_MODEL), y_s.reshape(bs * ts, D_MODEL)], axis=0)
        x = _out_proj(x, y, w_out)
        x = _ffn(x, g_ffn[layer, 1][None], wg[layer, 1], wu[layer, 1], wd[layer, 1])

    return (x[:mp].reshape(bp, tp, D_MODEL), x[mp:].reshape(bs, ts, D_MODEL),
            jnp.stack(pool_p), jnp.stack(pool_s),
            jnp.stack(kbp), jnp.stack(vbp), jnp.stack(kbs), jnp.stack(vbs),
            jnp.stack(kcp), jnp.stack(vcp), jnp.stack(icp),
            jnp.stack(kcs), jnp.stack(vcs), jnp.stack(ics))
```

```python
import functools
import math

import jax
import jax.numpy as jnp
import numpy as np
from jax import lax
from jax.experimental import pallas as pl
from jax.experimental.pallas import tpu as pltpu

F32 = jnp.float32
BF16 = jnp.bfloat16

D_MODEL = 1024
DEPTH = 4
D_FF = 2816
POOL_WINDOWS = (2, 4, 8, 16)
POOL_WIDTH = 512
POOL_GROUP = 128
POOL_BUF = 15
SB_HEADS = 8
SB_HEAD_DIM = 64
SB_WIDTH = 512
AB_IN = POOL_WIDTH + 3 * SB_WIDTH
C_HEAD_DIM = 64
C_HEADS = 16
C_KV_HEADS = 4
C_GROUP = 4
C_WIDTH = 1024
C_KV_WIDTH = 256
IDX_HEADS = 8
IDX_DIM = 64
TOPK_MAX = 256
C_IN = 2120
C_IN_PAD = 2176
REL_BUCKETS = 32
REL_MAX_EXACT = 16
REL_MAX_DIST = 128
Q_BLOCK = 128
PAGE_SIZE = 128
EPS = 1e-6
NEG = -1e30

VMEM_LIMIT_BYTES = 56 * 1024 * 1024
FF_CHUNK = 256
TOKEN_TILE = 512


def _rms(x, g):
    ms = jnp.mean(x * x, axis=-1, keepdims=True)
    return x * lax.rsqrt(ms + EPS) * g


def _ffn_kernel(x_ref, g_ref, wg_ref, wu_ref, wd_ref, o_ref):
    x = x_ref[...]
    h = _rms(x, g_ref[...]).astype(BF16)
    acc = x
    for c in range(D_FF // FF_CHUNK):
        sl = slice(c * FF_CHUNK, (c + 1) * FF_CHUNK)
        gate = jnp.dot(h, wg_ref[:, sl], preferred_element_type=F32)
        up = jnp.dot(h, wu_ref[:, sl], preferred_element_type=F32)
        act = (0.5 * gate * jax.nn.sigmoid(gate) * up).astype(BF16)
        acc = acc + jnp.dot(act, wd_ref[sl, :], preferred_element_type=F32)
    o_ref[...] = acc


def _ffn(x, g, wg, wu, wd):
    m = x.shape[0]
    resident = dict(pipeline_mode=pl.Buffered(1))
    return pl.pallas_call(
        _ffn_kernel,
        out_shape=jax.ShapeDtypeStruct((m, D_MODEL), F32),
        grid=(m // TOKEN_TILE,),
        in_specs=[
            pl.BlockSpec((TOKEN_TILE, D_MODEL), lambda i: (i, 0)),
            pl.BlockSpec((1, D_MODEL), lambda i: (0, 0)),
            pl.BlockSpec((D_MODEL, D_FF), lambda i: (0, 0), **resident),
            pl.BlockSpec((D_MODEL, D_FF), lambda i: (0, 0), **resident),
            pl.BlockSpec((D_FF, D_MODEL), lambda i: (0, 0), **resident),
        ],
        out_specs=pl.BlockSpec((TOKEN_TILE, D_MODEL), lambda i: (i, 0)),
        compiler_params=pltpu.CompilerParams(
            dimension_semantics=("parallel",), vmem_limit_bytes=VMEM_LIMIT_BYTES),
        name="ffn_half",
    )(x, g, wg, wu, wd)


def _proj_ab_kernel(x_ref, g_ref, w_ref, u_ref, q_ref, k_ref, v_ref):
    h = _rms(x_ref[...], g_ref[...]).astype(BF16)
    p = jnp.dot(h, w_ref[...], preferred_element_type=F32)
    u_ref[...] = p[:, :POOL_WIDTH]
    q_ref[...] = p[:, POOL_WIDTH:POOL_WIDTH + SB_WIDTH]
    k_ref[...] = p[:, POOL_WIDTH + SB_WIDTH:POOL_WIDTH + 2 * SB_WIDTH]
    v_ref[...] = p[:, POOL_WIDTH + 2 * SB_WIDTH:]


def _proj_ab(x, g, w):
    m = x.shape[0]
    out = jax.ShapeDtypeStruct((m, SB_WIDTH), F32)
    spec = pl.BlockSpec((TOKEN_TILE, SB_WIDTH), lambda i: (i, 0))
    return pl.pallas_call(
        _proj_ab_kernel,
        out_shape=(out, out, out, out),
        grid=(m // TOKEN_TILE,),
        in_specs=[
            pl.BlockSpec((TOKEN_TILE, D_MODEL), lambda i: (i, 0)),
            pl.BlockSpec((1, D_MODEL), lambda i: (0, 0)),
            pl.BlockSpec((D_MODEL, AB_IN), lambda i: (0, 0), pipeline_mode=pl.Buffered(1)),
        ],
        out_specs=(spec, spec, spec, spec),
        compiler_params=pltpu.CompilerParams(
            dimension_semantics=("parallel",), vmem_limit_bytes=VMEM_LIMIT_BYTES),
        name="proj_ab",
    )(x, g, w)


def _split_hi_lo(x):
    hi = x.astype(BF16)
    lo = (x - hi.astype(F32)).astype(BF16)
    return hi, lo


def _head_rms(x, seg_mean_ref, gain):
    hi, lo = _split_hi_lo(x * x)
    ms = (jnp.dot(hi, seg_mean_ref[...], preferred_element_type=F32)
          + jnp.dot(lo, seg_mean_ref[...], preferred_element_type=F32))
    return x * lax.rsqrt(ms + EPS) * gain


def _proj_c_kernel(x_ref, g_ref, w_ref, gq_ref, gk_ref, segq_ref, segk_ref,
                   q_ref, k_ref, v_ref, qi_ref, tail_ref):
    h = _rms(x_ref[...], g_ref[...]).astype(BF16)
    p = jnp.dot(h, w_ref[...], preferred_element_type=F32)
    o_k = C_WIDTH
    o_v = o_k + C_KV_WIDTH
    o_qi = o_v + C_KV_WIDTH
    o_tail = o_qi + IDX_HEADS * IDX_DIM
    q_ref[...] = _head_rms(p[:, :o_k], segq_ref, gq_ref[...])
    k_ref[...] = _head_rms(p[:, o_k:o_v], segk_ref, gk_ref[...])
    v_ref[...] = p[:, o_v:o_qi]
    qi_ref[...] = p[:, o_qi:o_tail]
    tail_ref[...] = p[:, o_tail:]


def _proj_c(x, g, w, gq, gk, segq, segk):
    m = x.shape[0]
    tail = C_IN_PAD - (C_WIDTH + 2 * C_KV_WIDTH + IDX_HEADS * IDX_DIM)
    widths = (C_WIDTH, C_KV_WIDTH, C_KV_WIDTH, IDX_HEADS * IDX_DIM, tail)
    const = lambda i: (0, 0)
    return pl.pallas_call(
        _proj_c_kernel,
        out_shape=tuple(jax.ShapeDtypeStruct((m, n), F32) for n in widths),
        grid=(m // TOKEN_TILE,),
        in_specs=[
            pl.BlockSpec((TOKEN_TILE, D_MODEL), lambda i: (i, 0)),
            pl.BlockSpec((1, D_MODEL), const),
            pl.BlockSpec((D_MODEL, C_IN_PAD), const, pipeline_mode=pl.Buffered(1)),
            pl.BlockSpec((1, C_WIDTH), const),
            pl.BlockSpec((1, C_KV_WIDTH), const),
            pl.BlockSpec((C_WIDTH, C_WIDTH), const, pipeline_mode=pl.Buffered(1)),
            pl.BlockSpec((C_KV_WIDTH, C_KV_WIDTH), const, pipeline_mode=pl.Buffered(1)),
        ],
        out_specs=tuple(pl.BlockSpec((TOKEN_TILE, n), lambda i: (i, 0)) for n in widths),
        compiler_params=pltpu.CompilerParams(
            dimension_semantics=("parallel",), vmem_limit_bytes=VMEM_LIMIT_BYTES),
        name="proj_c",
    )(x, g, w, gq, gk, segq, segk)


def _out_proj_kernel(x_ref, y_ref, w_ref, o_ref):
    o_ref[...] = x_ref[...] + jnp.dot(y_ref[...].astype(BF16), w_ref[...],
                                      preferred_element_type=F32)


def _out_proj(x, y, w):
    m = x.shape[0]
    tile = pl.BlockSpec((TOKEN_TILE, D_MODEL), lambda i: (i, 0))
    return pl.pallas_call(
        _out_proj_kernel,
        out_shape=jax.ShapeDtypeStruct((m, D_MODEL), F32),
        grid=(m // TOKEN_TILE,),
        in_specs=[tile, tile,
                  pl.BlockSpec((D_MODEL, D_MODEL), lambda i: (0, 0), pipeline_mode=pl.Buffered(1))],
        out_specs=tile,
        compiler_params=pltpu.CompilerParams(
            dimension_semantics=("parallel",), vmem_limit_bytes=VMEM_LIMIT_BYTES),
        name="out_proj",
    )(x, y, w)


ATT_TILE = 128
ATT_TK = 128
ATT_TQ = 256
SB_LOCKSTEP = 2
SB_EXIT = -104.0
MXU_DTYPE = BF16


def _split_hi_lo_mxu(x):
    hi = x.astype(MXU_DTYPE)
    lo = (x - hi.astype(F32)).astype(MXU_DTYPE)
    return hi, lo


def _sb_prompt_kernel(qT_ref, k_ref, vT_ref, upper_ref, o_ref):
    tk, tq = ATT_TK, ATT_TQ
    qb = pl.program_id(1)
    row = lax.broadcasted_iota(jnp.int32, (tk, tq), 0)
    lane = lax.broadcasted_iota(jnp.int32, (tk, tq), 1)
    q_idx = qb * tq + lane
    upper = upper_ref[...]
    last_tile = qb * (tq // tk) + (tq // tk - 1)

    def head_step(h, ks, valid, carry, acc):
        kb = k_ref[0, h, pl.ds(ks, tk), :]
        z = jnp.dot(kb, qT_ref[0, h], preferred_element_type=F32)
        sp = jnp.maximum(z, 0.0) + jnp.log(1.0 + jnp.exp(-jnp.abs(z)))
        lr = jnp.where(valid, -sp, 0.0)
        hi, lo = _split_hi_lo_mxu(lr)
        between = (jnp.dot(upper, hi, preferred_element_type=F32)
                   + jnp.dot(upper, lo, preferred_element_type=F32) + carry)
        a = jnp.where(valid, jnp.exp(z - sp + between), 0.0)
        vb = vT_ref[0, h, :, pl.ds(ks, tk)]
        acc = acc + jnp.dot(vb, a.astype(MXU_DTYPE), preferred_element_type=F32)
        return carry + jnp.sum(lr, axis=0, keepdims=True), acc

    for h0 in range(0, SB_HEADS, SB_LOCKSTEP):
        heads = range(h0, h0 + SB_LOCKSTEP)

        def body(state, heads=heads):
            j, _, carries, accs = state
            ks = pl.multiple_of(j * tk, tk)
            valid = (ks + row) < q_idx
            out = [head_step(h, ks, valid, c, a) for h, c, a in zip(heads, carries, accs)]
            carries = tuple(o[0] for o in out)
            top = functools.reduce(jnp.maximum, [jnp.max(c) for c in carries])
            return j - 1, top, carries, tuple(o[1] for o in out)

        def cond(state):
            j, top, _, _ = state
            return jnp.logical_and(j >= 0, top > SB_EXIT)

        init = (last_tile, jnp.float32(0.0),
                tuple(jnp.zeros((1, tq), F32) for _ in heads),
                tuple(jnp.zeros((SB_HEAD_DIM, tq), F32) for _ in heads))
        _, _, _, accs = lax.while_loop(cond, body, init)
        for h, acc in zip(heads, accs):
            o_ref[0, h] = acc


def _strict_upper(n):
    i = np.arange(n)
    return jnp.asarray((i[None, :] > i[:, None]).astype(np.float32), MXU_DTYPE)


def _strict_lower(n):
    i = np.arange(n)
    return jnp.asarray((i[None, :] < i[:, None]).astype(np.float32), MXU_DTYPE)


def _sb_prompt(q, k, v):
    b, t, _ = q.shape
    scale = SB_HEAD_DIM ** -0.5
    qT = (q * scale).reshape(b, t, SB_HEADS, SB_HEAD_DIM).transpose(0, 2, 3, 1).astype(MXU_DTYPE)
    kh = k.reshape(b, t, SB_HEADS, SB_HEAD_DIM).transpose(0, 2, 1, 3).astype(MXU_DTYPE)
    vT = v.reshape(b, t, SB_HEADS, SB_HEAD_DIM).transpose(0, 2, 3, 1).astype(MXU_DTYPE)
    oT = pl.pallas_call(
        _sb_prompt_kernel,
        out_shape=jax.ShapeDtypeStruct((b, SB_HEADS, SB_HEAD_DIM, t), F32),
        grid=(b, t // ATT_TQ),
        in_specs=[
            pl.BlockSpec((1, SB_HEADS, SB_HEAD_DIM, ATT_TQ), lambda i, j: (i, 0, 0, j)),
            pl.BlockSpec((1, SB_HEADS, t, SB_HEAD_DIM), lambda i, j: (i, 0, 0, 0)),
            pl.BlockSpec((1, SB_HEADS, SB_HEAD_DIM, t), lambda i, j: (i, 0, 0, 0)),
            pl.BlockSpec((ATT_TK, ATT_TK), lambda i, j: (0, 0)),
        ],
        out_specs=pl.BlockSpec((1, SB_HEADS, SB_HEAD_DIM, ATT_TQ), lambda i, j: (i, 0, 0, j)),
        compiler_params=pltpu.CompilerParams(
            dimension_semantics=("parallel", "arbitrary"), vmem_limit_bytes=VMEM_LIMIT_BYTES),
        name="sb_prompt",
    )(qT, kh, vT, _strict_upper(ATT_TK))
    return oT.transpose(0, 3, 1, 2).reshape(b, t, SB_WIDTH)


INT_MIN = -2 ** 31
COUNT_CHUNK = 4 * ATT_TK
LOG2E = math.log2(math.e)
C_TQ = 256
REL_NEAR_TILES = C_TQ // ATT_TK + 1
REL_LAST_BUCKET_FROM = math.ceil(REL_MAX_EXACT * (REL_MAX_DIST / REL_MAX_EXACT) ** (
    (REL_BUCKETS - 1 - REL_MAX_EXACT) / (REL_BUCKETS - REL_MAX_EXACT)))
assert ATT_TK + 1 >= REL_LAST_BUCKET_FROM
SUM_ROWS = 16


def _bucket_of(dist):
    n = jnp.maximum(dist, 0)
    nf = jnp.maximum(n, 1).astype(F32)
    large = REL_MAX_EXACT + (jnp.log(nf / REL_MAX_EXACT) / math.log(REL_MAX_DIST / REL_MAX_EXACT)
                             * (REL_BUCKETS - REL_MAX_EXACT)).astype(jnp.int32)
    return jnp.where(n < REL_MAX_EXACT, n, jnp.minimum(large, REL_BUCKETS - 1))


def _bias_tile_kernel(dec_seq, relb_ref, o_ref, s_ref):
    krow = lax.broadcasted_iota(jnp.int32, (ATT_TK, C_TQ), 0)
    qlane = lax.broadcasted_iota(jnp.int32, (ATT_TK, C_TQ), 1)
    for c in range(REL_NEAR_TILES):
        bucket = _bucket_of((1 - c) * ATT_TK + qlane - krow)
        for h in range(C_HEADS):
            tile = jnp.zeros((ATT_TK, C_TQ), F32)
            for b in range(REL_BUCKETS):
                tile = jnp.where(bucket == b, relb_ref[b, h], tile)
            o_ref[h, c] = (tile - relb_ref[REL_BUCKETS - 1, h]) * LOG2E
    t = ATT_TILE
    row = lax.broadcasted_iota(jnp.int32, (t, t), 0)
    lane = lax.broadcasted_iota(jnp.int32, (t, t), 1)
    row_head = row >> int(math.log2(dec_seq))
    row_t = row & (dec_seq - 1)
    buckets = (_bucket_of(t + row_t - lane), _bucket_of(row_t - lane),
               jnp.full((t, t), REL_BUCKETS - 1, jnp.int32))
    tiles = [jnp.zeros((t, t), F32) for _ in buckets]
    for b in range(REL_BUCKETS):
        by_head = jnp.zeros((t, t), F32)
        for h in range(C_HEADS):
            by_head = jnp.where(row_head == h, relb_ref[b, h], by_head)
        tiles = [jnp.where(bk == b, by_head, tl) for bk, tl in zip(buckets, tiles)]
    for c, tl in enumerate(tiles):
        s_ref[c] = tl


def _bias_tiles(rel_bias, dec_seq):
    assert C_HEADS * dec_seq == ATT_TILE
    return pl.pallas_call(
        functools.partial(_bias_tile_kernel, dec_seq),
        out_shape=(jax.ShapeDtypeStruct((C_HEADS, REL_NEAR_TILES, ATT_TK, C_TQ), F32),
                   jax.ShapeDtypeStruct((3, ATT_TILE, ATT_TILE), F32)),
        in_specs=[pl.BlockSpec(memory_space=pltpu.SMEM)],
        name="rel_bias_tiles",
    )(rel_bias)


def _sortable(s):
    bits = lax.bitcast_convert_type(s, jnp.int32)
    return bits ^ ((bits >> 31) & 0x7FFFFFFF)


def _c_prompt_kernel(topk, qiT_ref, wiT_ref, ki_ref, qT_ref, k_ref, vT_ref, btile_ref,
                     lower_ref, o_ref, key_ref, thr_ref, m_ref, l_ref, acc_ref):
    tk, tq = ATT_TK, C_TQ
    qb = pl.program_id(1)
    n_tiles = (qb + 1) * (tq // tk)
    n_chunks = (n_tiles * tk + COUNT_CHUNK - 1) // COUNT_CHUNK
    row = lax.broadcasted_iota(jnp.int32, (tk, tq), 0)
    lane = lax.broadcasted_iota(jnp.int32, (tk, tq), 1)
    q_idx = qb * tq + lane
    w = wiT_ref[0] * (IDX_HEADS ** -0.5)

    def score_block(j, _):
        ks = pl.multiple_of(j * tk, tk)
        kib = ki_ref[0, pl.ds(ks, tk), :]
        s = jnp.zeros((tk, tq), F32)
        for hh in range(IDX_HEADS):
            d = jnp.dot(kib, qiT_ref[0, hh], preferred_element_type=F32)
            s = s + jnp.maximum(d, 0.0) * w[hh:hh + 1, :]
        s = jnp.where(ks + row <= q_idx, s, -jnp.inf)
        key_ref[pl.ds(ks, tk), :] = _sortable(s)
        return 0

    lax.fori_loop(0, n_chunks * (COUNT_CHUNK // tk), score_block, 0)

    def count(pred, thr):
        def chunk(c, cnt):
            base = pl.multiple_of(c * COUNT_CHUNK, COUNT_CHUNK)
            ind = jnp.where(pred(key_ref[pl.ds(base, COUNT_CHUNK), :], thr), 1, 0)
            return cnt + jnp.sum(ind.reshape(COUNT_CHUNK // 8, 8, tq), axis=0)
        cnt = lax.fori_loop(0, n_chunks, chunk, jnp.zeros((8, tq), jnp.int32))
        return jnp.sum(cnt, axis=0, keepdims=True)

    def bit_step(i, t_u):
        cand_u = t_u | lax.shift_left(jnp.int32(1), 31 - i)
        cnt = count(lambda x, thr: x >= thr, cand_u ^ INT_MIN)
        return jnp.where(cnt >= topk, cand_u, t_u)

    thr = lax.fori_loop(0, 32, bit_step, jnp.zeros((1, tq), jnp.int32)) ^ INT_MIN
    n_ge = count(lambda x, t: x >= t, thr)
    thr_ref[...] = thr

    @pl.when(jnp.max(jnp.abs(n_ge - topk)) > 0)
    def _():
        need = (topk - count(lambda x, t: x > t, thr)).astype(F32)

        def select_block(j, seen):
            ks = pl.multiple_of(j * tk, tk)
            blk = key_ref[pl.ds(ks, tk), :]
            eq = jnp.where(blk == thr, 1.0, 0.0)
            rank = jnp.dot(lower_ref[...], eq.astype(MXU_DTYPE), preferred_element_type=F32) + seen
            tie = jnp.where(rank < need, eq, 0.0)
            sel = jnp.where(blk > thr, 1.0, tie)
            sel = jnp.where(ks + row <= q_idx, sel, 0.0)
            key_ref[pl.ds(ks, tk), :] = sel.astype(jnp.int32)
            return seen + jnp.sum(eq, axis=0, keepdims=True)

        lax.fori_loop(0, n_tiles, select_block, jnp.zeros((1, tq), F32))
        thr_ref[...] = jnp.ones((1, tq), jnp.int32)

    m_ref[...] = jnp.full(m_ref.shape, NEG, F32)
    l_ref[...] = jnp.zeros(l_ref.shape, F32)
    acc_ref[...] = jnp.zeros(acc_ref.shape, F32)
    ones = jnp.ones((SUM_ROWS, tk), MXU_DTYPE)

    def attend(j, near):
        ks = pl.multiple_of(j * tk, tk)
        sel = key_ref[pl.ds(ks, tk), :] >= thr_ref[...]
        for n in range(C_KV_HEADS):
            kb = k_ref[0, n, pl.ds(ks, tk), :]
            vb = vT_ref[0, n, :, pl.ds(ks, tk)]
            for g in range(C_GROUP):
                h = n * C_GROUP + g
                lg = jnp.dot(kb, qT_ref[0, h], preferred_element_type=F32)
                if near is not None:
                    lg = lg + btile_ref[h, near]
                lg = jnp.where(sel, lg, NEG)
                m_old = m_ref[h]
                m_new = jnp.maximum(m_old, jnp.max(lg, axis=0, keepdims=True))
                p = jnp.exp2(lg - m_new).astype(MXU_DTYPE)
                alpha = jnp.exp2(m_old - m_new)
                l_ref[h] = alpha * l_ref[h] + jnp.dot(ones, p, preferred_element_type=F32)
                acc_ref[h] = alpha * acc_ref[h] + jnp.dot(vb, p, preferred_element_type=F32)
                m_ref[h] = m_new

    def far_tile(j, _):
        attend(j, None)
        return 0

    first_near = n_tiles - REL_NEAR_TILES
    lax.fori_loop(0, jnp.maximum(first_near, 0), far_tile, 0)
    for c in range(REL_NEAR_TILES):
        if c == 0:
            pl.when(first_near >= 0)(functools.partial(attend, first_near, 0))
        else:
            attend(first_near + c, c)
    for h in range(C_HEADS):
        o_ref[0, h] = acc_ref[h] / l_ref[h][0:1]


def _c_prompt(q, k, v, qi, ki, wi, btiles):
    b, t, _ = q.shape
    topk = min(TOPK_MAX, t // 4)
    assert t % COUNT_CHUNK == 0 and COUNT_CHUNK > topk and t % C_TQ == 0
    qT = ((q * (C_HEAD_DIM ** -0.5 * LOG2E)).reshape(b, t, C_HEADS, C_HEAD_DIM)
          .transpose(0, 2, 3, 1).astype(MXU_DTYPE))
    kh = k.reshape(b, t, C_KV_HEADS, C_HEAD_DIM).transpose(0, 2, 1, 3).astype(MXU_DTYPE)
    vT = v.reshape(b, t, C_KV_HEADS, C_HEAD_DIM).transpose(0, 2, 3, 1).astype(MXU_DTYPE)
    qi_hi, qi_lo = _split_hi_lo_mxu(
        (qi * IDX_DIM ** -0.5).reshape(b, t, IDX_HEADS, IDX_DIM).transpose(0, 2, 3, 1))
    qiT = jnp.concatenate([qi_hi, qi_lo, qi_hi], axis=2)
    ki_hi, ki_lo = _split_hi_lo_mxu(ki)
    ki3 = jnp.concatenate([ki_hi, ki_hi, ki_lo], axis=2)
    wiT = wi.transpose(0, 2, 1)
    tq, tk = C_TQ, ATT_TK
    oT = pl.pallas_call(
        functools.partial(_c_prompt_kernel, topk),
        out_shape=jax.ShapeDtypeStruct((b, C_HEADS, C_HEAD_DIM, t), F32),
        grid=(b, t // tq),
        in_specs=[
            pl.BlockSpec((1, IDX_HEADS, 3 * IDX_DIM, tq), lambda i, j: (i, 0, 0, j)),
            pl.BlockSpec((1, IDX_HEADS, tq), lambda i, j: (i, 0, j)),
            pl.BlockSpec((1, t, 3 * IDX_DIM), lambda i, j: (i, 0, 0)),
            pl.BlockSpec((1, C_HEADS, C_HEAD_DIM, tq), lambda i, j: (i, 0, 0, j)),
            pl.BlockSpec((1, C_KV_HEADS, t, C_HEAD_DIM), lambda i, j: (i, 0, 0, 0)),
            pl.BlockSpec((1, C_KV_HEADS, C_HEAD_DIM, t), lambda i, j: (i, 0, 0, 0)),
            pl.BlockSpec((C_HEADS, REL_NEAR_TILES, tk, tq), lambda i, j: (0, 0, 0, 0),
                         pipeline_mode=pl.Buffered(1)),
            pl.BlockSpec((tk, tk), lambda i, j: (0, 0)),
        ],
        out_specs=pl.BlockSpec((1, C_HEADS, C_HEAD_DIM, tq), lambda i, j: (i, 0, 0, j)),
        scratch_shapes=[
            pltpu.VMEM((t, tq), jnp.int32),
            pltpu.VMEM((1, tq), jnp.int32),
            pltpu.VMEM((C_HEADS, 1, tq), F32),
            pltpu.VMEM((C_HEADS, SUM_ROWS, tq), F32),
            pltpu.VMEM((C_HEADS, C_HEAD_DIM, tq), F32),
        ],
        compiler_params=pltpu.CompilerParams(
            dimension_semantics=("parallel", "arbitrary"), vmem_limit_bytes=VMEM_LIMIT_BYTES),
        name="c_prompt",
    )(qiT, wiT, ki3, qT, kh, vT, btiles, _strict_lower(tk))
    return oT.transpose(0, 3, 1, 2).reshape(b, t, C_WIDTH)


_NT = (((1,), (1,)), ((), ()))
_NN = (((1,), (0,)), ((), ()))


def _pad_rows(x, rows):
    return jnp.concatenate([x, jnp.zeros((rows - x.shape[0], x.shape[1]), x.dtype)], axis=0)


def _prefix_and_total(n, strict_before):
    i = np.arange(n)
    tri = (i[:, None] < i[None, :]) if strict_before else (i[:, None] > i[None, :])
    return jnp.asarray(np.concatenate([tri, np.ones((n, n), bool)], axis=1).astype(np.float32), MXU_DTYPE)


def _sb_sample_kernel(n_pages, pt_ref, q_ref, kn_ref, vn_ref, hmask_ref, sufx_ref, *rest):
    k_pages, v_pages = rest[:n_pages], rest[n_pages:2 * n_pages]
    o_ref, carry_ref, acc_ref = rest[2 * n_pages:]
    t = q_ref.shape[1]
    rows, tk = SB_HEADS * t, PAGE_SIZE
    hmask = hmask_ref[...]
    q = q_ref[0] * (SB_HEAD_DIM ** -0.5)
    qbd = (jnp.concatenate([q] * SB_HEADS, axis=0) * hmask).astype(MXU_DTYPE)
    carry_ref[...] = jnp.zeros(carry_ref.shape, F32)
    acc_ref[...] = jnp.zeros(acc_ref.shape, F32)

    def step(z, weighted_values, valid):
        sp = jnp.maximum(z, 0.0) + jnp.log(1.0 + jnp.exp(-jnp.abs(z)))
        lr = -sp if valid is None else jnp.where(valid, -sp, 0.0)
        hi, lo = _split_hi_lo_mxu(lr)
        both = (jnp.dot(hi, sufx_ref[...], preferred_element_type=F32)
                + jnp.dot(lo, sufx_ref[...], preferred_element_type=F32))
        a = jnp.exp(z - sp + both[:, :tk] + carry_ref[...])
        if valid is not None:
            a = jnp.where(valid, a, 0.0)
        acc_ref[...] += weighted_values(a.astype(MXU_DTYPE))
        carry_ref[...] += both[:, tk:]

    row_t = lax.broadcasted_iota(jnp.int32, (rows, tk), 0) & (t - 1)
    lane = lax.broadcasted_iota(jnp.int32, (rows, tk), 1)
    kn = _pad_rows(kn_ref[0], tk).astype(MXU_DTYPE)
    vn = _pad_rows(vn_ref[0], tk).astype(MXU_DTYPE)
    step(lax.dot_general(qbd, kn, _NT, preferred_element_type=F32),
         lambda a: jnp.dot(a, vn, preferred_element_type=F32), lane < row_t)
    for p in reversed(range(n_pages)):
        @pl.when(jnp.max(carry_ref[...]) > SB_EXIT)
        def _(p=p):
            kT = k_pages[p][...].reshape(SB_WIDTH, tk).astype(MXU_DTYPE)
            vT = v_pages[p][...].reshape(SB_WIDTH, tk).astype(MXU_DTYPE)
            step(jnp.dot(qbd, kT, preferred_element_type=F32),
                 lambda a: lax.dot_general(a, vT, _NT, preferred_element_type=F32), None)

    acc = acc_ref[...] * hmask
    y = acc[0:t]
    for h in range(1, SB_HEADS):
        y = y + acc[h * t:(h + 1) * t]
    o_ref[0] = y


def _feature_major(cache):
    nd = cache.ndim
    return cache.transpose((0, 1) + tuple(range(3, nd)) + (2,))


def _page_specs(n_pages, layer, page_shape):
    zeros = (0,) * len(page_shape)
    return [pl.BlockSpec((None, None) + page_shape, lambda i, pt, p=p: (layer, pt[i, p]) + zeros)
            for p in range(n_pages)]


def _sb_sample(q, k, v, cache_k, cache_v, layer, page_table):
    b, t, _ = q.shape
    assert t & (t - 1) == 0 and t <= PAGE_SIZE
    n_pages = page_table.shape[1]
    rows = SB_HEADS * t
    page = (SB_HEADS, SB_HEAD_DIM, PAGE_SIZE)
    ck, cv = _feature_major(cache_k), _feature_major(cache_v)
    hmask = jnp.asarray((np.arange(rows)[:, None] // t == np.arange(SB_WIDTH)[None, :] // SB_HEAD_DIM)
                        .astype(np.float32))
    tok = pl.BlockSpec((1, t, SB_WIDTH), lambda i, pt: (i, 0, 0))
    const = lambda i, pt: (0, 0)
    return pl.pallas_call(
        functools.partial(_sb_sample_kernel, n_pages),
        out_shape=jax.ShapeDtypeStruct((b, t, SB_WIDTH), F32),
        grid_spec=pltpu.PrefetchScalarGridSpec(
            num_scalar_prefetch=1,
            grid=(b,),
            in_specs=[tok, tok, tok,
                      pl.BlockSpec((rows, SB_WIDTH), const),
                      pl.BlockSpec((PAGE_SIZE, 2 * PAGE_SIZE), const)]
            + _page_specs(n_pages, layer, page) + _page_specs(n_pages, layer, page),
            out_specs=tok,
            scratch_shapes=[pltpu.VMEM((rows, PAGE_SIZE), F32), pltpu.VMEM((rows, SB_WIDTH), F32)],
        ),
        compiler_params=pltpu.CompilerParams(
            dimension_semantics=("arbitrary",), vmem_limit_bytes=VMEM_LIMIT_BYTES),
        name="sb_sample",
    )(page_table, q, k, v, hmask, _prefix_and_total(PAGE_SIZE, False),
      *([ck] * n_pages), *([cv] * n_pages))


def _c_sample_kernel(n_pages, topk, pt_ref, qi_ref, w_ref, q_ref, kin_ref, kn_ref, vn_ref,
                     sbias_ref, prex_ref, *rest):
    idx_pages, k_pages, v_pages = rest[:n_pages], rest[n_pages:2 * n_pages], rest[2 * n_pages:3 * n_pages]
    o_ref, sc_ref, lg_ref = rest[3 * n_pages:]
    t = kin_ref.shape[1]
    tk = PAGE_SIZE
    nb = n_pages + 1
    ih = IDX_HEADS * t
    rows = C_HEADS * t
    row8 = lax.broadcasted_iota(jnp.int32, (t, tk), 0)
    lane8 = lax.broadcasted_iota(jnp.int32, (t, tk), 1)
    new_valid = lane8 <= row8
    qi_cat = qi_ref[0]
    w = w_ref[0] * (IDX_HEADS ** -0.5)

    def scores(idx_blk, valid):
        dims = _NT if valid is not None else _NN
        hi, lo = _split_hi_lo_mxu(idx_blk)
        s = lax.dot_general(qi_cat, hi, dims, preferred_element_type=F32)
        s = s[:ih] + s[ih:] + lax.dot_general(qi_cat[:ih], lo, dims, preferred_element_type=F32)
        r = jnp.maximum(s, 0.0) * w
        sc = r[0:t]
        for hh in range(1, IDX_HEADS):
            sc = sc + r[hh * t:(hh + 1) * t]
        if valid is not None:
            sc = jnp.where(valid, sc, -jnp.inf)
        return _sortable(sc)

    for p in range(n_pages):
        sc_ref[:, p * tk:(p + 1) * tk] = scores(idx_pages[p][...], None)
    sc_ref[:, n_pages * tk:] = scores(_pad_rows(kin_ref[0], tk), new_valid)

    def count(pred, thr):
        return jnp.sum(jnp.where(pred(sc_ref[...], thr), 1.0, 0.0), axis=1, keepdims=True)

    def bit_step(i, t_u):
        cand_u = t_u | lax.shift_left(jnp.int32(1), 31 - i)
        cnt = count(lambda x, thr: x >= thr, cand_u ^ INT_MIN)
        return jnp.where(cnt >= topk, cand_u, t_u)

    thr = lax.fori_loop(0, 32, bit_step, jnp.zeros((t, 1), jnp.int32)) ^ INT_MIN
    need = topk - count(lambda x, th: x > th, thr)

    seen = jnp.zeros((t, tk), F32)
    for blk in range(nb):
        x = sc_ref[:, blk * tk:(blk + 1) * tk]
        eq = jnp.where(x == thr, 1.0, 0.0)
        both = jnp.dot(eq.astype(MXU_DTYPE), prex_ref[...], preferred_element_type=F32)
        tie = jnp.where(both[:, :tk] + seen < need, eq, 0.0)
        sel = jnp.where(x > thr, 1.0, tie)
        if blk == n_pages:
            sel = jnp.where(new_valid, sel, 0.0)
        sc_ref[:, blk * tk:(blk + 1) * tk] = sel.astype(jnp.int32)
        seen = seen + both[:, tk:]

    q_rows = q_ref[0]
    grp = C_GROUP * t
    hd = C_HEAD_DIM

    def kv_block(pages, new_ref, blk, n):
        if blk < n_pages:
            return pages[blk][n].astype(MXU_DTYPE)
        return _pad_rows(new_ref[0][:, n * hd:(n + 1) * hd], tk).astype(MXU_DTYPE)

    m = jnp.full((rows, tk), NEG, F32)
    for blk in range(nb):
        bias = sbias_ref[2] if blk < n_pages - 1 else sbias_ref[blk - (n_pages - 1)]
        lg = jnp.concatenate(
            [lax.dot_general(q_rows[n * grp:(n + 1) * grp], kv_block(k_pages, kn_ref, blk, n),
                             _NN if blk < n_pages else _NT, preferred_element_type=F32)
             for n in range(C_KV_HEADS)], axis=0) + bias
        sel = jnp.concatenate([sc_ref[:, blk * tk:(blk + 1) * tk]] * C_HEADS, axis=0) != 0
        lg = jnp.where(sel, lg, NEG)
        lg_ref[:, blk * tk:(blk + 1) * tk] = lg
        m = jnp.maximum(m, lg)
    m_row = jnp.max(m, axis=1, keepdims=True)
    lsum = jnp.zeros((rows, tk), F32)
    accs = [jnp.zeros((grp, hd), F32) for _ in range(C_KV_HEADS)]
    for blk in range(nb):
        p = jnp.exp(lg_ref[:, blk * tk:(blk + 1) * tk] - m_row)
        lsum = lsum + p
        p = p.astype(MXU_DTYPE)
        accs = [acc + lax.dot_general(p[n * grp:(n + 1) * grp], kv_block(v_pages, vn_ref, blk, n),
                                      _NT if blk < n_pages else _NN, preferred_element_type=F32)
                for n, acc in enumerate(accs)]
    l_row = jnp.sum(lsum, axis=1, keepdims=True)
    for n, acc in enumerate(accs):
        o = acc / l_row[n * grp:(n + 1) * grp]
        for g in range(C_GROUP):
            o_ref[0, n * C_GROUP + g] = o[g * t:(g + 1) * t]


def _c_sample(q, k, v, qi, ki, wi, cache_k, cache_v, cache_idx, layer, page_table, sbias):
    b, t, _ = q.shape
    n_pages = page_table.shape[1]
    topk = min(TOPK_MAX, (n_pages * PAGE_SIZE + t) // 4)
    rows = C_HEADS * t
    ih = IDX_HEADS * t
    assert rows == ATT_TILE and t <= PAGE_SIZE
    q_rows = ((q * C_HEAD_DIM ** -0.5).reshape(b, t, C_HEADS, C_HEAD_DIM).transpose(0, 2, 1, 3)
              .reshape(b, rows, C_HEAD_DIM).astype(MXU_DTYPE))
    qi_rows = ((qi * IDX_DIM ** -0.5).reshape(b, t, IDX_HEADS, IDX_DIM).transpose(0, 2, 1, 3)
               .reshape(b, ih, IDX_DIM))
    qi_cat = jnp.concatenate(_split_hi_lo_mxu(qi_rows), axis=1)
    w_col = wi.transpose(0, 2, 1).reshape(b, ih, 1)
    kv_page = (C_KV_HEADS, C_HEAD_DIM, PAGE_SIZE)
    per_seq = lambda r, c: pl.BlockSpec((1, r, c), lambda i, pt: (i, 0, 0))
    const2 = lambda i, pt: (0, 0)
    o = pl.pallas_call(
        functools.partial(_c_sample_kernel, n_pages, topk),
        out_shape=jax.ShapeDtypeStruct((b, C_HEADS, t, C_HEAD_DIM), F32),
        grid_spec=pltpu.PrefetchScalarGridSpec(
            num_scalar_prefetch=1,
            grid=(b,),
            in_specs=[per_seq(2 * ih, IDX_DIM), per_seq(ih, 1), per_seq(rows, C_HEAD_DIM),
                      per_seq(t, IDX_DIM), per_seq(t, C_KV_WIDTH), per_seq(t, C_KV_WIDTH),
                      pl.BlockSpec((3, ATT_TILE, ATT_TILE), lambda i, pt: (0, 0, 0)),
                      pl.BlockSpec((PAGE_SIZE, 2 * PAGE_SIZE), const2)]
            + _page_specs(n_pages, layer, (IDX_DIM, PAGE_SIZE)) + _page_specs(n_pages, layer, kv_page)
            + _page_specs(n_pages, layer, kv_page),
            out_specs=pl.BlockSpec((1, C_HEADS, t, C_HEAD_DIM), lambda i, pt: (i, 0, 0, 0)),
            scratch_shapes=[pltpu.VMEM((t, (n_pages + 1) * PAGE_SIZE), jnp.int32),
                            pltpu.VMEM((rows, (n_pages + 1) * PAGE_SIZE), F32)],
        ),
        compiler_params=pltpu.CompilerParams(
            dimension_semantics=("arbitrary",), vmem_limit_bytes=VMEM_LIMIT_BYTES),
        name="c_sample",
    )(page_table, qi_cat, w_col, q_rows, ki, k, v, sbias, _prefix_and_total(PAGE_SIZE, True),
      *([_feature_major(cache_idx)] * n_pages), *([_feature_major(cache_k)] * n_pages),
      *([_feature_major(cache_v)] * n_pages))
    return o.transpose(0, 2, 1, 3).reshape(b, t, C_WIDTH)


def _pool_mix(u_ext, pos, w_pool, pool_scale):
    b, l, c = u_ext.shape
    t = l - POOL_BUF
    cs = jnp.concatenate([jnp.zeros((b, 1, c), F32), jnp.cumsum(u_ext, axis=1)], axis=1)
    u_new = u_ext[:, POOL_BUF:]
    hi = cs[:, POOL_BUF + 1:]
    outs = []
    for g, w in enumerate(POOL_WINDOWS):
        sl = slice(g * POOL_GROUP, (g + 1) * POOL_GROUP)
        lo = cs[:, POOL_BUF + 1 - w:POOL_BUF + 1 - w + t, sl]
        cnt = jnp.minimum(pos + 1, w).astype(F32)[None, :, None]
        outs.append((hi[..., sl] - lo) / cnt - u_new[..., sl])
    d = jnp.stack(outs, axis=2)
    y = jnp.einsum('btgc,gcd->btgd', d, w_pool).reshape(b, t, POOL_WIDTH)
    return y * pool_scale


def _ab_prompt_mix(u, q, k, v, w_pool, pool_scale):
    b, t, _ = u.shape
    pos = jnp.arange(t, dtype=jnp.int32)
    u_ext = jnp.concatenate([jnp.zeros((b, POOL_BUF, POOL_WIDTH), F32), u], axis=1)
    y_pool = _pool_mix(u_ext, pos, w_pool, pool_scale)
    y_sb = _sb_prompt(q, k, v)
    return jnp.concatenate([y_pool, y_sb], axis=-1), u_ext[:, -POOL_BUF:]


def _ab_sample_mix(u, q, k, v, pool_buf, cache_k, cache_v, layer, page_table, w_pool, pool_scale):
    t = u.shape[1]
    q_pos = page_table.shape[1] * PAGE_SIZE + jnp.arange(t, dtype=jnp.int32)
    u_ext = jnp.concatenate([pool_buf, u], axis=1)
    y_pool = _pool_mix(u_ext, q_pos, w_pool, pool_scale)
    y_sb = _sb_sample(q, k, v, cache_k, cache_v, layer, page_table)
    return jnp.concatenate([y_pool, y_sb], axis=-1), u_ext[:, -POOL_BUF:]


def _segment_mean_matrix(width):
    seg = np.arange(width) // C_HEAD_DIM
    return jnp.asarray((seg[:, None] == seg[None, :]).astype(np.float32) / C_HEAD_DIM, BF16)


def kernel(x_prompt, x_sample, state_pool, cache_b_k, cache_b_v, cache_c_k, cache_c_v, cache_c_idx, page_table,
           g_ffn, w_ffn_gate, w_ffn_up, w_ffn_down, g_mix, w_in_ab, w_pool, pool_scale, w_out_ab,
           w_in_c, g_q, g_k, w_out_c, rel_bias):
    bp, tp, _ = x_prompt.shape
    bs, ts, _ = x_sample.shape
    mp = bp * tp
    x = jnp.concatenate([x_prompt.reshape(mp, D_MODEL), x_sample.reshape(bs * ts, D_MODEL)], axis=0)

    wg = w_ffn_gate.astype(BF16)
    wu = w_ffn_up.astype(BF16)
    wd = w_ffn_down.astype(BF16)
    w_ab = w_in_ab.astype(BF16)
    w_oab = w_out_ab.astype(BF16)
    w_c = jnp.pad(w_in_c, ((0, 0), (0, 0), (0, C_IN_PAD - C_IN))).astype(BF16)
    w_oc = w_out_c.astype(BF16)
    segq = _segment_mean_matrix(C_WIDTH)
    segk = _segment_mean_matrix(C_KV_WIDTH)
    btiles, sbias = _bias_tiles(rel_bias, ts)

    def split(a):
        return a[:mp].reshape(bp, tp, -1), a[mp:].reshape(bs, ts, -1)

    pool_p, pool_s, kbp, vbp, kbs, vbs = [], [], [], [], [], []
    kcp, vcp, icp, kcs, vcs, ics = [], [], [], [], [], []
    for layer in range(DEPTH):
        j = layer // 2
        x = _ffn(x, g_ffn[layer, 0][None], wg[layer, 0], wu[layer, 0], wd[layer, 0])
        g = g_mix[layer][None]
        if layer % 2 == 0:
            u, q, k, v = _proj_ab(x, g, w_ab[j])
            (u_p, u_s), (q_p, q_s), (k_p, k_s), (v_p, v_s) = split(u), split(q), split(k), split(v)
            y_p, sp = _ab_prompt_mix(u_p, q_p, k_p, v_p, w_pool[j], pool_scale[j])
            y_s, ss = _ab_sample_mix(u_s, q_s, k_s, v_s, state_pool[j], cache_b_k, cache_b_v, j,
                                     page_table, w_pool[j], pool_scale[j])
            pool_p.append(sp); pool_s.append(ss)
            kbp.append(k_p.reshape(bp, tp, SB_HEADS, SB_HEAD_DIM))
            vbp.append(v_p.reshape(bp, tp, SB_HEADS, SB_HEAD_DIM))
            kbs.append(k_s.reshape(bs, ts, SB_HEADS, SB_HEAD_DIM))
            vbs.append(v_s.reshape(bs, ts, SB_HEADS, SB_HEAD_DIM))
            w_out = w_oab[j]
        else:
            gq = jnp.tile(g_q[j], C_HEADS)[None]
            gk = jnp.tile(g_k[j], C_KV_HEADS)[None]
            q, k, v, qi, tail = _proj_c(x, g, w_c[j], gq, gk, segq, segk)
            ki = tail[:, :IDX_DIM]
            wi = tail[:, IDX_DIM:IDX_DIM + IDX_HEADS]
            (q_p, q_s), (k_p, k_s), (v_p, v_s) = split(q), split(k), split(v)
            (qi_p, qi_s), (ki_p, ki_s), (wi_p, wi_s) = split(qi), split(ki), split(wi)
            y_p = _c_prompt(q_p, k_p, v_p, qi_p, ki_p, wi_p, btiles)
            y_s = _c_sample(q_s, k_s, v_s, qi_s, ki_s, wi_s, cache_c_k, cache_c_v, cache_c_idx,
                            j, page_table, sbias)
            kcp.append(k_p.reshape(bp, tp, C_KV_HEADS, C_HEAD_DIM))
            vcp.append(v_p.reshape(bp, tp, C_KV_HEADS, C_HEAD_DIM))
            icp.append(ki_p)
            kcs.append(k_s.reshape(bs, ts, C_KV_HEADS, C_HEAD_DIM))
            vcs.append(v_s.reshape(bs, ts, C_KV_HEADS, C_HEAD_DIM))
            ics.append(ki_s)
            w_out = w_oc[j]
        y = jnp.concatenate([y_p.reshape(mp, D_MODEL), y_s.reshape(bs * ts, D_MODEL)], axis=0)
        x = _out_proj(x, y, w_out)
        x = _ffn(x, g_ffn[layer, 1][None], wg[layer, 1], wu[layer, 1], wd[layer, 1])

    return (x[:mp].reshape(bp, tp, D_MODEL), x[mp:].reshape(bs, ts, D_MODEL),
            jnp.stack(pool_p), jnp.stack(pool_s),
            jnp.stack(kbp), jnp.stack(vbp), jnp.stack(kbs), jnp.stack(vbs),
            jnp.stack(kcp), jnp.stack(vcp), jnp.stack(icp),
            jnp.stack(kcs), jnp.stack(vcs), jnp.stack(ics))
```

```python
import functools
import math

import jax
import jax.numpy as jnp
import numpy as np
from jax import lax
from jax.experimental import pallas as pl
from jax.experimental.pallas import tpu as pltpu

F32 = jnp.float32
BF16 = jnp.bfloat16
MXU_DTYPE = BF16

D_MODEL = 1024
DEPTH = 4
D_FF = 2816
POOL_WINDOWS = (2, 4, 8, 16)
POOL_WIDTH = 512
POOL_GROUP = 128
POOL_BUF = 15
SB_HEADS = 8
SB_HEAD_DIM = 64
SB_WIDTH = 512
AB_IN = POOL_WIDTH + 3 * SB_WIDTH
C_HEAD_DIM = 64
C_HEADS = 16
C_KV_HEADS = 4
C_GROUP = 4
C_WIDTH = 1024
C_KV_WIDTH = 256
IDX_HEADS = 8
IDX_DIM = 64
TOPK_MAX = 256
C_IN = 2120
C_IN_PAD = 2176
REL_BUCKETS = 32
REL_MAX_EXACT = 16
REL_MAX_DIST = 128
Q_BLOCK = 128
PAGE_SIZE = 128
EPS = 1e-6
NEG = -1e30

VMEM_LIMIT_BYTES = 56 * 1024 * 1024
FF_CHUNK = 256
TOKEN_TILE = 512


def _rms(x, g):
    ms = jnp.mean(x * x, axis=-1, keepdims=True)
    return x * lax.rsqrt(ms + EPS) * g


def _ffn_kernel(x_ref, g_ref, wg_ref, wu_ref, wd_ref, o_ref):
    x = x_ref[...]
    h = _rms(x, g_ref[...]).astype(MXU_DTYPE)
    acc = x
    for c in range(D_FF // FF_CHUNK):
        sl = slice(c * FF_CHUNK, (c + 1) * FF_CHUNK)
        gate = jnp.dot(h, wg_ref[:, sl], preferred_element_type=F32)
        up = jnp.dot(h, wu_ref[:, sl], preferred_element_type=F32)
        act = (0.5 * gate * jax.nn.sigmoid(gate) * up).astype(MXU_DTYPE)
        acc = acc + jnp.dot(act, wd_ref[sl, :], preferred_element_type=F32)
    o_ref[...] = acc


def _ffn(x, g, wg, wu, wd):
    m = x.shape[0]
    resident = dict(pipeline_mode=pl.Buffered(1))
    return pl.pallas_call(
        _ffn_kernel,
        out_shape=jax.ShapeDtypeStruct((m, D_MODEL), F32),
        grid=(m // TOKEN_TILE,),
        in_specs=[
            pl.BlockSpec((TOKEN_TILE, D_MODEL), lambda i: (i, 0)),
            pl.BlockSpec((1, D_MODEL), lambda i: (0, 0)),
            pl.BlockSpec((D_MODEL, D_FF), lambda i: (0, 0), **resident),
            pl.BlockSpec((D_MODEL, D_FF), lambda i: (0, 0), **resident),
            pl.BlockSpec((D_FF, D_MODEL), lambda i: (0, 0), **resident),
        ],
        out_specs=pl.BlockSpec((TOKEN_TILE, D_MODEL), lambda i: (i, 0)),
        compiler_params=pltpu.CompilerParams(
            dimension_semantics=("parallel",), vmem_limit_bytes=VMEM_LIMIT_BYTES),
        name="ffn_half",
    )(x, g, wg, wu, wd)


def _proj_ab_sample_kernel(x_ref, g_ref, w_ref, u_ref, q_ref, k_ref, v_ref):
    h = _rms(x_ref[...], g_ref[...]).astype(MXU_DTYPE)
    p = jnp.dot(h, w_ref[...], preferred_element_type=F32)
    u_ref[...] = p[:, :POOL_WIDTH]
    q_ref[...] = p[:, POOL_WIDTH:POOL_WIDTH + SB_WIDTH]
    k_ref[...] = p[:, POOL_WIDTH + SB_WIDTH:POOL_WIDTH + 2 * SB_WIDTH]
    v_ref[...] = p[:, POOL_WIDTH + 2 * SB_WIDTH:]


def _proj_ab_sample(x, g, w, row0, rows):
    first = row0 // TOKEN_TILE
    out = jax.ShapeDtypeStruct((rows, SB_WIDTH), F32)
    spec = pl.BlockSpec((TOKEN_TILE, SB_WIDTH), lambda i: (i, 0))
    return pl.pallas_call(
        _proj_ab_sample_kernel,
        out_shape=(out, out, out, out),
        grid=(rows // TOKEN_TILE,),
        in_specs=[
            pl.BlockSpec((TOKEN_TILE, D_MODEL), lambda i: (first + i, 0)),
            pl.BlockSpec((1, D_MODEL), lambda i: (0, 0)),
            pl.BlockSpec((D_MODEL, AB_IN), lambda i: (0, 0), pipeline_mode=pl.Buffered(1)),
        ],
        out_specs=(spec, spec, spec, spec),
        compiler_params=pltpu.CompilerParams(
            dimension_semantics=("parallel",), vmem_limit_bytes=VMEM_LIMIT_BYTES),
        name="proj_ab_sample",
    )(x, g, w)


def _proj_ab_prompt_kernel(x_ref, g_ref, w_ref, *rest):
    u_ref, qT_ref, kn_ref, kT_ref, vT_ref = rest[-5:]
    h = _rms(x_ref[...], g_ref[...]).astype(MXU_DTYPE)
    p = jnp.dot(h, w_ref[...], preferred_element_type=F32)
    u_ref[...] = p[:, :POOL_WIDTH]
    q = p[:, POOL_WIDTH:POOL_WIDTH + SB_WIDTH] * (SB_HEAD_DIM ** -0.5)
    k = p[:, POOL_WIDTH + SB_WIDTH:POOL_WIDTH + 2 * SB_WIDTH]
    v = p[:, POOL_WIDTH + 2 * SB_WIDTH:]
    qT_ref[0] = q.T.astype(MXU_DTYPE)
    kn_ref[0] = k.astype(MXU_DTYPE)
    kT_ref[0] = k.T
    vT_ref[0] = v.T


def _slot_alias(prev, first_input, first_output):
    if prev is None:
        return [], {}
    specs = [pl.BlockSpec(memory_space=pl.ANY)] * len(prev)
    return specs, {first_input + n: first_output + n for n in range(len(prev))}


def _proj_ab_prompt(x, g, w, batch, seq, slot, n_slots, prev):
    nt = seq // TOKEN_TILE
    alias_specs, aliases = _slot_alias(prev, 3, 3)
    stacked = jax.ShapeDtypeStruct((n_slots, batch, SB_WIDTH, seq), F32)
    stacked_spec = pl.BlockSpec((None, 1, SB_WIDTH, TOKEN_TILE), lambda b, i: (slot, b, 0, i))
    return pl.pallas_call(
        _proj_ab_prompt_kernel,
        out_shape=(jax.ShapeDtypeStruct((batch * seq, POOL_WIDTH), F32),
                   jax.ShapeDtypeStruct((batch, SB_WIDTH, seq), MXU_DTYPE),
                   jax.ShapeDtypeStruct((batch, seq, SB_WIDTH), MXU_DTYPE),
                   stacked, stacked),
        grid=(batch, nt),
        in_specs=[
            pl.BlockSpec((TOKEN_TILE, D_MODEL), lambda b, i: (b * nt + i, 0)),
            pl.BlockSpec((1, D_MODEL), lambda b, i: (0, 0)),
            pl.BlockSpec((D_MODEL, AB_IN), lambda b, i: (0, 0), pipeline_mode=pl.Buffered(1)),
        ] + alias_specs,
        out_specs=(pl.BlockSpec((TOKEN_TILE, POOL_WIDTH), lambda b, i: (b * nt + i, 0)),
                   pl.BlockSpec((1, SB_WIDTH, TOKEN_TILE), lambda b, i: (b, 0, i)),
                   pl.BlockSpec((1, TOKEN_TILE, SB_WIDTH), lambda b, i: (b, i, 0)),
                   stacked_spec, stacked_spec),
        input_output_aliases=aliases,
        compiler_params=pltpu.CompilerParams(
            dimension_semantics=("parallel", "parallel"), vmem_limit_bytes=VMEM_LIMIT_BYTES),
        name="proj_ab_prompt",
    )(x, g, w, *(prev or ()))


def _split_hi_lo(x):
    hi = x.astype(MXU_DTYPE)
    lo = (x - hi.astype(F32)).astype(MXU_DTYPE)
    return hi, lo


def _head_rms(x, seg_mean_ref, gain):
    hi, lo = _split_hi_lo(x * x)
    ms = (jnp.dot(hi, seg_mean_ref[...], preferred_element_type=F32)
          + jnp.dot(lo, seg_mean_ref[...], preferred_element_type=F32))
    return x * lax.rsqrt(ms + EPS) * gain


C_OFF_K = C_WIDTH
C_OFF_V = C_OFF_K + C_KV_WIDTH
C_OFF_QI = C_OFF_V + C_KV_WIDTH
C_OFF_TAIL = C_OFF_QI + IDX_HEADS * IDX_DIM
C_TAIL = C_IN_PAD - C_OFF_TAIL


def _proj_c_parts(x_ref, g_ref, w_ref, gq_ref, gk_ref, segq_ref, segk_ref):
    h = _rms(x_ref[...], g_ref[...]).astype(MXU_DTYPE)
    p = jnp.dot(h, w_ref[...], preferred_element_type=F32)
    q = _head_rms(p[:, :C_OFF_K], segq_ref, gq_ref[...])
    k = _head_rms(p[:, C_OFF_K:C_OFF_V], segk_ref, gk_ref[...])
    return q, k, p[:, C_OFF_V:C_OFF_QI], p[:, C_OFF_QI:C_OFF_TAIL], p[:, C_OFF_TAIL:]


def _proj_c_sample_kernel(x_ref, g_ref, w_ref, gq_ref, gk_ref, segq_ref, segk_ref,
                          q_ref, k_ref, v_ref, qi_ref, tail_ref):
    q, k, v, qi, tail = _proj_c_parts(x_ref, g_ref, w_ref, gq_ref, gk_ref, segq_ref, segk_ref)
    q_ref[...] = q
    k_ref[...] = k
    v_ref[...] = v
    qi_ref[...] = qi
    tail_ref[...] = tail


def _proj_c_in_specs(row_block):
    const = lambda *_: (0, 0)
    return [
        pl.BlockSpec((TOKEN_TILE, D_MODEL), row_block),
        pl.BlockSpec((1, D_MODEL), const),
        pl.BlockSpec((D_MODEL, C_IN_PAD), const, pipeline_mode=pl.Buffered(1)),
        pl.BlockSpec((1, C_WIDTH), const),
        pl.BlockSpec((1, C_KV_WIDTH), const),
        pl.BlockSpec((C_WIDTH, C_WIDTH), const, pipeline_mode=pl.Buffered(1)),
        pl.BlockSpec((C_KV_WIDTH, C_KV_WIDTH), const, pipeline_mode=pl.Buffered(1)),
    ]


def _proj_c_sample(x, g, w, gq, gk, segq, segk, row0, rows):
    first = row0 // TOKEN_TILE
    widths = (C_WIDTH, C_KV_WIDTH, C_KV_WIDTH, IDX_HEADS * IDX_DIM, C_TAIL)
    return pl.pallas_call(
        _proj_c_sample_kernel,
        out_shape=tuple(jax.ShapeDtypeStruct((rows, n), F32) for n in widths),
        grid=(rows // TOKEN_TILE,),
        in_specs=_proj_c_in_specs(lambda i: (first + i, 0)),
        out_specs=tuple(pl.BlockSpec((TOKEN_TILE, n), lambda i: (i, 0)) for n in widths),
        compiler_params=pltpu.CompilerParams(
            dimension_semantics=("parallel",), vmem_limit_bytes=VMEM_LIMIT_BYTES),
        name="proj_c_sample",
    )(x, g, w, gq, gk, segq, segk)


def _proj_c_prompt_kernel(x_ref, g_ref, w_ref, gq_ref, gk_ref, segq_ref, segk_ref, *rest):
    qT_ref, kn_ref, qiT_ref, wiT_ref, ki3_ref, kT_ref, vT_ref, kiT_ref = rest[-8:]
    q, k, v, qi, tail = _proj_c_parts(x_ref, g_ref, w_ref, gq_ref, gk_ref, segq_ref, segk_ref)
    tm = q.shape[0]
    qT_ref[0] = (q * (C_HEAD_DIM ** -0.5 * LOG2E)).T.astype(MXU_DTYPE).reshape(C_HEADS, C_HEAD_DIM, tm)
    kn_ref[0] = k.astype(MXU_DTYPE)
    kT_ref[0] = k.T
    vT_ref[0] = v.T
    qi_hi, qi_lo = _split_hi_lo((qi * (IDX_DIM ** -0.5)).T)
    for hh in range(IDX_HEADS):
        rows = slice(hh * IDX_DIM, (hh + 1) * IDX_DIM)
        qiT_ref[0, hh, 0:IDX_DIM] = qi_hi[rows]
        qiT_ref[0, hh, IDX_DIM:2 * IDX_DIM] = qi_lo[rows]
        qiT_ref[0, hh, 2 * IDX_DIM:] = qi_hi[rows]
    tail_t = tail.T
    kiT_ref[0] = tail_t[:IDX_DIM]
    wiT_ref[0] = tail_t[IDX_DIM:IDX_DIM + IDX_HEADS]
    ki_hi, ki_lo = _split_hi_lo(tail[:, :IDX_DIM])
    ki3_ref[0] = jnp.concatenate([ki_hi, ki_hi, ki_lo], axis=1)


def _proj_c_prompt(x, g, w, gq, gk, segq, segk, batch, seq, slot, n_slots, prev):
    nt = seq // TOKEN_TILE
    tm = TOKEN_TILE
    alias_specs, aliases = _slot_alias(prev, 7, 5)

    def stacked(width):
        return (jax.ShapeDtypeStruct((n_slots, batch, width, seq), F32),
                pl.BlockSpec((None, 1, width, tm), lambda b, i: (slot, b, 0, i)))

    (kT_s, kT_b), (vT_s, vT_b), (kiT_s, kiT_b) = stacked(C_KV_WIDTH), stacked(C_KV_WIDTH), stacked(IDX_DIM)
    return pl.pallas_call(
        _proj_c_prompt_kernel,
        out_shape=(jax.ShapeDtypeStruct((batch, C_HEADS, C_HEAD_DIM, seq), MXU_DTYPE),
                   jax.ShapeDtypeStruct((batch, seq, C_KV_WIDTH), MXU_DTYPE),
                   jax.ShapeDtypeStruct((batch, IDX_HEADS, 3 * IDX_DIM, seq), MXU_DTYPE),
                   jax.ShapeDtypeStruct((batch, IDX_HEADS, seq), F32),
                   jax.ShapeDtypeStruct((batch, seq, 3 * IDX_DIM), MXU_DTYPE),
                   kT_s, vT_s, kiT_s),
        grid=(batch, nt),
        in_specs=_proj_c_in_specs(lambda b, i: (b * nt + i, 0)) + alias_specs,
        out_specs=(pl.BlockSpec((1, C_HEADS, C_HEAD_DIM, tm), lambda b, i: (b, 0, 0, i)),
                   pl.BlockSpec((1, tm, C_KV_WIDTH), lambda b, i: (b, i, 0)),
                   pl.BlockSpec((1, IDX_HEADS, 3 * IDX_DIM, tm), lambda b, i: (b, 0, 0, i)),
                   pl.BlockSpec((1, IDX_HEADS, tm), lambda b, i: (b, 0, i)),
                   pl.BlockSpec((1, tm, 3 * IDX_DIM), lambda b, i: (b, i, 0)),
                   kT_b, vT_b, kiT_b),
        input_output_aliases=aliases,
        compiler_params=pltpu.CompilerParams(
            dimension_semantics=("parallel", "parallel"), vmem_limit_bytes=VMEM_LIMIT_BYTES),
        name="proj_c_prompt",
    )(x, g, w, gq, gk, segq, segk, *(prev or ()))


def _out_proj_kernel(n_parts, x_ref, *rest):
    parts, w_ref, o_ref = rest[:n_parts], rest[n_parts], rest[-1]
    acc = x_ref[...]
    off = 0
    for part in parts:
        width = part.shape[1]
        acc = acc + jnp.dot(part[...].astype(MXU_DTYPE), w_ref[off:off + width, :],
                            preferred_element_type=F32)
        off += width
    o_ref[...] = acc


def _out_proj(x, parts, w, row0, prev=None):
    rows = parts[0].shape[0]
    first = row0 // TOKEN_TILE
    x_tile = pl.BlockSpec((TOKEN_TILE, D_MODEL), lambda i: (first + i, 0))
    alias_specs, aliases = _slot_alias(None if prev is None else (prev,), 2 + len(parts), 0)
    return pl.pallas_call(
        functools.partial(_out_proj_kernel, len(parts)),
        out_shape=jax.ShapeDtypeStruct(x.shape, F32),
        grid=(rows // TOKEN_TILE,),
        in_specs=[x_tile]
        + [pl.BlockSpec((TOKEN_TILE, p.shape[1]), lambda i: (i, 0)) for p in parts]
        + [pl.BlockSpec((D_MODEL, D_MODEL), lambda i: (0, 0), pipeline_mode=pl.Buffered(1))]
        + alias_specs,
        out_specs=x_tile,
        input_output_aliases=aliases,
        compiler_params=pltpu.CompilerParams(
            dimension_semantics=("parallel",), vmem_limit_bytes=VMEM_LIMIT_BYTES),
        name="out_proj",
    )(x, *parts, w, *(() if prev is None else (prev,)))


ATT_TILE = 128
ATT_TK = 128
ATT_TQ = 256
SB_LOCKSTEP = 2
SB_EXIT = -104.0


def _sb_prompt_kernel(qT_ref, k_ref, vT_ref, upper_ref, o_ref):
    tk, tq = ATT_TK, ATT_TQ
    qb = pl.program_id(1)
    row = lax.broadcasted_iota(jnp.int32, (tk, tq), 0)
    lane = lax.broadcasted_iota(jnp.int32, (tk, tq), 1)
    q_idx = qb * tq + lane
    upper = upper_ref[...]
    last_tile = qb * (tq // tk) + (tq // tk - 1)

    def head_step(h, ks, valid, carry, acc):
        hs = slice(h * SB_HEAD_DIM, (h + 1) * SB_HEAD_DIM)
        kb = k_ref[0, pl.ds(ks, tk), hs]
        z = jnp.dot(kb, qT_ref[0, hs, :], preferred_element_type=F32)
        sp = jnp.maximum(z, 0.0) + jnp.log(1.0 + jnp.exp(-jnp.abs(z)))
        lr = jnp.where(valid, -sp, 0.0)
        hi, lo = _split_hi_lo(lr)
        between = (jnp.dot(upper, hi, preferred_element_type=F32)
                   + jnp.dot(upper, lo, preferred_element_type=F32) + carry)
        a = jnp.where(valid, jnp.exp(z - sp + between), 0.0)
        vb = vT_ref[0, hs, pl.ds(ks, tk)].astype(MXU_DTYPE)
        acc = acc + jnp.dot(vb, a.astype(MXU_DTYPE), preferred_element_type=F32)
        return carry + jnp.sum(lr, axis=0, keepdims=True), acc

    for h0 in range(0, SB_HEADS, SB_LOCKSTEP):
        heads = range(h0, h0 + SB_LOCKSTEP)

        def body(state, heads=heads):
            j, _, carries, accs = state
            ks = pl.multiple_of(j * tk, tk)
            valid = (ks + row) < q_idx
            out = [head_step(h, ks, valid, c, a) for h, c, a in zip(heads, carries, accs)]
            carries = tuple(o[0] for o in out)
            top = functools.reduce(jnp.maximum, [jnp.max(c) for c in carries])
            return j - 1, top, carries, tuple(o[1] for o in out)

        def cond(state):
            j, top, _, _ = state
            return jnp.logical_and(j >= 0, top > SB_EXIT)

        init = (last_tile, jnp.float32(0.0),
                tuple(jnp.zeros((1, tq), F32) for _ in heads),
                tuple(jnp.zeros((SB_HEAD_DIM, tq), F32) for _ in heads))
        _, _, _, accs = lax.while_loop(cond, body, init)
        o_ref[0, :, h0 * SB_HEAD_DIM:(h0 + SB_LOCKSTEP) * SB_HEAD_DIM] = jnp.concatenate(accs, axis=0).T


def _strict_upper(n):
    i = np.arange(n)
    return jnp.asarray((i[None, :] > i[:, None]).astype(np.float32), MXU_DTYPE)


def _strict_lower(n):
    i = np.arange(n)
    return jnp.asarray((i[None, :] < i[:, None]).astype(np.float32), MXU_DTYPE)


def _sb_prompt(qT, k, vT_all, slot):
    b, width, t = qT.shape
    assert (SB_LOCKSTEP * SB_HEAD_DIM) % 128 == 0
    return pl.pallas_call(
        _sb_prompt_kernel,
        out_shape=jax.ShapeDtypeStruct((b, t, width), F32),
        grid=(b, t // ATT_TQ),
        in_specs=[
            pl.BlockSpec((1, width, ATT_TQ), lambda i, j: (i, 0, j)),
            pl.BlockSpec((1, t, width), lambda i, j: (i, 0, 0)),
            pl.BlockSpec((None, 1, width, t), lambda i, j: (slot, i, 0, 0)),
            pl.BlockSpec((ATT_TK, ATT_TK), lambda i, j: (0, 0)),
        ],
        out_specs=pl.BlockSpec((1, ATT_TQ, width), lambda i, j: (i, j, 0)),
        compiler_params=pltpu.CompilerParams(
            dimension_semantics=("parallel", "arbitrary"), vmem_limit_bytes=VMEM_LIMIT_BYTES),
        name="sb_prompt",
    )(qT, k, vT_all, _strict_upper(ATT_TK))


INT_MIN = -2 ** 31
COUNT_CHUNK = 4 * ATT_TK
LOG2E = math.log2(math.e)
C_TQ = 256
REL_NEAR_TILES = C_TQ // ATT_TK + 1
REL_LAST_BUCKET_FROM = math.ceil(REL_MAX_EXACT * (REL_MAX_DIST / REL_MAX_EXACT) ** (
    (REL_BUCKETS - 1 - REL_MAX_EXACT) / (REL_BUCKETS - REL_MAX_EXACT)))
assert ATT_TK + 1 >= REL_LAST_BUCKET_FROM
SUM_ROWS = 16


def _bucket_of(dist):
    n = jnp.maximum(dist, 0)
    nf = jnp.maximum(n, 1).astype(F32)
    large = REL_MAX_EXACT + (jnp.log(nf / REL_MAX_EXACT) / math.log(REL_MAX_DIST / REL_MAX_EXACT)
                             * (REL_BUCKETS - REL_MAX_EXACT)).astype(jnp.int32)
    return jnp.where(n < REL_MAX_EXACT, n, jnp.minimum(large, REL_BUCKETS - 1))


def _bias_tile_kernel(dec_seq, relb_ref, o_ref, s_ref):
    krow = lax.broadcasted_iota(jnp.int32, (ATT_TK, C_TQ), 0)
    qlane = lax.broadcasted_iota(jnp.int32, (ATT_TK, C_TQ), 1)
    for c in range(REL_NEAR_TILES):
        bucket = _bucket_of((1 - c) * ATT_TK + qlane - krow)
        for h in range(C_HEADS):
            tile = jnp.zeros((ATT_TK, C_TQ), F32)
            for b in range(REL_BUCKETS):
                tile = jnp.where(bucket == b, relb_ref[b, h], tile)
            o_ref[h, c] = (tile - relb_ref[REL_BUCKETS - 1, h]) * LOG2E
    t = ATT_TILE
    row = lax.broadcasted_iota(jnp.int32, (t, t), 0)
    lane = lax.broadcasted_iota(jnp.int32, (t, t), 1)
    row_head = row >> int(math.log2(dec_seq))
    row_t = row & (dec_seq - 1)
    buckets = (_bucket_of(t + row_t - lane), _bucket_of(row_t - lane),
               jnp.full((t, t), REL_BUCKETS - 1, jnp.int32))
    tiles = [jnp.zeros((t, t), F32) for _ in buckets]
    for b in range(REL_BUCKETS):
        by_head = jnp.zeros((t, t), F32)
        for h in range(C_HEADS):
            by_head = jnp.where(row_head == h, relb_ref[b, h], by_head)
        tiles = [jnp.where(bk == b, by_head, tl) for bk, tl in zip(buckets, tiles)]
    for c, tl in enumerate(tiles):
        s_ref[c] = tl


def _bias_tiles(rel_bias, dec_seq):
    assert C_HEADS * dec_seq == ATT_TILE
    return pl.pallas_call(
        functools.partial(_bias_tile_kernel, dec_seq),
        out_shape=(jax.ShapeDtypeStruct((C_HEADS, REL_NEAR_TILES, ATT_TK, C_TQ), F32),
                   jax.ShapeDtypeStruct((3, ATT_TILE, ATT_TILE), F32)),
        in_specs=[pl.BlockSpec(memory_space=pltpu.SMEM)],
        name="rel_bias_tiles",
    )(rel_bias)


def _sortable(s):
    bits = lax.bitcast_convert_type(s, jnp.int32)
    return bits ^ ((bits >> 31) & 0x7FFFFFFF)


def _c_prompt_kernel(topk, qiT_ref, wiT_ref, ki_ref, qT_ref, k_ref, vT_ref, btile_ref,
                     lower_ref, o_ref, key_ref, thr_ref, m_ref, l_ref, acc_ref):
    tk, tq = ATT_TK, C_TQ
    qb = pl.program_id(1)
    n_tiles = (qb + 1) * (tq // tk)
    n_chunks = (n_tiles * tk + COUNT_CHUNK - 1) // COUNT_CHUNK
    row = lax.broadcasted_iota(jnp.int32, (tk, tq), 0)
    lane = lax.broadcasted_iota(jnp.int32, (tk, tq), 1)
    q_idx = qb * tq + lane
    w = wiT_ref[0] * (IDX_HEADS ** -0.5)

    def score_block(j, _):
        ks = pl.multiple_of(j * tk, tk)
        kib = ki_ref[0, pl.ds(ks, tk), :]
        s = jnp.zeros((tk, tq), F32)
        for hh in range(IDX_HEADS):
            d = jnp.dot(kib, qiT_ref[0, hh], preferred_element_type=F32)
            s = s + jnp.maximum(d, 0.0) * w[hh:hh + 1, :]
        s = jnp.where(ks + row <= q_idx, s, -jnp.inf)
        key_ref[pl.ds(ks, tk), :] = _sortable(s)
        return 0

    lax.fori_loop(0, n_chunks * (COUNT_CHUNK // tk), score_block, 0)

    def count(pred, thr):
        def chunk(c, cnt):
            base = pl.multiple_of(c * COUNT_CHUNK, COUNT_CHUNK)
            ind = jnp.where(pred(key_ref[pl.ds(base, COUNT_CHUNK), :], thr), 1, 0)
            return cnt + jnp.sum(ind.reshape(COUNT_CHUNK // 8, 8, tq), axis=0)
        cnt = lax.fori_loop(0, n_chunks, chunk, jnp.zeros((8, tq), jnp.int32))
        return jnp.sum(cnt, axis=0, keepdims=True)

    def bit_step(i, t_u):
        cand_u = t_u | lax.shift_left(jnp.int32(1), 31 - i)
        cnt = count(lambda x, thr: x >= thr, cand_u ^ INT_MIN)
        return jnp.where(cnt >= topk, cand_u, t_u)

    thr = lax.fori_loop(0, 32, bit_step, jnp.zeros((1, tq), jnp.int32)) ^ INT_MIN
    n_ge = count(lambda x, t: x >= t, thr)
    thr_ref[...] = thr

    @pl.when(jnp.max(jnp.abs(n_ge - topk)) > 0)
    def _():
        need = (topk - count(lambda x, t: x > t, thr)).astype(F32)

        def select_block(j, seen):
            ks = pl.multiple_of(j * tk, tk)
            blk = key_ref[pl.ds(ks, tk), :]
            eq = jnp.where(blk == thr, 1.0, 0.0)
            rank = jnp.dot(lower_ref[...], eq.astype(MXU_DTYPE), preferred_element_type=F32) + seen
            tie = jnp.where(rank < need, eq, 0.0)
            sel = jnp.where(blk > thr, 1.0, tie)
            sel = jnp.where(ks + row <= q_idx, sel, 0.0)
            key_ref[pl.ds(ks, tk), :] = sel.astype(jnp.int32)
            return seen + jnp.sum(eq, axis=0, keepdims=True)

        lax.fori_loop(0, n_tiles, select_block, jnp.zeros((1, tq), F32))
        thr_ref[...] = jnp.ones((1, tq), jnp.int32)

    m_ref[...] = jnp.full(m_ref.shape, NEG, F32)
    l_ref[...] = jnp.zeros(l_ref.shape, F32)
    acc_ref[...] = jnp.zeros(acc_ref.shape, F32)
    ones = jnp.ones((SUM_ROWS, tk), MXU_DTYPE)

    def attend(j, near):
        ks = pl.multiple_of(j * tk, tk)
        sel = key_ref[pl.ds(ks, tk), :] >= thr_ref[...]
        for n in range(C_KV_HEADS):
            ns = slice(n * C_HEAD_DIM, (n + 1) * C_HEAD_DIM)
            kb = k_ref[0, pl.ds(ks, tk), ns]
            vb = vT_ref[0, ns, pl.ds(ks, tk)].astype(MXU_DTYPE)
            for g in range(C_GROUP):
                h = n * C_GROUP + g
                lg = jnp.dot(kb, qT_ref[0, h], preferred_element_type=F32)
                if near is not None:
                    lg = lg + btile_ref[h, near]
                lg = jnp.where(sel, lg, NEG)
                m_old = m_ref[h]
                m_new = jnp.maximum(m_old, jnp.max(lg, axis=0, keepdims=True))
                p = jnp.exp2(lg - m_new).astype(MXU_DTYPE)
                alpha = jnp.exp2(m_old - m_new)
                l_ref[h] = alpha * l_ref[h] + jnp.dot(ones, p, preferred_element_type=F32)
                acc_ref[h] = alpha * acc_ref[h] + jnp.dot(vb, p, preferred_element_type=F32)
                m_ref[h] = m_new

    def far_tile(j, _):
        attend(j, None)
        return 0

    first_near = n_tiles - REL_NEAR_TILES
    lax.fori_loop(0, jnp.maximum(first_near, 0), far_tile, 0)
    for c in range(REL_NEAR_TILES):
        if c == 0:
            pl.when(first_near >= 0)(functools.partial(attend, first_near, 0))
        else:
            attend(first_near + c, c)
    for h in range(0, C_HEADS, 2):
        pair = jnp.concatenate([acc_ref[h] / l_ref[h][0:1], acc_ref[h + 1] / l_ref[h + 1][0:1]], axis=0)
        o_ref[0, :, h * C_HEAD_DIM:(h + 2) * C_HEAD_DIM] = pair.T


def _c_prompt(qT, k, vT_all, slot, qiT, wiT, ki3, btiles):
    b, _, _, t = qT.shape
    topk = min(TOPK_MAX, t // 4)
    assert t % COUNT_CHUNK == 0 and COUNT_CHUNK > topk and t % C_TQ == 0
    tq, tk = C_TQ, ATT_TK
    return pl.pallas_call(
        functools.partial(_c_prompt_kernel, topk),
        out_shape=jax.ShapeDtypeStruct((b, t, C_WIDTH), F32),
        grid=(b, t // tq),
        in_specs=[
            pl.BlockSpec((1, IDX_HEADS, 3 * IDX_DIM, tq), lambda i, j: (i, 0, 0, j)),
            pl.BlockSpec((1, IDX_HEADS, tq), lambda i, j: (i, 0, j)),
            pl.BlockSpec((1, t, 3 * IDX_DIM), lambda i, j: (i, 0, 0)),
            pl.BlockSpec((1, C_HEADS, C_HEAD_DIM, tq), lambda i, j: (i, 0, 0, j)),
            pl.BlockSpec((1, t, C_KV_WIDTH), lambda i, j: (i, 0, 0)),
            pl.BlockSpec((None, 1, C_KV_WIDTH, t), lambda i, j: (slot, i, 0, 0)),
            pl.BlockSpec((C_HEADS, REL_NEAR_TILES, tk, tq), lambda i, j: (0, 0, 0, 0),
                         pipeline_mode=pl.Buffered(1)),
            pl.BlockSpec((tk, tk), lambda i, j: (0, 0)),
        ],
        out_specs=pl.BlockSpec((1, tq, C_WIDTH), lambda i, j: (i, j, 0)),
        scratch_shapes=[
            pltpu.VMEM((t, tq), jnp.int32),
            pltpu.VMEM((1, tq), jnp.int32),
            pltpu.VMEM((C_HEADS, 1, tq), F32),
            pltpu.VMEM((C_HEADS, SUM_ROWS, tq), F32),
            pltpu.VMEM((C_HEADS, C_HEAD_DIM, tq), F32),
        ],
        compiler_params=pltpu.CompilerParams(
            dimension_semantics=("parallel", "arbitrary"), vmem_limit_bytes=VMEM_LIMIT_BYTES),
        name="c_prompt",
    )(qiT, wiT, ki3, qT, k, vT_all, btiles, _strict_lower(tk))


_NT = (((1,), (1,)), ((), ()))
_NN = (((1,), (0,)), ((), ()))


def _pad_rows(x, rows):
    return jnp.concatenate([x, jnp.zeros((rows - x.shape[0], x.shape[1]), x.dtype)], axis=0)


def _prefix_and_total(n, strict_before):
    i = np.arange(n)
    tri = (i[:, None] < i[None, :]) if strict_before else (i[:, None] > i[None, :])
    return jnp.asarray(np.concatenate([tri, np.ones((n, n), bool)], axis=1).astype(np.float32), MXU_DTYPE)


def _sb_sample_kernel(n_pages, pt_ref, q_ref, kn_ref, vn_ref, hmask_ref, sufx_ref, *rest):
    k_pages, v_pages = rest[:n_pages], rest[n_pages:2 * n_pages]
    o_ref, carry_ref, acc_ref = rest[2 * n_pages:]
    t = q_ref.shape[1]
    rows, tk = SB_HEADS * t, PAGE_SIZE
    hmask = hmask_ref[...]
    q = q_ref[0] * (SB_HEAD_DIM ** -0.5)
    qbd = (jnp.concatenate([q] * SB_HEADS, axis=0) * hmask).astype(MXU_DTYPE)
    carry_ref[...] = jnp.zeros(carry_ref.shape, F32)
    acc_ref[...] = jnp.zeros(acc_ref.shape, F32)

    def step(z, weighted_values, valid):
        sp = jnp.maximum(z, 0.0) + jnp.log(1.0 + jnp.exp(-jnp.abs(z)))
        lr = -sp if valid is None else jnp.where(valid, -sp, 0.0)
        hi, lo = _split_hi_lo(lr)
        both = (jnp.dot(hi, sufx_ref[...], preferred_element_type=F32)
                + jnp.dot(lo, sufx_ref[...], preferred_element_type=F32))
        a = jnp.exp(z - sp + both[:, :tk] + carry_ref[...])
        if valid is not None:
            a = jnp.where(valid, a, 0.0)
        acc_ref[...] += weighted_values(a.astype(MXU_DTYPE))
        carry_ref[...] += both[:, tk:]

    row_t = lax.broadcasted_iota(jnp.int32, (rows, tk), 0) & (t - 1)
    lane = lax.broadcasted_iota(jnp.int32, (rows, tk), 1)
    kn = _pad_rows(kn_ref[0], tk).astype(MXU_DTYPE)
    vn = _pad_rows(vn_ref[0], tk).astype(MXU_DTYPE)
    step(lax.dot_general(qbd, kn, _NT, preferred_element_type=F32),
         lambda a: jnp.dot(a, vn, preferred_element_type=F32), lane < row_t)
    for p in reversed(range(n_pages)):
        @pl.when(jnp.max(carry_ref[...]) > SB_EXIT)
        def _(p=p):
            kT = k_pages[p][...].reshape(SB_WIDTH, tk).astype(MXU_DTYPE)
            vT = v_pages[p][...].reshape(SB_WIDTH, tk).astype(MXU_DTYPE)
            step(jnp.dot(qbd, kT, preferred_element_type=F32),
                 lambda a: lax.dot_general(a, vT, _NT, preferred_element_type=F32), None)

    acc = acc_ref[...] * hmask
    y = acc[0:t]
    for h in range(1, SB_HEADS):
        y = y + acc[h * t:(h + 1) * t]
    o_ref[0] = y


def _feature_major(cache):
    nd = cache.ndim
    return cache.transpose((0, 1) + tuple(range(3, nd)) + (2,))


def _page_specs(n_pages, layer, page_shape):
    zeros = (0,) * len(page_shape)
    return [pl.BlockSpec((None, None) + page_shape, lambda i, pt, p=p: (layer, pt[i, p]) + zeros)
            for p in range(n_pages)]


def _sb_sample(q, k, v, cache_k, cache_v, layer, page_table):
    b, t, _ = q.shape
    assert t & (t - 1) == 0 and t <= PAGE_SIZE
    n_pages = page_table.shape[1]
    rows = SB_HEADS * t
    page = (SB_HEADS, SB_HEAD_DIM, PAGE_SIZE)
    ck, cv = _feature_major(cache_k), _feature_major(cache_v)
    hmask = jnp.asarray((np.arange(rows)[:, None] // t == np.arange(SB_WIDTH)[None, :] // SB_HEAD_DIM)
                        .astype(np.float32))
    tok = pl.BlockSpec((1, t, SB_WIDTH), lambda i, pt: (i, 0, 0))
    const = lambda i, pt: (0, 0)
    return pl.pallas_call(
        functools.partial(_sb_sample_kernel, n_pages),
        out_shape=jax.ShapeDtypeStruct((b, t, SB_WIDTH), F32),
        grid_spec=pltpu.PrefetchScalarGridSpec(
            num_scalar_prefetch=1,
            grid=(b,),
            in_specs=[tok, tok, tok,
                      pl.BlockSpec((rows, SB_WIDTH), const),
                      pl.BlockSpec((PAGE_SIZE, 2 * PAGE_SIZE), const)]
            + _page_specs(n_pages, layer, page) + _page_specs(n_pages, layer, page),
            out_specs=tok,
            scratch_shapes=[pltpu.VMEM((rows, PAGE_SIZE), F32), pltpu.VMEM((rows, SB_WIDTH), F32)],
        ),
        compiler_params=pltpu.CompilerParams(
            dimension_semantics=("arbitrary",), vmem_limit_bytes=VMEM_LIMIT_BYTES),
        name="sb_sample",
    )(page_table, q, k, v, hmask, _prefix_and_total(PAGE_SIZE, False),
      *([ck] * n_pages), *([cv] * n_pages))


def _c_sample_kernel(n_pages, topk, pt_ref, qi_ref, w_ref, q_ref, kin_ref, kn_ref, vn_ref,
                     sbias_ref, prex_ref, *rest):
    idx_pages, k_pages, v_pages = rest[:n_pages], rest[n_pages:2 * n_pages], rest[2 * n_pages:3 * n_pages]
    o_ref, sc_ref, lg_ref = rest[3 * n_pages:]
    t = kin_ref.shape[1]
    tk = PAGE_SIZE
    nb = n_pages + 1
    ih = IDX_HEADS * t
    rows = C_HEADS * t
    row8 = lax.broadcasted_iota(jnp.int32, (t, tk), 0)
    lane8 = lax.broadcasted_iota(jnp.int32, (t, tk), 1)
    new_valid = lane8 <= row8
    qi_cat = qi_ref[0]
    w = w_ref[0] * (IDX_HEADS ** -0.5)

    def scores(idx_blk, valid):
        dims = _NT if valid is not None else _NN
        hi, lo = _split_hi_lo(idx_blk)
        s = lax.dot_general(qi_cat, hi, dims, preferred_element_type=F32)
        s = s[:ih] + s[ih:] + lax.dot_general(qi_cat[:ih], lo, dims, preferred_element_type=F32)
        r = jnp.maximum(s, 0.0) * w
        sc = r[0:t]
        for hh in range(1, IDX_HEADS):
            sc = sc + r[hh * t:(hh + 1) * t]
        if valid is not None:
            sc = jnp.where(valid, sc, -jnp.inf)
        return _sortable(sc)

    for p in range(n_pages):
        sc_ref[:, p * tk:(p + 1) * tk] = scores(idx_pages[p][...], None)
    sc_ref[:, n_pages * tk:] = scores(_pad_rows(kin_ref[0], tk), new_valid)

    def count(pred, thr):
        return jnp.sum(jnp.where(pred(sc_ref[...], thr), 1.0, 0.0), axis=1, keepdims=True)

    def bit_step(i, t_u):
        cand_u = t_u | lax.shift_left(jnp.int32(1), 31 - i)
        cnt = count(lambda x, thr: x >= thr, cand_u ^ INT_MIN)
        return jnp.where(cnt >= topk, cand_u, t_u)

    thr = lax.fori_loop(0, 32, bit_step, jnp.zeros((t, 1), jnp.int32)) ^ INT_MIN
    need = topk - count(lambda x, th: x > th, thr)

    seen = jnp.zeros((t, tk), F32)
    for blk in range(nb):
        x = sc_ref[:, blk * tk:(blk + 1) * tk]
        eq = jnp.where(x == thr, 1.0, 0.0)
        both = jnp.dot(eq.astype(MXU_DTYPE), prex_ref[...], preferred_element_type=F32)
        tie = jnp.where(both[:, :tk] + seen < need, eq, 0.0)
        sel = jnp.where(x > thr, 1.0, tie)
        if blk == n_pages:
            sel = jnp.where(new_valid, sel, 0.0)
        sc_ref[:, blk * tk:(blk + 1) * tk] = sel.astype(jnp.int32)
        seen = seen + both[:, tk:]

    q_rows = q_ref[0]
    grp = C_GROUP * t
    hd = C_HEAD_DIM

    def kv_block(pages, new_ref, blk, n):
        if blk < n_pages:
            return pages[blk][n].astype(MXU_DTYPE)
        return _pad_rows(new_ref[0][:, n * hd:(n + 1) * hd], tk).astype(MXU_DTYPE)

    m = jnp.full((rows, tk), NEG, F32)
    for blk in range(nb):
        bias = sbias_ref[2] if blk < n_pages - 1 else sbias_ref[blk - (n_pages - 1)]
        lg = jnp.concatenate(
            [lax.dot_general(q_rows[n * grp:(n + 1) * grp], kv_block(k_pages, kn_ref, blk, n),
                             _NN if blk < n_pages else _NT, preferred_element_type=F32)
             for n in range(C_KV_HEADS)], axis=0) + bias
        sel = jnp.concatenate([sc_ref[:, blk * tk:(blk + 1) * tk]] * C_HEADS, axis=0) != 0
        lg = jnp.where(sel, lg, NEG)
        lg_ref[:, blk * tk:(blk + 1) * tk] = lg
        m = jnp.maximum(m, lg)
    m_row = jnp.max(m, axis=1, keepdims=True)
    lsum = jnp.zeros((rows, tk), F32)
    accs = [jnp.zeros((grp, hd), F32) for _ in range(C_KV_HEADS)]
    for blk in range(nb):
        p = jnp.exp(lg_ref[:, blk * tk:(blk + 1) * tk] - m_row)
        lsum = lsum + p
        p = p.astype(MXU_DTYPE)
        accs = [acc + lax.dot_general(p[n * grp:(n + 1) * grp], kv_block(v_pages, vn_ref, blk, n),
                                      _NT if blk < n_pages else _NN, preferred_element_type=F32)
                for n, acc in enumerate(accs)]
    l_row = jnp.sum(lsum, axis=1, keepdims=True)
    for n, acc in enumerate(accs):
        o = acc / l_row[n * grp:(n + 1) * grp]
        for g in range(C_GROUP):
            o_ref[0, n * C_GROUP + g] = o[g * t:(g + 1) * t]


def _c_sample(q, k, v, qi, ki, wi, cache_k, cache_v, cache_idx, layer, page_table, sbias):
    b, t, _ = q.shape
    n_pages = page_table.shape[1]
    topk = min(TOPK_MAX, (n_pages * PAGE_SIZE + t) // 4)
    rows = C_HEADS * t
    ih = IDX_HEADS * t
    assert rows == ATT_TILE and t <= PAGE_SIZE
    q_rows = ((q * C_HEAD_DIM ** -0.5).reshape(b, t, C_HEADS, C_HEAD_DIM).transpose(0, 2, 1, 3)
              .reshape(b, rows, C_HEAD_DIM).astype(MXU_DTYPE))
    qi_rows = ((qi * IDX_DIM ** -0.5).reshape(b, t, IDX_HEADS, IDX_DIM).transpose(0, 2, 1, 3)
               .reshape(b, ih, IDX_DIM))
    qi_cat = jnp.concatenate(_split_hi_lo(qi_rows), axis=1)
    w_col = wi.transpose(0, 2, 1).reshape(b, ih, 1)
    kv_page = (C_KV_HEADS, C_HEAD_DIM, PAGE_SIZE)
    per_seq = lambda r, c: pl.BlockSpec((1, r, c), lambda i, pt: (i, 0, 0))
    const2 = lambda i, pt: (0, 0)
    o = pl.pallas_call(
        functools.partial(_c_sample_kernel, n_pages, topk),
        out_shape=jax.ShapeDtypeStruct((b, C_HEADS, t, C_HEAD_DIM), F32),
        grid_spec=pltpu.PrefetchScalarGridSpec(
            num_scalar_prefetch=1,
            grid=(b,),
            in_specs=[per_seq(2 * ih, IDX_DIM), per_seq(ih, 1), per_seq(rows, C_HEAD_DIM),
                      per_seq(t, IDX_DIM), per_seq(t, C_KV_WIDTH), per_seq(t, C_KV_WIDTH),
                      pl.BlockSpec((3, ATT_TILE, ATT_TILE), lambda i, pt: (0, 0, 0)),
                      pl.BlockSpec((PAGE_SIZE, 2 * PAGE_SIZE), const2)]
            + _page_specs(n_pages, layer, (IDX_DIM, PAGE_SIZE)) + _page_specs(n_pages, layer, kv_page)
            + _page_specs(n_pages, layer, kv_page),
            out_specs=pl.BlockSpec((1, C_HEADS, t, C_HEAD_DIM), lambda i, pt: (i, 0, 0, 0)),
            scratch_shapes=[pltpu.VMEM((t, (n_pages + 1) * PAGE_SIZE), jnp.int32),
                            pltpu.VMEM((rows, (n_pages + 1) * PAGE_SIZE), F32)],
        ),
        compiler_params=pltpu.CompilerParams(
            dimension_semantics=("arbitrary",), vmem_limit_bytes=VMEM_LIMIT_BYTES),
        name="c_sample",
    )(page_table, qi_cat, w_col, q_rows, ki, k, v, sbias, _prefix_and_total(PAGE_SIZE, True),
      *([_feature_major(cache_idx)] * n_pages), *([_feature_major(cache_k)] * n_pages),
      *([_feature_major(cache_v)] * n_pages))
    return o.transpose(0, 2, 1, 3).reshape(b, t, C_WIDTH)


POOL_HALO = POOL_BUF + 1


def _pool_tile(ext, first_pos, w_ref, scale_ref):
    t = ext.shape[0] - POOL_HALO
    pos1 = lax.broadcasted_iota(jnp.int32, (t, POOL_GROUP), 0) + (first_pos + 1)
    outs = []
    for g, w in enumerate(POOL_WINDOWS):
        lanes = slice(g * POOL_GROUP, (g + 1) * POOL_GROUP)
        e = ext[:, lanes]
        s, span = e, 1
        while span < w:
            s = s + pltpu.roll(s, span, 0)
            span *= 2
        cnt = jnp.minimum(pos1, w).astype(F32)
        d = s[POOL_HALO:] / cnt - e[POOL_HALO:]
        y = jnp.dot(d.astype(MXU_DTYPE), w_ref[g], preferred_element_type=F32)
        outs.append(y * scale_ref[:, lanes])
    return jnp.concatenate(outs, axis=1)


def _pool_prompt_kernel(u_ref, halo_ref, w_ref, scale_ref, o_ref):
    i = pl.program_id(1)
    halo = jnp.where(i > 0, halo_ref[...], 0.0)
    ext = jnp.concatenate([halo, u_ref[...]], axis=0)
    o_ref[...] = _pool_tile(ext, i * u_ref.shape[0], w_ref, scale_ref)


def _pool_prompt(u, w_pool, pool_scale, batch, seq):
    assert max(POOL_WINDOWS) <= POOL_HALO and TOKEN_TILE % POOL_HALO == 0
    nt = seq // TOKEN_TILE
    per_tile = TOKEN_TILE // POOL_HALO
    return pl.pallas_call(
        _pool_prompt_kernel,
        out_shape=jax.ShapeDtypeStruct(u.shape, F32),
        grid=(batch, nt),
        in_specs=[
            pl.BlockSpec((TOKEN_TILE, POOL_WIDTH), lambda b, i: (b * nt + i, 0)),
            pl.BlockSpec((POOL_HALO, POOL_WIDTH),
                         lambda b, i: (jnp.maximum((b * nt + i) * per_tile - 1, 0), 0)),
            pl.BlockSpec(w_pool.shape, lambda b, i: (0, 0, 0)),
            pl.BlockSpec((1, POOL_WIDTH), lambda b, i: (0, 0)),
        ],
        out_specs=pl.BlockSpec((TOKEN_TILE, POOL_WIDTH), lambda b, i: (b * nt + i, 0)),
        compiler_params=pltpu.CompilerParams(
            dimension_semantics=("parallel", "arbitrary"), vmem_limit_bytes=VMEM_LIMIT_BYTES),
        name="pool_prompt",
    )(u, u, w_pool.astype(MXU_DTYPE), pool_scale[None])


POOL_SAMPLE_GROUP = 8


def _pool_sample_kernel(first_pos, ext_ref, w_ref, scale_ref, o_ref):
    for s in range(ext_ref.shape[0]):
        o_ref[s] = _pool_tile(ext_ref[s], first_pos, w_ref, scale_ref)


def _pool_sample(u_ext, first_pos, w_pool, pool_scale):
    n, rows, _ = u_ext.shape
    t = rows - POOL_HALO
    grp = POOL_SAMPLE_GROUP
    return pl.pallas_call(
        functools.partial(_pool_sample_kernel, first_pos),
        out_shape=jax.ShapeDtypeStruct((n, t, POOL_WIDTH), F32),
        grid=(n // grp,),
        in_specs=[
            pl.BlockSpec((grp, rows, POOL_WIDTH), lambda i: (i, 0, 0)),
            pl.BlockSpec(w_pool.shape, lambda i: (0, 0, 0)),
            pl.BlockSpec((1, POOL_WIDTH), lambda i: (0, 0)),
        ],
        out_specs=pl.BlockSpec((grp, t, POOL_WIDTH), lambda i: (i, 0, 0)),
        compiler_params=pltpu.CompilerParams(
            dimension_semantics=("parallel",), vmem_limit_bytes=VMEM_LIMIT_BYTES),
        name="pool_sample",
    )(u_ext, w_pool.astype(MXU_DTYPE), pool_scale[None])


def _segment_mean_matrix(width):
    seg = np.arange(width) // C_HEAD_DIM
    return jnp.asarray((seg[:, None] == seg[None, :]).astype(np.float32) / C_HEAD_DIM, MXU_DTYPE)


def kernel(x_prompt, x_sample, state_pool, cache_b_k, cache_b_v, cache_c_k, cache_c_v, cache_c_idx, page_table,
           g_ffn, w_ffn_gate, w_ffn_up, w_ffn_down, g_mix, w_in_ab, w_pool, pool_scale, w_out_ab,
           w_in_c, g_q, g_k, w_out_c, rel_bias):
    bp, tp, _ = x_prompt.shape
    bs, ts, _ = x_sample.shape
    mp, ms = bp * tp, bs * ts
    n_even, n_odd = (DEPTH + 1) // 2, DEPTH // 2
    past = page_table.shape[1] * PAGE_SIZE
    x = jnp.concatenate([x_prompt.reshape(mp, D_MODEL), x_sample.reshape(ms, D_MODEL)], axis=0)

    wg = w_ffn_gate.astype(MXU_DTYPE)
    wu = w_ffn_up.astype(MXU_DTYPE)
    wd = w_ffn_down.astype(MXU_DTYPE)
    w_ab = w_in_ab.astype(MXU_DTYPE)
    w_oab = w_out_ab.astype(MXU_DTYPE)
    w_c = jnp.pad(w_in_c, ((0, 0), (0, 0), (0, C_IN_PAD - C_IN))).astype(MXU_DTYPE)
    w_oc = w_out_c.astype(MXU_DTYPE)
    segq = _segment_mean_matrix(C_WIDTH)
    segk = _segment_mean_matrix(C_KV_WIDTH)
    btiles, sbias = _bias_tiles(rel_bias, ts)

    pool_p, pool_s, kbs, vbs, kcs, vcs, ics = [], [], [], [], [], [], []
    kv_b = kv_c = None
    for layer in range(DEPTH):
        j = layer // 2
        x = _ffn(x, g_ffn[layer, 0][None], wg[layer, 0], wu[layer, 0], wd[layer, 0])
        g = g_mix[layer][None]
        if layer % 2 == 0:
            u_p, qT_p, k_p, *kv_b = _proj_ab_prompt(x, g, w_ab[j], bp, tp, j, n_even, kv_b)
            u_s, q_s, k_s, v_s = (a.reshape(bs, ts, -1) for a in _proj_ab_sample(x, g, w_ab[j], mp, ms))
            u_ext = jnp.concatenate([jnp.zeros((bs, POOL_HALO - POOL_BUF, POOL_WIDTH), F32),
                                     state_pool[j], u_s], axis=1)
            parts_p = (_pool_prompt(u_p, w_pool[j], pool_scale[j], bp, tp),
                       _sb_prompt(qT_p, k_p, kv_b[1], j).reshape(mp, SB_WIDTH))
            parts_s = (_pool_sample(u_ext, past, w_pool[j], pool_scale[j]).reshape(ms, POOL_WIDTH),
                       _sb_sample(q_s, k_s, v_s, cache_b_k, cache_b_v, j, page_table).reshape(ms, SB_WIDTH))
            pool_p.append(u_p.reshape(bp, tp, POOL_WIDTH)[:, tp - POOL_BUF:])
            pool_s.append(u_ext[:, -POOL_BUF:])
            kbs.append(k_s.reshape(bs, ts, SB_HEADS, SB_HEAD_DIM))
            vbs.append(v_s.reshape(bs, ts, SB_HEADS, SB_HEAD_DIM))
            w_out = w_oab[j]
        else:
            gq = jnp.tile(g_q[j], C_HEADS)[None]
            gk = jnp.tile(g_k[j], C_KV_HEADS)[None]
            qT_p, k_p, qiT_p, wiT_p, ki3_p, *kv_c = _proj_c_prompt(
                x, g, w_c[j], gq, gk, segq, segk, bp, tp, j, n_odd, kv_c)
            q_s, k_s, v_s, qi_s, tail_s = (
                a.reshape(bs, ts, -1) for a in _proj_c_sample(x, g, w_c[j], gq, gk, segq, segk, mp, ms))
            ki_s = tail_s[..., :IDX_DIM]
            wi_s = tail_s[..., IDX_DIM:IDX_DIM + IDX_HEADS]
            parts_p = (_c_prompt(qT_p, k_p, kv_c[1], j, qiT_p, wiT_p, ki3_p, btiles).reshape(mp, C_WIDTH),)
            parts_s = (_c_sample(q_s, k_s, v_s, qi_s, ki_s, wi_s, cache_c_k, cache_c_v, cache_c_idx,
                                 j, page_table, sbias).reshape(ms, C_WIDTH),)
            kcs.append(k_s.reshape(bs, ts, C_KV_HEADS, C_HEAD_DIM))
            vcs.append(v_s.reshape(bs, ts, C_KV_HEADS, C_HEAD_DIM))
            ics.append(ki_s)
            w_out = w_oc[j]
        x_new = _out_proj(x, parts_p, w_out, 0)
        x = _out_proj(x, parts_s, w_out, mp, prev=x_new)
        x = _ffn(x, g_ffn[layer, 1][None], wg[layer, 1], wu[layer, 1], wd[layer, 1])

    def token_major(aT, heads):
        n, b, width, t = aT.shape
        return aT.reshape(n, b, heads, width // heads, t).transpose(0, 1, 4, 2, 3)

    kbT, vbT = kv_b
    kcT, vcT, icT = kv_c
    return (x[:mp].reshape(bp, tp, D_MODEL), x[mp:].reshape(bs, ts, D_MODEL),
            jnp.stack(pool_p), jnp.stack(pool_s),
            token_major(kbT, SB_HEADS), token_major(vbT, SB_HEADS), jnp.stack(kbs), jnp.stack(vbs),
            token_major(kcT, C_KV_HEADS), token_major(vcT, C_KV_HEADS), icT.transpose(0, 1, 3, 2),
            jnp.stack(kcs), jnp.stack(vcs), jnp.stack(ics))
```

```python
import functools
import math

import jax
import jax.numpy as jnp
import numpy as np
from jax import lax
from jax.experimental import pallas as pl
from jax.experimental.pallas import tpu as pltpu

F32 = jnp.float32
BF16 = jnp.bfloat16
MXU_DTYPE = BF16

D_MODEL = 1024
DEPTH = 4
D_FF = 2816
POOL_WINDOWS = (2, 4, 8, 16)
POOL_WIDTH = 512
POOL_GROUP = 128
POOL_BUF = 15
SB_HEADS = 8
SB_HEAD_DIM = 64
SB_WIDTH = 512
AB_IN = POOL_WIDTH + 3 * SB_WIDTH
C_HEAD_DIM = 64
C_HEADS = 16
C_KV_HEADS = 4
C_GROUP = 4
C_WIDTH = 1024
C_KV_WIDTH = 256
IDX_HEADS = 8
IDX_DIM = 64
TOPK_MAX = 256
C_IN = 2120
C_IN_PAD = 2176
REL_BUCKETS = 32
REL_MAX_EXACT = 16
REL_MAX_DIST = 128
Q_BLOCK = 128
PAGE_SIZE = 128
EPS = 1e-6
NEG = -1e30

VMEM_LIMIT_BYTES = 56 * 1024 * 1024
FF_CHUNK = 256
TOKEN_TILE = 512


def _rms(x, g):
    ms = jnp.mean(x * x, axis=-1, keepdims=True)
    return x * lax.rsqrt(ms + EPS) * g


def _ffn_kernel(x_ref, g_ref, wg_ref, wu_ref, wd_ref, o_ref):
    x = x_ref[...]
    h = _rms(x, g_ref[...]).astype(MXU_DTYPE)
    acc = x
    for c in range(D_FF // FF_CHUNK):
        sl = slice(c * FF_CHUNK, (c + 1) * FF_CHUNK)
        gate = jnp.dot(h, wg_ref[:, sl], preferred_element_type=F32)
        up = jnp.dot(h, wu_ref[:, sl], preferred_element_type=F32)
        act = (0.5 * gate * jax.nn.sigmoid(gate) * up).astype(MXU_DTYPE)
        acc = acc + jnp.dot(act, wd_ref[sl, :], preferred_element_type=F32)
    o_ref[...] = acc


def _ffn(x, g, wg, wu, wd):
    m = x.shape[0]
    resident = dict(pipeline_mode=pl.Buffered(1))
    return pl.pallas_call(
        _ffn_kernel,
        out_shape=jax.ShapeDtypeStruct((m, D_MODEL), F32),
        grid=(m // TOKEN_TILE,),
        in_specs=[
            pl.BlockSpec((TOKEN_TILE, D_MODEL), lambda i: (i, 0)),
            pl.BlockSpec((1, D_MODEL), lambda i: (0, 0)),
            pl.BlockSpec((D_MODEL, D_FF), lambda i: (0, 0), **resident),
            pl.BlockSpec((D_MODEL, D_FF), lambda i: (0, 0), **resident),
            pl.BlockSpec((D_FF, D_MODEL), lambda i: (0, 0), **resident),
        ],
        out_specs=pl.BlockSpec((TOKEN_TILE, D_MODEL), lambda i: (i, 0)),
        compiler_params=pltpu.CompilerParams(
            dimension_semantics=("parallel",), vmem_limit_bytes=VMEM_LIMIT_BYTES),
        name="ffn_half",
    )(x, g, wg, wu, wd)


def _proj_ab_sample_kernel(x_ref, g_ref, w_ref, u_ref, q_ref, k_ref, v_ref):
    h = _rms(x_ref[...], g_ref[...]).astype(MXU_DTYPE)
    p = jnp.dot(h, w_ref[...], preferred_element_type=F32)
    u_ref[...] = p[:, :POOL_WIDTH]
    q_ref[...] = p[:, POOL_WIDTH:POOL_WIDTH + SB_WIDTH]
    k_ref[...] = p[:, POOL_WIDTH + SB_WIDTH:POOL_WIDTH + 2 * SB_WIDTH]
    v_ref[...] = p[:, POOL_WIDTH + 2 * SB_WIDTH:]


def _proj_ab_sample(x, g, w, row0, rows):
    first = row0 // TOKEN_TILE
    out = jax.ShapeDtypeStruct((rows, SB_WIDTH), F32)
    spec = pl.BlockSpec((TOKEN_TILE, SB_WIDTH), lambda i: (i, 0))
    return pl.pallas_call(
        _proj_ab_sample_kernel,
        out_shape=(out, out, out, out),
        grid=(rows // TOKEN_TILE,),
        in_specs=[
            pl.BlockSpec((TOKEN_TILE, D_MODEL), lambda i: (first + i, 0)),
            pl.BlockSpec((1, D_MODEL), lambda i: (0, 0)),
            pl.BlockSpec((D_MODEL, AB_IN), lambda i: (0, 0), pipeline_mode=pl.Buffered(1)),
        ],
        out_specs=(spec, spec, spec, spec),
        compiler_params=pltpu.CompilerParams(
            dimension_semantics=("parallel",), vmem_limit_bytes=VMEM_LIMIT_BYTES),
        name="proj_ab_sample",
    )(x, g, w)


def _proj_ab_prompt_kernel(x_ref, g_ref, w_ref, *rest):
    u_ref, qT_ref, kn_ref, kT_ref, vT_ref = rest[-5:]
    h = _rms(x_ref[...], g_ref[...]).astype(MXU_DTYPE)
    p = jnp.dot(h, w_ref[...], preferred_element_type=F32)
    u_ref[...] = p[:, :POOL_WIDTH]
    q = p[:, POOL_WIDTH:POOL_WIDTH + SB_WIDTH] * (SB_HEAD_DIM ** -0.5)
    k = p[:, POOL_WIDTH + SB_WIDTH:POOL_WIDTH + 2 * SB_WIDTH]
    v = p[:, POOL_WIDTH + 2 * SB_WIDTH:]
    qT_ref[0] = q.T.astype(MXU_DTYPE)
    kn_ref[0] = k.astype(MXU_DTYPE)
    kT_ref[0] = k.T
    vT_ref[0] = v.T


def _slot_alias(prev, first_input, first_output):
    if prev is None:
        return [], {}
    specs = [pl.BlockSpec(memory_space=pl.ANY)] * len(prev)
    return specs, {first_input + n: first_output + n for n in range(len(prev))}


def _proj_ab_prompt(x, g, w, batch, seq, slot, n_slots, prev):
    nt = seq // TOKEN_TILE
    alias_specs, aliases = _slot_alias(prev, 3, 3)
    stacked = jax.ShapeDtypeStruct((n_slots, batch, SB_WIDTH, seq), F32)
    stacked_spec = pl.BlockSpec((None, 1, SB_WIDTH, TOKEN_TILE), lambda b, i: (slot, b, 0, i))
    return pl.pallas_call(
        _proj_ab_prompt_kernel,
        out_shape=(jax.ShapeDtypeStruct((batch * seq, POOL_WIDTH), F32),
                   jax.ShapeDtypeStruct((batch, SB_WIDTH, seq), MXU_DTYPE),
                   jax.ShapeDtypeStruct((batch, seq, SB_WIDTH), MXU_DTYPE),
                   stacked, stacked),
        grid=(batch, nt),
        in_specs=[
            pl.BlockSpec((TOKEN_TILE, D_MODEL), lambda b, i: (b * nt + i, 0)),
            pl.BlockSpec((1, D_MODEL), lambda b, i: (0, 0)),
            pl.BlockSpec((D_MODEL, AB_IN), lambda b, i: (0, 0), pipeline_mode=pl.Buffered(1)),
        ] + alias_specs,
        out_specs=(pl.BlockSpec((TOKEN_TILE, POOL_WIDTH), lambda b, i: (b * nt + i, 0)),
                   pl.BlockSpec((1, SB_WIDTH, TOKEN_TILE), lambda b, i: (b, 0, i)),
                   pl.BlockSpec((1, TOKEN_TILE, SB_WIDTH), lambda b, i: (b, i, 0)),
                   stacked_spec, stacked_spec),
        input_output_aliases=aliases,
        compiler_params=pltpu.CompilerParams(
            dimension_semantics=("parallel", "parallel"), vmem_limit_bytes=VMEM_LIMIT_BYTES),
        name="proj_ab_prompt",
    )(x, g, w, *(prev or ()))


def _split_hi_lo(x):
    hi = x.astype(MXU_DTYPE)
    lo = (x - hi.astype(F32)).astype(MXU_DTYPE)
    return hi, lo


def _head_rms(x, seg_mean_ref, gain):
    hi, lo = _split_hi_lo(x * x)
    ms = (jnp.dot(hi, seg_mean_ref[...], preferred_element_type=F32)
          + jnp.dot(lo, seg_mean_ref[...], preferred_element_type=F32))
    return x * lax.rsqrt(ms + EPS) * gain


C_OFF_K = C_WIDTH
C_OFF_V = C_OFF_K + C_KV_WIDTH
C_OFF_QI = C_OFF_V + C_KV_WIDTH
C_OFF_TAIL = C_OFF_QI + IDX_HEADS * IDX_DIM
C_TAIL = C_IN_PAD - C_OFF_TAIL


def _proj_c_parts(x_ref, g_ref, w_ref, gq_ref, gk_ref, segq_ref, segk_ref):
    h = _rms(x_ref[...], g_ref[...]).astype(MXU_DTYPE)
    p = jnp.dot(h, w_ref[...], preferred_element_type=F32)
    q = _head_rms(p[:, :C_OFF_K], segq_ref, gq_ref[...])
    k = _head_rms(p[:, C_OFF_K:C_OFF_V], segk_ref, gk_ref[...])
    return q, k, p[:, C_OFF_V:C_OFF_QI], p[:, C_OFF_QI:C_OFF_TAIL], p[:, C_OFF_TAIL:]


def _proj_c_sample_kernel(x_ref, g_ref, w_ref, gq_ref, gk_ref, segq_ref, segk_ref,
                          q_ref, k_ref, v_ref, qi_ref, tail_ref):
    q, k, v, qi, tail = _proj_c_parts(x_ref, g_ref, w_ref, gq_ref, gk_ref, segq_ref, segk_ref)
    q_ref[...] = q
    k_ref[...] = k
    v_ref[...] = v
    qi_ref[...] = qi
    tail_ref[...] = tail


def _proj_c_in_specs(row_block):
    const = lambda *_: (0, 0)
    return [
        pl.BlockSpec((TOKEN_TILE, D_MODEL), row_block),
        pl.BlockSpec((1, D_MODEL), const),
        pl.BlockSpec((D_MODEL, C_IN_PAD), const, pipeline_mode=pl.Buffered(1)),
        pl.BlockSpec((1, C_WIDTH), const),
        pl.BlockSpec((1, C_KV_WIDTH), const),
        pl.BlockSpec((C_WIDTH, C_WIDTH), const, pipeline_mode=pl.Buffered(1)),
        pl.BlockSpec((C_KV_WIDTH, C_KV_WIDTH), const, pipeline_mode=pl.Buffered(1)),
    ]


def _proj_c_sample(x, g, w, gq, gk, segq, segk, row0, rows):
    first = row0 // TOKEN_TILE
    widths = (C_WIDTH, C_KV_WIDTH, C_KV_WIDTH, IDX_HEADS * IDX_DIM, C_TAIL)
    return pl.pallas_call(
        _proj_c_sample_kernel,
        out_shape=tuple(jax.ShapeDtypeStruct((rows, n), F32) for n in widths),
        grid=(rows // TOKEN_TILE,),
        in_specs=_proj_c_in_specs(lambda i: (first + i, 0)),
        out_specs=tuple(pl.BlockSpec((TOKEN_TILE, n), lambda i: (i, 0)) for n in widths),
        compiler_params=pltpu.CompilerParams(
            dimension_semantics=("parallel",), vmem_limit_bytes=VMEM_LIMIT_BYTES),
        name="proj_c_sample",
    )(x, g, w, gq, gk, segq, segk)


def _proj_c_prompt_kernel(x_ref, g_ref, w_ref, gq_ref, gk_ref, segq_ref, segk_ref, *rest):
    qT_ref, kn_ref, qiT_ref, wiT_ref, ki3_ref, kT_ref, vT_ref, kiT_ref = rest[-8:]
    q, k, v, qi, tail = _proj_c_parts(x_ref, g_ref, w_ref, gq_ref, gk_ref, segq_ref, segk_ref)
    tm = q.shape[0]
    qT_ref[0] = (q * (C_HEAD_DIM ** -0.5 * LOG2E)).T.astype(MXU_DTYPE).reshape(C_HEADS, C_HEAD_DIM, tm)
    kn_ref[0] = k.astype(MXU_DTYPE)
    kT_ref[0] = k.T
    vT_ref[0] = v.T
    qi_hi, qi_lo = _split_hi_lo((qi * (IDX_DIM ** -0.5)).T)
    for hh in range(IDX_HEADS):
        rows = slice(hh * IDX_DIM, (hh + 1) * IDX_DIM)
        qiT_ref[0, hh, 0:IDX_DIM] = qi_hi[rows]
        qiT_ref[0, hh, IDX_DIM:2 * IDX_DIM] = qi_lo[rows]
        qiT_ref[0, hh, 2 * IDX_DIM:] = qi_hi[rows]
    tail_t = tail.T
    kiT_ref[0] = tail_t[:IDX_DIM]
    wiT_ref[0] = tail_t[IDX_DIM:IDX_DIM + IDX_HEADS]
    ki_hi, ki_lo = _split_hi_lo(tail[:, :IDX_DIM])
    ki3_ref[0] = jnp.concatenate([ki_hi, ki_hi, ki_lo], axis=1)


def _proj_c_prompt(x, g, w, gq, gk, segq, segk, batch, seq, slot, n_slots, prev):
    nt = seq // TOKEN_TILE
    tm = TOKEN_TILE
    alias_specs, aliases = _slot_alias(prev, 7, 5)

    def stacked(width):
        return (jax.ShapeDtypeStruct((n_slots, batch, width, seq), F32),
                pl.BlockSpec((None, 1, width, tm), lambda b, i: (slot, b, 0, i)))

    (kT_s, kT_b), (vT_s, vT_b), (kiT_s, kiT_b) = stacked(C_KV_WIDTH), stacked(C_KV_WIDTH), stacked(IDX_DIM)
    return pl.pallas_call(
        _proj_c_prompt_kernel,
        out_shape=(jax.ShapeDtypeStruct((batch, C_HEADS, C_HEAD_DIM, seq), MXU_DTYPE),
                   jax.ShapeDtypeStruct((batch, seq, C_KV_WIDTH), MXU_DTYPE),
                   jax.ShapeDtypeStruct((batch, IDX_HEADS, 3 * IDX_DIM, seq), MXU_DTYPE),
                   jax.ShapeDtypeStruct((batch, IDX_HEADS, seq), F32),
                   jax.ShapeDtypeStruct((batch, seq, 3 * IDX_DIM), MXU_DTYPE),
                   kT_s, vT_s, kiT_s),
        grid=(batch, nt),
        in_specs=_proj_c_in_specs(lambda b, i: (b * nt + i, 0)) + alias_specs,
        out_specs=(pl.BlockSpec((1, C_HEADS, C_HEAD_DIM, tm), lambda b, i: (b, 0, 0, i)),
                   pl.BlockSpec((1, tm, C_KV_WIDTH), lambda b, i: (b, i, 0)),
                   pl.BlockSpec((1, IDX_HEADS, 3 * IDX_DIM, tm), lambda b, i: (b, 0, 0, i)),
                   pl.BlockSpec((1, IDX_HEADS, tm), lambda b, i: (b, 0, i)),
                   pl.BlockSpec((1, tm, 3 * IDX_DIM), lambda b, i: (b, i, 0)),
                   kT_b, vT_b, kiT_b),
        input_output_aliases=aliases,
        compiler_params=pltpu.CompilerParams(
            dimension_semantics=("parallel", "parallel"), vmem_limit_bytes=VMEM_LIMIT_BYTES),
        name="proj_c_prompt",
    )(x, g, w, gq, gk, segq, segk, *(prev or ()))


def _out_proj_kernel(n_parts, x_ref, *rest):
    parts, w_ref, o_ref = rest[:n_parts], rest[n_parts], rest[-1]
    acc = x_ref[...]
    off = 0
    for part in parts:
        width = part.shape[1]
        acc = acc + jnp.dot(part[...].astype(MXU_DTYPE), w_ref[off:off + width, :],
                            preferred_element_type=F32)
        off += width
    o_ref[...] = acc


def _out_proj(x, parts, w, row0, prev=None):
    rows = parts[0].shape[0]
    first = row0 // TOKEN_TILE
    x_tile = pl.BlockSpec((TOKEN_TILE, D_MODEL), lambda i: (first + i, 0))
    alias_specs, aliases = _slot_alias(None if prev is None else (prev,), 2 + len(parts), 0)
    return pl.pallas_call(
        functools.partial(_out_proj_kernel, len(parts)),
        out_shape=jax.ShapeDtypeStruct(x.shape, F32),
        grid=(rows // TOKEN_TILE,),
        in_specs=[x_tile]
        + [pl.BlockSpec((TOKEN_TILE, p.shape[1]), lambda i: (i, 0)) for p in parts]
        + [pl.BlockSpec((D_MODEL, D_MODEL), lambda i: (0, 0), pipeline_mode=pl.Buffered(1))]
        + alias_specs,
        out_specs=x_tile,
        input_output_aliases=aliases,
        compiler_params=pltpu.CompilerParams(
            dimension_semantics=("parallel",), vmem_limit_bytes=VMEM_LIMIT_BYTES),
        name="out_proj",
    )(x, *parts, w, *(() if prev is None else (prev,)))


ATT_TILE = 128
ATT_TK = 128
ATT_TQ = 256
SB_TK = 256
SB_LOCKSTEP = 2
SB_EXIT = -104.0


def _sb_prompt_kernel(qT_ref, k_ref, vT_ref, upper_ref, o_ref):
    tk, tq = SB_TK, ATT_TQ
    qb = pl.program_id(1)
    row = lax.broadcasted_iota(jnp.int32, (tk, tq), 0)
    lane = lax.broadcasted_iota(jnp.int32, (tk, tq), 1)
    q_idx = qb * tq + lane
    upper = upper_ref[...]
    last_tile = qb * (tq // tk) + (tq // tk - 1)

    def head_step(h, ks, valid, carry, acc):
        hs = slice(h * SB_HEAD_DIM, (h + 1) * SB_HEAD_DIM)
        kb = k_ref[0, pl.ds(ks, tk), hs]
        z = jnp.dot(kb, qT_ref[0, hs, :], preferred_element_type=F32)
        sp = jnp.maximum(z, 0.0) + jnp.log(1.0 + jnp.exp(-jnp.abs(z)))
        lr = jnp.where(valid, -sp, 0.0)
        hi, lo = _split_hi_lo(lr)
        between = (jnp.dot(upper, hi, preferred_element_type=F32)
                   + jnp.dot(upper, lo, preferred_element_type=F32) + carry)
        a = jnp.where(valid, jnp.exp(z - sp + between), 0.0)
        vb = vT_ref[0, hs, pl.ds(ks, tk)].astype(MXU_DTYPE)
        acc = acc + jnp.dot(vb, a.astype(MXU_DTYPE), preferred_element_type=F32)
        return carry + jnp.sum(lr, axis=0, keepdims=True), acc

    for h0 in range(0, SB_HEADS, SB_LOCKSTEP):
        heads = range(h0, h0 + SB_LOCKSTEP)

        def body(state, heads=heads):
            j, _, carries, accs = state
            ks = pl.multiple_of(j * tk, tk)
            valid = (ks + row) < q_idx
            out = [head_step(h, ks, valid, c, a) for h, c, a in zip(heads, carries, accs)]
            carries = tuple(o[0] for o in out)
            top = functools.reduce(jnp.maximum, [jnp.max(c) for c in carries])
            return j - 1, top, carries, tuple(o[1] for o in out)

        def cond(state):
            j, top, _, _ = state
            return jnp.logical_and(j >= 0, top > SB_EXIT)

        init = (last_tile, jnp.float32(0.0),
                tuple(jnp.zeros((1, tq), F32) for _ in heads),
                tuple(jnp.zeros((SB_HEAD_DIM, tq), F32) for _ in heads))
        _, _, _, accs = lax.while_loop(cond, body, init)
        o_ref[0, :, h0 * SB_HEAD_DIM:(h0 + SB_LOCKSTEP) * SB_HEAD_DIM] = jnp.concatenate(accs, axis=0).T


def _strict_upper(n):
    i = np.arange(n)
    return jnp.asarray((i[None, :] > i[:, None]).astype(np.float32), MXU_DTYPE)


def _strict_lower(n):
    i = np.arange(n)
    return jnp.asarray((i[None, :] < i[:, None]).astype(np.float32), MXU_DTYPE)


def _sb_prompt(qT, k, vT_all, slot):
    b, width, t = qT.shape
    assert (SB_LOCKSTEP * SB_HEAD_DIM) % 128 == 0
    return pl.pallas_call(
        _sb_prompt_kernel,
        out_shape=jax.ShapeDtypeStruct((b, t, width), F32),
        grid=(b, t // ATT_TQ),
        in_specs=[
            pl.BlockSpec((1, width, ATT_TQ), lambda i, j: (i, 0, j)),
            pl.BlockSpec((1, t, width), lambda i, j: (i, 0, 0)),
            pl.BlockSpec((None, 1, width, t), lambda i, j: (slot, i, 0, 0)),
            pl.BlockSpec((SB_TK, SB_TK), lambda i, j: (0, 0)),
        ],
        out_specs=pl.BlockSpec((1, ATT_TQ, width), lambda i, j: (i, j, 0)),
        compiler_params=pltpu.CompilerParams(
            dimension_semantics=("parallel", "arbitrary"), vmem_limit_bytes=VMEM_LIMIT_BYTES),
        name="sb_prompt",
    )(qT, k, vT_all, _strict_upper(SB_TK))


INT_MIN = -2 ** 31
COUNT_CHUNK = 4 * ATT_TK
LOG2E = math.log2(math.e)
C_TQ = 256
REL_NEAR_TILES = C_TQ // ATT_TK + 1
REL_LAST_BUCKET_FROM = math.ceil(REL_MAX_EXACT * (REL_MAX_DIST / REL_MAX_EXACT) ** (
    (REL_BUCKETS - 1 - REL_MAX_EXACT) / (REL_BUCKETS - REL_MAX_EXACT)))
assert ATT_TK + 1 >= REL_LAST_BUCKET_FROM
SUM_ROWS = 16


def _bucket_of(dist):
    n = jnp.maximum(dist, 0)
    nf = jnp.maximum(n, 1).astype(F32)
    large = REL_MAX_EXACT + (jnp.log(nf / REL_MAX_EXACT) / math.log(REL_MAX_DIST / REL_MAX_EXACT)
                             * (REL_BUCKETS - REL_MAX_EXACT)).astype(jnp.int32)
    return jnp.where(n < REL_MAX_EXACT, n, jnp.minimum(large, REL_BUCKETS - 1))


def _bias_tile_kernel(dec_seq, relb_ref, o_ref, s_ref):
    krow = lax.broadcasted_iota(jnp.int32, (ATT_TK, C_TQ), 0)
    qlane = lax.broadcasted_iota(jnp.int32, (ATT_TK, C_TQ), 1)
    for c in range(REL_NEAR_TILES):
        bucket = _bucket_of((1 - c) * ATT_TK + qlane - krow)
        for h in range(C_HEADS):
            tile = jnp.zeros((ATT_TK, C_TQ), F32)
            for b in range(REL_BUCKETS):
                tile = jnp.where(bucket == b, relb_ref[b, h], tile)
            o_ref[h, c] = (tile - relb_ref[REL_BUCKETS - 1, h]) * LOG2E
    t = ATT_TILE
    row = lax.broadcasted_iota(jnp.int32, (t, t), 0)
    lane = lax.broadcasted_iota(jnp.int32, (t, t), 1)
    row_head = row >> int(math.log2(dec_seq))
    row_t = row & (dec_seq - 1)
    buckets = (_bucket_of(t + row_t - lane), _bucket_of(row_t - lane),
               jnp.full((t, t), REL_BUCKETS - 1, jnp.int32))
    tiles = [jnp.zeros((t, t), F32) for _ in buckets]
    for b in range(REL_BUCKETS):
        by_head = jnp.zeros((t, t), F32)
        for h in range(C_HEADS):
            by_head = jnp.where(row_head == h, relb_ref[b, h], by_head)
        tiles = [jnp.where(bk == b, by_head, tl) for bk, tl in zip(buckets, tiles)]
    for c, tl in enumerate(tiles):
        s_ref[c] = tl


def _bias_tiles(rel_bias, dec_seq):
    assert C_HEADS * dec_seq == ATT_TILE
    return pl.pallas_call(
        functools.partial(_bias_tile_kernel, dec_seq),
        out_shape=(jax.ShapeDtypeStruct((C_HEADS, REL_NEAR_TILES, ATT_TK, C_TQ), F32),
                   jax.ShapeDtypeStruct((3, ATT_TILE, ATT_TILE), F32)),
        in_specs=[pl.BlockSpec(memory_space=pltpu.SMEM)],
        name="rel_bias_tiles",
    )(rel_bias)


def _sortable(s):
    bits = lax.bitcast_convert_type(s, jnp.int32)
    return bits ^ ((bits >> 31) & 0x7FFFFFFF)


def _c_prompt_kernel(topk, qiT_ref, wiT_ref, ki_ref, qT_ref, k_ref, vT_ref, btile_ref,
                     lower_ref, o_ref, key_ref, thr_ref, m_ref, l_ref, acc_ref):
    tk, tq = ATT_TK, C_TQ
    qb = pl.program_id(1)
    n_tiles = (qb + 1) * (tq // tk)
    n_chunks = (n_tiles * tk + COUNT_CHUNK - 1) // COUNT_CHUNK
    row = lax.broadcasted_iota(jnp.int32, (tk, tq), 0)
    lane = lax.broadcasted_iota(jnp.int32, (tk, tq), 1)
    q_idx = qb * tq + lane
    w = wiT_ref[0] * (IDX_HEADS ** -0.5)

    def score_block(j, _):
        ks = pl.multiple_of(j * tk, tk)
        kib = ki_ref[0, pl.ds(ks, tk), :]
        s = jnp.zeros((tk, tq), F32)
        for hh in range(IDX_HEADS):
            d = jnp.dot(kib, qiT_ref[0, hh], preferred_element_type=F32)
            s = s + jnp.maximum(d, 0.0) * w[hh:hh + 1, :]
        s = jnp.where(ks + row <= q_idx, s, -jnp.inf)
        key_ref[pl.ds(ks, tk), :] = _sortable(s)
        return 0

    lax.fori_loop(0, n_chunks * (COUNT_CHUNK // tk), score_block, 0)

    def count(pred, thr):
        def chunk(c, cnt):
            base = pl.multiple_of(c * COUNT_CHUNK, COUNT_CHUNK)
            ind = jnp.where(pred(key_ref[pl.ds(base, COUNT_CHUNK), :], thr), 1, 0)
            return cnt + jnp.sum(ind.reshape(COUNT_CHUNK // 8, 8, tq), axis=0)
        cnt = lax.fori_loop(0, n_chunks, chunk, jnp.zeros((8, tq), jnp.int32))
        return jnp.sum(cnt, axis=0, keepdims=True)

    def bit_step(i, t_u):
        cand_u = t_u | lax.shift_left(jnp.int32(1), 31 - i)
        cnt = count(lambda x, thr: x >= thr, cand_u ^ INT_MIN)
        return jnp.where(cnt >= topk, cand_u, t_u)

    thr = lax.fori_loop(0, 32, bit_step, jnp.zeros((1, tq), jnp.int32)) ^ INT_MIN
    n_ge = count(lambda x, t: x >= t, thr)
    thr_ref[...] = thr

    @pl.when(jnp.max(jnp.abs(n_ge - topk)) > 0)
    def _():
        need = (topk - count(lambda x, t: x > t, thr)).astype(F32)

        def select_block(j, seen):
            ks = pl.multiple_of(j * tk, tk)
            blk = key_ref[pl.ds(ks, tk), :]
            eq = jnp.where(blk == thr, 1.0, 0.0)
            rank = jnp.dot(lower_ref[...], eq.astype(MXU_DTYPE), preferred_element_type=F32) + seen
            tie = jnp.where(rank < need, eq, 0.0)
            sel = jnp.where(blk > thr, 1.0, tie)
            sel = jnp.where(ks + row <= q_idx, sel, 0.0)
            key_ref[pl.ds(ks, tk), :] = sel.astype(jnp.int32)
            return seen + jnp.sum(eq, axis=0, keepdims=True)

        lax.fori_loop(0, n_tiles, select_block, jnp.zeros((1, tq), F32))
        thr_ref[...] = jnp.ones((1, tq), jnp.int32)

    m_ref[...] = jnp.full(m_ref.shape, NEG, F32)
    l_ref[...] = jnp.zeros(l_ref.shape, F32)
    acc_ref[...] = jnp.zeros(acc_ref.shape, F32)
    ones = jnp.ones((SUM_ROWS, tk), MXU_DTYPE)

    def attend(j, near):
        ks = pl.multiple_of(j * tk, tk)
        sel = key_ref[pl.ds(ks, tk), :] >= thr_ref[...]
        for n in range(C_KV_HEADS):
            ns = slice(n * C_HEAD_DIM, (n + 1) * C_HEAD_DIM)
            kb = k_ref[0, pl.ds(ks, tk), ns]
            vb = vT_ref[0, ns, pl.ds(ks, tk)].astype(MXU_DTYPE)
            for g in range(C_GROUP):
                h = n * C_GROUP + g
                lg = jnp.dot(kb, qT_ref[0, h], preferred_element_type=F32)
                if near is not None:
                    lg = lg + btile_ref[h, near]
                lg = jnp.where(sel, lg, NEG)
                m_old = m_ref[h]
                m_new = jnp.maximum(m_old, jnp.max(lg, axis=0, keepdims=True))
                p = jnp.exp2(lg - m_new).astype(MXU_DTYPE)
                alpha = jnp.exp2(m_old - m_new)
                l_ref[h] = alpha * l_ref[h] + jnp.dot(ones, p, preferred_element_type=F32)
                acc_ref[h] = alpha * acc_ref[h] + jnp.dot(vb, p, preferred_element_type=F32)
                m_ref[h] = m_new

    def far_tile(j, _):
        attend(j, None)
        return 0

    first_near = n_tiles - REL_NEAR_TILES
    lax.fori_loop(0, jnp.maximum(first_near, 0), far_tile, 0)
    for c in range(REL_NEAR_TILES):
        if c == 0:
            pl.when(first_near >= 0)(functools.partial(attend, first_near, 0))
        else:
            attend(first_near + c, c)
    for h in range(0, C_HEADS, 2):
        pair = jnp.concatenate([acc_ref[h] / l_ref[h][0:1], acc_ref[h + 1] / l_ref[h + 1][0:1]], axis=0)
        o_ref[0, :, h * C_HEAD_DIM:(h + 2) * C_HEAD_DIM] = pair.T


def _c_prompt(qT, k, vT_all, slot, qiT, wiT, ki3, btiles):
    b, _, _, t = qT.shape
    topk = min(TOPK_MAX, t // 4)
    assert t % COUNT_CHUNK == 0 and COUNT_CHUNK > topk and t % C_TQ == 0
    tq, tk = C_TQ, ATT_TK
    return pl.pallas_call(
        functools.partial(_c_prompt_kernel, topk),
        out_shape=jax.ShapeDtypeStruct((b, t, C_WIDTH), F32),
        grid=(b, t // tq),
        in_specs=[
            pl.BlockSpec((1, IDX_HEADS, 3 * IDX_DIM, tq), lambda i, j: (i, 0, 0, j)),
            pl.BlockSpec((1, IDX_HEADS, tq), lambda i, j: (i, 0, j)),
            pl.BlockSpec((1, t, 3 * IDX_DIM), lambda i, j: (i, 0, 0)),
            pl.BlockSpec((1, C_HEADS, C_HEAD_DIM, tq), lambda i, j: (i, 0, 0, j)),
            pl.BlockSpec((1, t, C_KV_WIDTH), lambda i, j: (i, 0, 0)),
            pl.BlockSpec((None, 1, C_KV_WIDTH, t), lambda i, j: (slot, i, 0, 0)),
            pl.BlockSpec((C_HEADS, REL_NEAR_TILES, tk, tq), lambda i, j: (0, 0, 0, 0),
                         pipeline_mode=pl.Buffered(1)),
            pl.BlockSpec((tk, tk), lambda i, j: (0, 0)),
        ],
        out_specs=pl.BlockSpec((1, tq, C_WIDTH), lambda i, j: (i, j, 0)),
        scratch_shapes=[
            pltpu.VMEM((t, tq), jnp.int32),
            pltpu.VMEM((1, tq), jnp.int32),
            pltpu.VMEM((C_HEADS, 1, tq), F32),
            pltpu.VMEM((C_HEADS, SUM_ROWS, tq), F32),
            pltpu.VMEM((C_HEADS, C_HEAD_DIM, tq), F32),
        ],
        compiler_params=pltpu.CompilerParams(
            dimension_semantics=("parallel", "arbitrary"), vmem_limit_bytes=VMEM_LIMIT_BYTES),
        name="c_prompt",
    )(qiT, wiT, ki3, qT, k, vT_all, btiles, _strict_lower(tk))


_NT = (((1,), (1,)), ((), ()))
_NN = (((1,), (0,)), ((), ()))


def _pad_rows(x, rows):
    return jnp.concatenate([x, jnp.zeros((rows - x.shape[0], x.shape[1]), x.dtype)], axis=0)


def _prefix_and_total(n, strict_before):
    i = np.arange(n)
    tri = (i[:, None] < i[None, :]) if strict_before else (i[:, None] > i[None, :])
    return jnp.asarray(np.concatenate([tri, np.ones((n, n), bool)], axis=1).astype(np.float32), MXU_DTYPE)


def _sb_sample_kernel(n_pages, resume, pt_ref, q_ref, kn_ref, vn_ref, hmask_ref, sufx_ref, *rest):
    k_pages, v_pages = rest[:n_pages], rest[n_pages:2 * n_pages]
    rest = rest[2 * n_pages:]
    if resume:
        carry_in_ref, acc_in_ref, o_ref, carry_ref, acc_ref = rest
    else:
        o_ref, carry_out_ref, acc_out_ref = rest
        carry_ref, acc_ref = carry_out_ref.at[0], acc_out_ref.at[0]
    t = q_ref.shape[1]
    rows, tk = SB_HEADS * t, PAGE_SIZE
    hmask = hmask_ref[...]
    q = q_ref[0] * (SB_HEAD_DIM ** -0.5)
    qbd = (jnp.concatenate([q] * SB_HEADS, axis=0) * hmask).astype(MXU_DTYPE)

    def step(z, weighted_values, valid):
        sp = jnp.maximum(z, 0.0) + jnp.log(1.0 + jnp.exp(-jnp.abs(z)))
        lr = -sp if valid is None else jnp.where(valid, -sp, 0.0)
        hi, lo = _split_hi_lo(lr)
        both = (jnp.dot(hi, sufx_ref[...], preferred_element_type=F32)
                + jnp.dot(lo, sufx_ref[...], preferred_element_type=F32))
        a = jnp.exp(z - sp + both[:, :tk] + carry_ref[...])
        if valid is not None:
            a = jnp.where(valid, a, 0.0)
        acc_ref[...] += weighted_values(a.astype(MXU_DTYPE))
        carry_ref[...] += both[:, tk:]

    if resume:
        carry_ref[...] = carry_in_ref[0]
        acc_ref[...] = acc_in_ref[0]
    else:
        carry_ref[...] = jnp.zeros(carry_ref.shape, F32)
        acc_ref[...] = jnp.zeros(acc_ref.shape, F32)
        row_t = lax.broadcasted_iota(jnp.int32, (rows, tk), 0) & (t - 1)
        lane = lax.broadcasted_iota(jnp.int32, (rows, tk), 1)
        kn = _pad_rows(kn_ref[0], tk).astype(MXU_DTYPE)
        vn = _pad_rows(vn_ref[0], tk).astype(MXU_DTYPE)
        step(lax.dot_general(qbd, kn, _NT, preferred_element_type=F32),
             lambda a: jnp.dot(a, vn, preferred_element_type=F32), lane < row_t)
    for p in range(n_pages):
        @pl.when(jnp.max(carry_ref[...]) > SB_EXIT)
        def _(p=p):
            kT = k_pages[p][...].reshape(SB_WIDTH, tk).astype(MXU_DTYPE)
            vT = v_pages[p][...].reshape(SB_WIDTH, tk).astype(MXU_DTYPE)
            step(jnp.dot(qbd, kT, preferred_element_type=F32),
                 lambda a: lax.dot_general(a, vT, _NT, preferred_element_type=F32), None)

    acc = acc_ref[...] * hmask
    y = acc[0:t]
    for h in range(1, SB_HEADS):
        y = y + acc[h * t:(h + 1) * t]
    o_ref[0] = y


def _feature_major(cache):
    nd = cache.ndim
    return cache.transpose((0, 1) + tuple(range(3, nd)) + (2,))


def _page_specs(pages, layer, page_shape):
    zeros = (0,) * len(page_shape)
    return [pl.BlockSpec((None, None) + page_shape, lambda i, pt, p=p: (layer, pt[i, p]) + zeros)
            for p in pages]


SB_NEAR_PAGES = 2


def _sb_sample_pass(pages, state, q, k, v, ck, cv, layer, page_table):
    b, t, _ = q.shape
    rows = SB_HEADS * t
    resume = state is not None
    page = (SB_HEADS, SB_HEAD_DIM, PAGE_SIZE)
    hmask = jnp.asarray((np.arange(rows)[:, None] // t == np.arange(SB_WIDTH)[None, :] // SB_HEAD_DIM)
                        .astype(np.float32))
    tok = pl.BlockSpec((1, t, SB_WIDTH), lambda i, pt: (i, 0, 0))
    const = lambda i, pt: (0, 0)
    carry = (jax.ShapeDtypeStruct((b, rows, PAGE_SIZE), F32),
             pl.BlockSpec((1, rows, PAGE_SIZE), lambda i, pt: (i, 0, 0)))
    acc = (jax.ShapeDtypeStruct((b, rows, SB_WIDTH), F32),
           pl.BlockSpec((1, rows, SB_WIDTH), lambda i, pt: (i, 0, 0)))
    y_shape = jax.ShapeDtypeStruct((b, t, SB_WIDTH), F32)
    return pl.pallas_call(
        functools.partial(_sb_sample_kernel, len(pages), resume),
        out_shape=y_shape if resume else (y_shape, carry[0], acc[0]),
        grid_spec=pltpu.PrefetchScalarGridSpec(
            num_scalar_prefetch=1,
            grid=(b,),
            in_specs=[tok, tok, tok,
                      pl.BlockSpec((rows, SB_WIDTH), const),
                      pl.BlockSpec((PAGE_SIZE, 2 * PAGE_SIZE), const)]
            + _page_specs(pages, layer, page) + _page_specs(pages, layer, page)
            + ([carry[1], acc[1]] if resume else []),
            out_specs=tok if resume else (tok, carry[1], acc[1]),
            scratch_shapes=([pltpu.VMEM((rows, PAGE_SIZE), F32), pltpu.VMEM((rows, SB_WIDTH), F32)]
                            if resume else []),
        ),
        compiler_params=pltpu.CompilerParams(
            dimension_semantics=("arbitrary",), vmem_limit_bytes=VMEM_LIMIT_BYTES),
        name="sb_sample_resume" if resume else "sb_sample",
    )(page_table, q, k, v, hmask, _prefix_and_total(PAGE_SIZE, False),
      *([ck] * len(pages)), *([cv] * len(pages)), *(state or ()))


def _sb_sample(q, k, v, cache_k, cache_v, layer, page_table):
    t = q.shape[1]
    assert t & (t - 1) == 0 and t <= PAGE_SIZE
    ck, cv = _feature_major(cache_k), _feature_major(cache_v)
    newest_first = tuple(reversed(range(page_table.shape[1])))
    near, far = newest_first[:SB_NEAR_PAGES], newest_first[SB_NEAR_PAGES:]
    args = (q, k, v, ck, cv, layer, page_table)
    y, carry, acc = _sb_sample_pass(near, None, *args)
    if not far:
        return y
    return lax.cond(jnp.max(carry) > SB_EXIT,
                    lambda: _sb_sample_pass(far, (carry, acc), *args), lambda: y)


def _c_sample_kernel(n_pages, topk, pt_ref, qi_ref, w_ref, q_ref, kin_ref, kn_ref, vn_ref,
                     sbias_ref, prex_ref, *rest):
    idx_pages, k_pages, v_pages = rest[:n_pages], rest[n_pages:2 * n_pages], rest[2 * n_pages:3 * n_pages]
    o_ref, sc_ref, lg_ref = rest[3 * n_pages:]
    t = kin_ref.shape[1]
    tk = PAGE_SIZE
    nb = n_pages + 1
    ih = IDX_HEADS * t
    rows = C_HEADS * t
    row8 = lax.broadcasted_iota(jnp.int32, (t, tk), 0)
    lane8 = lax.broadcasted_iota(jnp.int32, (t, tk), 1)
    new_valid = lane8 <= row8
    qi_cat = qi_ref[0]
    w = w_ref[0] * (IDX_HEADS ** -0.5)

    def scores(idx_blk, valid):
        dims = _NT if valid is not None else _NN
        hi, lo = _split_hi_lo(idx_blk)
        s = lax.dot_general(qi_cat, hi, dims, preferred_element_type=F32)
        s = s[:ih] + s[ih:] + lax.dot_general(qi_cat[:ih], lo, dims, preferred_element_type=F32)
        r = jnp.maximum(s, 0.0) * w
        sc = r[0:t]
        for hh in range(1, IDX_HEADS):
            sc = sc + r[hh * t:(hh + 1) * t]
        if valid is not None:
            sc = jnp.where(valid, sc, -jnp.inf)
        return _sortable(sc)

    for p in range(n_pages):
        sc_ref[:, p * tk:(p + 1) * tk] = scores(idx_pages[p][...], None)
    sc_ref[:, n_pages * tk:] = scores(_pad_rows(kin_ref[0], tk), new_valid)

    def count(pred, thr):
        return jnp.sum(jnp.where(pred(sc_ref[...], thr), 1.0, 0.0), axis=1, keepdims=True)

    def bit_step(i, t_u):
        cand_u = t_u | lax.shift_left(jnp.int32(1), 31 - i)
        cnt = count(lambda x, thr: x >= thr, cand_u ^ INT_MIN)
        return jnp.where(cnt >= topk, cand_u, t_u)

    thr = lax.fori_loop(0, 32, bit_step, jnp.zeros((t, 1), jnp.int32)) ^ INT_MIN
    need = topk - count(lambda x, th: x > th, thr)

    seen = jnp.zeros((t, tk), F32)
    for blk in range(nb):
        x = sc_ref[:, blk * tk:(blk + 1) * tk]
        eq = jnp.where(x == thr, 1.0, 0.0)
        both = jnp.dot(eq.astype(MXU_DTYPE), prex_ref[...], preferred_element_type=F32)
        tie = jnp.where(both[:, :tk] + seen < need, eq, 0.0)
        sel = jnp.where(x > thr, 1.0, tie)
        if blk == n_pages:
            sel = jnp.where(new_valid, sel, 0.0)
        sc_ref[:, blk * tk:(blk + 1) * tk] = sel.astype(jnp.int32)
        seen = seen + both[:, tk:]

    q_rows = q_ref[0]
    grp = C_GROUP * t
    hd = C_HEAD_DIM

    def kv_block(pages, new_ref, blk, n):
        if blk < n_pages:
            return pages[blk][n].astype(MXU_DTYPE)
        return _pad_rows(new_ref[0][:, n * hd:(n + 1) * hd], tk).astype(MXU_DTYPE)

    m = jnp.full((rows, tk), NEG, F32)
    for blk in range(nb):
        bias = sbias_ref[2] if blk < n_pages - 1 else sbias_ref[blk - (n_pages - 1)]
        lg = jnp.concatenate(
            [lax.dot_general(q_rows[n * grp:(n + 1) * grp], kv_block(k_pages, kn_ref, blk, n),
                             _NN if blk < n_pages else _NT, preferred_element_type=F32)
             for n in range(C_KV_HEADS)], axis=0) + bias
        sel = jnp.concatenate([sc_ref[:, blk * tk:(blk + 1) * tk]] * C_HEADS, axis=0) != 0
        lg = jnp.where(sel, lg, NEG)
        lg_ref[:, blk * tk:(blk + 1) * tk] = lg
        m = jnp.maximum(m, lg)
    m_row = jnp.max(m, axis=1, keepdims=True)
    lsum = jnp.zeros((rows, tk), F32)
    accs = [jnp.zeros((grp, hd), F32) for _ in range(C_KV_HEADS)]
    for blk in range(nb):
        p = jnp.exp(lg_ref[:, blk * tk:(blk + 1) * tk] - m_row)
        lsum = lsum + p
        p = p.astype(MXU_DTYPE)
        accs = [acc + lax.dot_general(p[n * grp:(n + 1) * grp], kv_block(v_pages, vn_ref, blk, n),
                                      _NT if blk < n_pages else _NN, preferred_element_type=F32)
                for n, acc in enumerate(accs)]
    l_row = jnp.sum(lsum, axis=1, keepdims=True)
    for n, acc in enumerate(accs):
        o = acc / l_row[n * grp:(n + 1) * grp]
        for g in range(C_GROUP):
            o_ref[0, n * C_GROUP + g] = o[g * t:(g + 1) * t]


def _c_sample(q, k, v, qi, ki, wi, cache_k, cache_v, cache_idx, layer, page_table, sbias):
    b, t, _ = q.shape
    n_pages = page_table.shape[1]
    topk = min(TOPK_MAX, (n_pages * PAGE_SIZE + t) // 4)
    rows = C_HEADS * t
    ih = IDX_HEADS * t
    assert rows == ATT_TILE and t <= PAGE_SIZE
    q_rows = ((q * C_HEAD_DIM ** -0.5).reshape(b, t, C_HEADS, C_HEAD_DIM).transpose(0, 2, 1, 3)
              .reshape(b, rows, C_HEAD_DIM).astype(MXU_DTYPE))
    qi_rows = ((qi * IDX_DIM ** -0.5).reshape(b, t, IDX_HEADS, IDX_DIM).transpose(0, 2, 1, 3)
               .reshape(b, ih, IDX_DIM))
    qi_cat = jnp.concatenate(_split_hi_lo(qi_rows), axis=1)
    w_col = wi.transpose(0, 2, 1).reshape(b, ih, 1)
    kv_page = (C_KV_HEADS, C_HEAD_DIM, PAGE_SIZE)
    per_seq = lambda r, c: pl.BlockSpec((1, r, c), lambda i, pt: (i, 0, 0))
    const2 = lambda i, pt: (0, 0)
    o = pl.pallas_call(
        functools.partial(_c_sample_kernel, n_pages, topk),
        out_shape=jax.ShapeDtypeStruct((b, C_HEADS, t, C_HEAD_DIM), F32),
        grid_spec=pltpu.PrefetchScalarGridSpec(
            num_scalar_prefetch=1,
            grid=(b,),
            in_specs=[per_seq(2 * ih, IDX_DIM), per_seq(ih, 1), per_seq(rows, C_HEAD_DIM),
                      per_seq(t, IDX_DIM), per_seq(t, C_KV_WIDTH), per_seq(t, C_KV_WIDTH),
                      pl.BlockSpec((3, ATT_TILE, ATT_TILE), lambda i, pt: (0, 0, 0)),
                      pl.BlockSpec((PAGE_SIZE, 2 * PAGE_SIZE), const2)]
            + _page_specs(range(n_pages), layer, (IDX_DIM, PAGE_SIZE))
            + _page_specs(range(n_pages), layer, kv_page) + _page_specs(range(n_pages), layer, kv_page),
            out_specs=pl.BlockSpec((1, C_HEADS, t, C_HEAD_DIM), lambda i, pt: (i, 0, 0, 0)),
            scratch_shapes=[pltpu.VMEM((t, (n_pages + 1) * PAGE_SIZE), jnp.int32),
                            pltpu.VMEM((rows, (n_pages + 1) * PAGE_SIZE), F32)],
        ),
        compiler_params=pltpu.CompilerParams(
            dimension_semantics=("arbitrary",), vmem_limit_bytes=VMEM_LIMIT_BYTES),
        name="c_sample",
    )(page_table, qi_cat, w_col, q_rows, ki, k, v, sbias, _prefix_and_total(PAGE_SIZE, True),
      *([_feature_major(cache_idx)] * n_pages), *([_feature_major(cache_k)] * n_pages),
      *([_feature_major(cache_v)] * n_pages))
    return o.transpose(0, 2, 1, 3).reshape(b, t, C_WIDTH)


POOL_HALO = POOL_BUF + 1


def _pool_tile(ext, first_pos, w_ref, scale_ref):
    t = ext.shape[0] - POOL_HALO
    pos1 = lax.broadcasted_iota(jnp.int32, (t, POOL_GROUP), 0) + (first_pos + 1)
    outs = []
    for g, w in enumerate(POOL_WINDOWS):
        lanes = slice(g * POOL_GROUP, (g + 1) * POOL_GROUP)
        e = ext[:, lanes]
        s, span = e, 1
        while span < w:
            s = s + pltpu.roll(s, span, 0)
            span *= 2
        cnt = jnp.minimum(pos1, w).astype(F32)
        d = s[POOL_HALO:] / cnt - e[POOL_HALO:]
        y = jnp.dot(d.astype(MXU_DTYPE), w_ref[g], preferred_element_type=F32)
        outs.append(y * scale_ref[:, lanes])
    return jnp.concatenate(outs, axis=1)


def _pool_prompt_kernel(u_ref, halo_ref, w_ref, scale_ref, o_ref):
    i = pl.program_id(1)
    halo = jnp.where(i > 0, halo_ref[...], 0.0)
    ext = jnp.concatenate([halo, u_ref[...]], axis=0)
    o_ref[...] = _pool_tile(ext, i * u_ref.shape[0], w_ref, scale_ref)


def _pool_prompt(u, w_pool, pool_scale, batch, seq):
    assert max(POOL_WINDOWS) <= POOL_HALO and TOKEN_TILE % POOL_HALO == 0
    nt = seq // TOKEN_TILE
    per_tile = TOKEN_TILE // POOL_HALO
    return pl.pallas_call(
        _pool_prompt_kernel,
        out_shape=jax.ShapeDtypeStruct(u.shape, F32),
        grid=(batch, nt),
        in_specs=[
            pl.BlockSpec((TOKEN_TILE, POOL_WIDTH), lambda b, i: (b * nt + i, 0)),
            pl.BlockSpec((POOL_HALO, POOL_WIDTH),
                         lambda b, i: (jnp.maximum((b * nt + i) * per_tile - 1, 0), 0)),
            pl.BlockSpec(w_pool.shape, lambda b, i: (0, 0, 0)),
            pl.BlockSpec((1, POOL_WIDTH), lambda b, i: (0, 0)),
        ],
        out_specs=pl.BlockSpec((TOKEN_TILE, POOL_WIDTH), lambda b, i: (b * nt + i, 0)),
        compiler_params=pltpu.CompilerParams(
            dimension_semantics=("parallel", "arbitrary"), vmem_limit_bytes=VMEM_LIMIT_BYTES),
        name="pool_prompt",
    )(u, u, w_pool.astype(MXU_DTYPE), pool_scale[None])


POOL_SAMPLE_GROUP = 8


def _pool_sample_kernel(first_pos, ext_ref, w_ref, scale_ref, o_ref):
    for s in range(ext_ref.shape[0]):
        o_ref[s] = _pool_tile(ext_ref[s], first_pos, w_ref, scale_ref)


def _pool_sample(u_ext, first_pos, w_pool, pool_scale):
    n, rows, _ = u_ext.shape
    t = rows - POOL_HALO
    grp = POOL_SAMPLE_GROUP
    return pl.pallas_call(
        functools.partial(_pool_sample_kernel, first_pos),
        out_shape=jax.ShapeDtypeStruct((n, t, POOL_WIDTH), F32),
        grid=(n // grp,),
        in_specs=[
            pl.BlockSpec((grp, rows, POOL_WIDTH), lambda i: (i, 0, 0)),
            pl.BlockSpec(w_pool.shape, lambda i: (0, 0, 0)),
            pl.BlockSpec((1, POOL_WIDTH), lambda i: (0, 0)),
        ],
        out_specs=pl.BlockSpec((grp, t, POOL_WIDTH), lambda i: (i, 0, 0)),
        compiler_params=pltpu.CompilerParams(
            dimension_semantics=("parallel",), vmem_limit_bytes=VMEM_LIMIT_BYTES),
        name="pool_sample",
    )(u_ext, w_pool.astype(MXU_DTYPE), pool_scale[None])


def _segment_mean_matrix(width):
    seg = np.arange(width) // C_HEAD_DIM
    return jnp.asarray((seg[:, None] == seg[None, :]).astype(np.float32) / C_HEAD_DIM, MXU_DTYPE)


def kernel(x_prompt, x_sample, state_pool, cache_b_k, cache_b_v, cache_c_k, cache_c_v, cache_c_idx, page_table,
           g_ffn, w_ffn_gate, w_ffn_up, w_ffn_down, g_mix, w_in_ab, w_pool, pool_scale, w_out_ab,
           w_in_c, g_q, g_k, w_out_c, rel_bias):
    bp, tp, _ = x_prompt.shape
    bs, ts, _ = x_sample.shape
    mp, ms = bp * tp, bs * ts
    n_even, n_odd = (DEPTH + 1) // 2, DEPTH // 2
    past = page_table.shape[1] * PAGE_SIZE
    x = jnp.concatenate([x_prompt.reshape(mp, D_MODEL), x_sample.reshape(ms, D_MODEL)], axis=0)

    wg = w_ffn_gate.astype(MXU_DTYPE)
    wu = w_ffn_up.astype(MXU_DTYPE)
    wd = w_ffn_down.astype(MXU_DTYPE)
    w_ab = w_in_ab.astype(MXU_DTYPE)
    w_oab = w_out_ab.astype(MXU_DTYPE)
    w_c = jnp.pad(w_in_c, ((0, 0), (0, 0), (0, C_IN_PAD - C_IN))).astype(MXU_DTYPE)
    w_oc = w_out_c.astype(MXU_DTYPE)
    segq = _segment_mean_matrix(C_WIDTH)
    segk = _segment_mean_matrix(C_KV_WIDTH)
    btiles, sbias = _bias_tiles(rel_bias, ts)

    pool_p, pool_s, kbs, vbs, kcs, vcs, ics = [], [], [], [], [], [], []
    kv_b = kv_c = None
    for layer in range(DEPTH):
        j = layer // 2
        x = _ffn(x, g_ffn[layer, 0][None], wg[layer, 0], wu[layer, 0], wd[layer, 0])
        g = g_mix[layer][None]
        if layer % 2 == 0:
            u_p, qT_p, k_p, *kv_b = _proj_ab_prompt(x, g, w_ab[j], bp, tp, j, n_even, kv_b)
            u_s, q_s, k_s, v_s = (a.reshape(bs, ts, -1) for a in _proj_ab_sample(x, g, w_ab[j], mp, ms))
            u_ext = jnp.concatenate([jnp.zeros((bs, POOL_HALO - POOL_BUF, POOL_WIDTH), F32),
                                     state_pool[j], u_s], axis=1)
            parts_p = (_pool_prompt(u_p, w_pool[j], pool_scale[j], bp, tp),
                       _sb_prompt(qT_p, k_p, kv_b[1], j).reshape(mp, SB_WIDTH))
            parts_s = (_pool_sample(u_ext, past, w_pool[j], pool_scale[j]).reshape(ms, POOL_WIDTH),
                       _sb_sample(q_s, k_s, v_s, cache_b_k, cache_b_v, j, page_table).reshape(ms, SB_WIDTH))
            pool_p.append(u_p.reshape(bp, tp, POOL_WIDTH)[:, tp - POOL_BUF:])
            pool_s.append(u_ext[:, -POOL_BUF:])
            kbs.append(k_s.reshape(bs, ts, SB_HEADS, SB_HEAD_DIM))
            vbs.append(v_s.reshape(bs, ts, SB_HEADS, SB_HEAD_DIM))
            w_out = w_oab[j]
        else:
            gq = jnp.tile(g_q[j], C_HEADS)[None]
            gk = jnp.tile(g_k[j], C_KV_HEADS)[None]
            qT_p, k_p, qiT_p, wiT_p, ki3_p, *kv_c = _proj_c_prompt(
                x, g, w_c[j], gq, gk, segq, segk, bp, tp, j, n_odd, kv_c)
            q_s, k_s, v_s, qi_s, tail_s = (
                a.reshape(bs, ts, -1) for a in _proj_c_sample(x, g, w_c[j], gq, gk, segq, segk, mp, ms))
            ki_s = tail_s[..., :IDX_DIM]
            wi_s = tail_s[..., IDX_DIM:IDX_DIM + IDX_HEADS]
            parts_p = (_c_prompt(qT_p, k_p, kv_c[1], j, qiT_p, wiT_p, ki3_p, btiles).reshape(mp, C_WIDTH),)
            parts_s = (_c_sample(q_s, k_s, v_s, qi_s, ki_s, wi_s, cache_c_k, cache_c_v, cache_c_idx,
                                 j, page_table, sbias).reshape(ms, C_WIDTH),)
            kcs.append(k_s.reshape(bs, ts, C_KV_HEADS, C_HEAD_DIM))
            vcs.append(v_s.reshape(bs, ts, C_KV_HEADS, C_HEAD_DIM))
            ics.append(ki_s)
            w_out = w_oc[j]
        x_new = _out_proj(x, parts_p, w_out, 0)
        x = _out_proj(x, parts_s, w_out, mp, prev=x_new)
        x = _ffn(x, g_ffn[layer, 1][None], wg[layer, 1], wu[layer, 1], wd[layer, 1])

    def token_major(aT, heads):
        n, b, width, t = aT.shape
        return aT.reshape(n, b, heads, width // heads, t).transpose(0, 1, 4, 2, 3)

    kbT, vbT = kv_b
    kcT, vcT, icT = kv_c
    return (x[:mp].reshape(bp, tp, D_MODEL), x[mp:].reshape(bs, ts, D_MODEL),
            jnp.stack(pool_p), jnp.stack(pool_s),
            token_major(kbT, SB_HEADS), token_major(vbT, SB_HEADS), jnp.stack(kbs), jnp.stack(vbs),
            token_major(kcT, C_KV_HEADS), token_major(vcT, C_KV_HEADS), icT.transpose(0, 1, 3, 2),
            jnp.stack(kcs), jnp.stack(vcs), jnp.stack(ics))
```

```python
import functools
import math

import jax
import jax.numpy as jnp
import numpy as np
from jax import lax
from jax.experimental import pallas as pl
from jax.experimental.pallas import tpu as pltpu

F32 = jnp.float32
BF16 = jnp.bfloat16
MXU_DTYPE = BF16

D_MODEL = 1024
DEPTH = 4
D_FF = 2816
POOL_WINDOWS = (2, 4, 8, 16)
POOL_WIDTH = 512
POOL_GROUP = 128
POOL_BUF = 15
SB_HEADS = 8
SB_HEAD_DIM = 64
SB_WIDTH = 512
AB_IN = POOL_WIDTH + 3 * SB_WIDTH
C_HEAD_DIM = 64
C_HEADS = 16
C_KV_HEADS = 4
C_GROUP = 4
C_WIDTH = 1024
C_KV_WIDTH = 256
IDX_HEADS = 8
IDX_DIM = 64
TOPK_MAX = 256
C_IN = 2120
C_IN_PAD = 2176
REL_BUCKETS = 32
REL_MAX_EXACT = 16
REL_MAX_DIST = 128
Q_BLOCK = 128
PAGE_SIZE = 128
EPS = 1e-6
NEG = -1e30

VMEM_LIMIT_BYTES = 56 * 1024 * 1024
FF_CHUNK = 256
TOKEN_TILE = 512


def _rms(x, g):
    ms = jnp.mean(x * x, axis=-1, keepdims=True)
    return x * lax.rsqrt(ms + EPS) * g


def _ffn_kernel(x_ref, g_ref, wg_ref, wu_ref, wd_ref, o_ref):
    x = x_ref[...]
    h = _rms(x, g_ref[...]).astype(MXU_DTYPE)
    acc = x
    for c in range(D_FF // FF_CHUNK):
        sl = slice(c * FF_CHUNK, (c + 1) * FF_CHUNK)
        gate = jnp.dot(h, wg_ref[:, sl], preferred_element_type=F32)
        up = jnp.dot(h, wu_ref[:, sl], preferred_element_type=F32)
        act = (0.5 * gate * jax.nn.sigmoid(gate) * up).astype(MXU_DTYPE)
        acc = acc + jnp.dot(act, wd_ref[sl, :], preferred_element_type=F32)
    o_ref[...] = acc


def _ffn(x, g, wg, wu, wd):
    m = x.shape[0]
    resident = dict(pipeline_mode=pl.Buffered(1))
    return pl.pallas_call(
        _ffn_kernel,
        out_shape=jax.ShapeDtypeStruct((m, D_MODEL), F32),
        grid=(m // TOKEN_TILE,),
        in_specs=[
            pl.BlockSpec((TOKEN_TILE, D_MODEL), lambda i: (i, 0)),
            pl.BlockSpec((1, D_MODEL), lambda i: (0, 0)),
            pl.BlockSpec((D_MODEL, D_FF), lambda i: (0, 0), **resident),
            pl.BlockSpec((D_MODEL, D_FF), lambda i: (0, 0), **resident),
            pl.BlockSpec((D_FF, D_MODEL), lambda i: (0, 0), **resident),
        ],
        out_specs=pl.BlockSpec((TOKEN_TILE, D_MODEL), lambda i: (i, 0)),
        compiler_params=pltpu.CompilerParams(
            dimension_semantics=("parallel",), vmem_limit_bytes=VMEM_LIMIT_BYTES),
        name="ffn_half",
    )(x, g, wg, wu, wd)


def _proj_ab_sample_kernel(x_ref, g_ref, w_ref, u_ref, q_ref, k_ref, v_ref):
    h = _rms(x_ref[...], g_ref[...]).astype(MXU_DTYPE)
    p = jnp.dot(h, w_ref[...], preferred_element_type=F32)
    u_ref[...] = p[:, :POOL_WIDTH]
    q_ref[...] = p[:, POOL_WIDTH:POOL_WIDTH + SB_WIDTH]
    k_ref[...] = p[:, POOL_WIDTH + SB_WIDTH:POOL_WIDTH + 2 * SB_WIDTH]
    v_ref[...] = p[:, POOL_WIDTH + 2 * SB_WIDTH:]


def _proj_ab_sample(x, g, w, row0, rows):
    first = row0 // TOKEN_TILE
    out = jax.ShapeDtypeStruct((rows, SB_WIDTH), F32)
    spec = pl.BlockSpec((TOKEN_TILE, SB_WIDTH), lambda i: (i, 0))
    return pl.pallas_call(
        _proj_ab_sample_kernel,
        out_shape=(out, out, out, out),
        grid=(rows // TOKEN_TILE,),
        in_specs=[
            pl.BlockSpec((TOKEN_TILE, D_MODEL), lambda i: (first + i, 0)),
            pl.BlockSpec((1, D_MODEL), lambda i: (0, 0)),
            pl.BlockSpec((D_MODEL, AB_IN), lambda i: (0, 0), pipeline_mode=pl.Buffered(1)),
        ],
        out_specs=(spec, spec, spec, spec),
        compiler_params=pltpu.CompilerParams(
            dimension_semantics=("parallel",), vmem_limit_bytes=VMEM_LIMIT_BYTES),
        name="proj_ab_sample",
    )(x, g, w)


def _proj_ab_prompt_kernel(x_ref, g_ref, w_ref, *rest):
    u_ref, qT_ref, kn_ref, kT_ref, vT_ref = rest[-5:]
    h = _rms(x_ref[...], g_ref[...]).astype(MXU_DTYPE)
    p = jnp.dot(h, w_ref[...], preferred_element_type=F32)
    u_ref[...] = p[:, :POOL_WIDTH]
    q = p[:, POOL_WIDTH:POOL_WIDTH + SB_WIDTH] * (SB_HEAD_DIM ** -0.5)
    k = p[:, POOL_WIDTH + SB_WIDTH:POOL_WIDTH + 2 * SB_WIDTH]
    v = p[:, POOL_WIDTH + 2 * SB_WIDTH:]
    qT_ref[0] = q.T.astype(MXU_DTYPE)
    kn_ref[0] = k.astype(MXU_DTYPE)
    kT_ref[0] = k.T
    vT_ref[0] = v.T


def _slot_alias(prev, first_input, first_output):
    if prev is None:
        return [], {}
    specs = [pl.BlockSpec(memory_space=pl.ANY)] * len(prev)
    return specs, {first_input + n: first_output + n for n in range(len(prev))}


def _proj_ab_prompt(x, g, w, batch, seq, slot, n_slots, prev):
    nt = seq // TOKEN_TILE
    alias_specs, aliases = _slot_alias(prev, 3, 3)
    stacked = jax.ShapeDtypeStruct((n_slots, batch, SB_WIDTH, seq), F32)
    stacked_spec = pl.BlockSpec((None, 1, SB_WIDTH, TOKEN_TILE), lambda b, i: (slot, b, 0, i))
    return pl.pallas_call(
        _proj_ab_prompt_kernel,
        out_shape=(jax.ShapeDtypeStruct((batch * seq, POOL_WIDTH), F32),
                   jax.ShapeDtypeStruct((batch, SB_WIDTH, seq), MXU_DTYPE),
                   jax.ShapeDtypeStruct((batch, seq, SB_WIDTH), MXU_DTYPE),
                   stacked, stacked),
        grid=(batch, nt),
        in_specs=[
            pl.BlockSpec((TOKEN_TILE, D_MODEL), lambda b, i: (b * nt + i, 0)),
            pl.BlockSpec((1, D_MODEL), lambda b, i: (0, 0)),
            pl.BlockSpec((D_MODEL, AB_IN), lambda b, i: (0, 0), pipeline_mode=pl.Buffered(1)),
        ] + alias_specs,
        out_specs=(pl.BlockSpec((TOKEN_TILE, POOL_WIDTH), lambda b, i: (b * nt + i, 0)),
                   pl.BlockSpec((1, SB_WIDTH, TOKEN_TILE), lambda b, i: (b, 0, i)),
                   pl.BlockSpec((1, TOKEN_TILE, SB_WIDTH), lambda b, i: (b, i, 0)),
                   stacked_spec, stacked_spec),
        input_output_aliases=aliases,
        compiler_params=pltpu.CompilerParams(
            dimension_semantics=("parallel", "parallel"), vmem_limit_bytes=VMEM_LIMIT_BYTES),
        name="proj_ab_prompt",
    )(x, g, w, *(prev or ()))


def _split_hi_lo(x):
    hi = x.astype(MXU_DTYPE)
    lo = (x - hi.astype(F32)).astype(MXU_DTYPE)
    return hi, lo


def _head_rms(x, seg_mean_ref, gain):
    hi, lo = _split_hi_lo(x * x)
    ms = (jnp.dot(hi, seg_mean_ref[...], preferred_element_type=F32)
          + jnp.dot(lo, seg_mean_ref[...], preferred_element_type=F32))
    return x * lax.rsqrt(ms + EPS) * gain


C_OFF_K = C_WIDTH
C_OFF_V = C_OFF_K + C_KV_WIDTH
C_OFF_QI = C_OFF_V + C_KV_WIDTH
C_OFF_TAIL = C_OFF_QI + IDX_HEADS * IDX_DIM
C_TAIL = C_IN_PAD - C_OFF_TAIL


def _proj_c_parts(x_ref, g_ref, w_ref, gq_ref, gk_ref, segq_ref, segk_ref):
    h = _rms(x_ref[...], g_ref[...]).astype(MXU_DTYPE)
    p = jnp.dot(h, w_ref[...], preferred_element_type=F32)
    q = _head_rms(p[:, :C_OFF_K], segq_ref, gq_ref[...])
    k = _head_rms(p[:, C_OFF_K:C_OFF_V], segk_ref, gk_ref[...])
    return q, k, p[:, C_OFF_V:C_OFF_QI], p[:, C_OFF_QI:C_OFF_TAIL], p[:, C_OFF_TAIL:]


def _proj_c_sample_kernel(x_ref, g_ref, w_ref, gq_ref, gk_ref, segq_ref, segk_ref,
                          q_ref, k_ref, v_ref, qi_ref, tail_ref):
    q, k, v, qi, tail = _proj_c_parts(x_ref, g_ref, w_ref, gq_ref, gk_ref, segq_ref, segk_ref)
    q_ref[...] = q
    k_ref[...] = k
    v_ref[...] = v
    qi_ref[...] = qi
    tail_ref[...] = tail


def _proj_c_in_specs(row_block):
    const = lambda *_: (0, 0)
    return [
        pl.BlockSpec((TOKEN_TILE, D_MODEL), row_block),
        pl.BlockSpec((1, D_MODEL), const),
        pl.BlockSpec((D_MODEL, C_IN_PAD), const, pipeline_mode=pl.Buffered(1)),
        pl.BlockSpec((1, C_WIDTH), const),
        pl.BlockSpec((1, C_KV_WIDTH), const),
        pl.BlockSpec((C_WIDTH, C_WIDTH), const, pipeline_mode=pl.Buffered(1)),
        pl.BlockSpec((C_KV_WIDTH, C_KV_WIDTH), const, pipeline_mode=pl.Buffered(1)),
    ]


def _proj_c_sample(x, g, w, gq, gk, segq, segk, row0, rows):
    first = row0 // TOKEN_TILE
    widths = (C_WIDTH, C_KV_WIDTH, C_KV_WIDTH, IDX_HEADS * IDX_DIM, C_TAIL)
    return pl.pallas_call(
        _proj_c_sample_kernel,
        out_shape=tuple(jax.ShapeDtypeStruct((rows, n), F32) for n in widths),
        grid=(rows // TOKEN_TILE,),
        in_specs=_proj_c_in_specs(lambda i: (first + i, 0)),
        out_specs=tuple(pl.BlockSpec((TOKEN_TILE, n), lambda i: (i, 0)) for n in widths),
        compiler_params=pltpu.CompilerParams(
            dimension_semantics=("parallel",), vmem_limit_bytes=VMEM_LIMIT_BYTES),
        name="proj_c_sample",
    )(x, g, w, gq, gk, segq, segk)


def _proj_c_prompt_kernel(x_ref, g_ref, w_ref, gq_ref, gk_ref, segq_ref, segk_ref, *rest):
    qT_ref, kn_ref, qiT_ref, wiT_ref, ki3_ref, kT_ref, vT_ref, kiT_ref = rest[-8:]
    q, k, v, qi, tail = _proj_c_parts(x_ref, g_ref, w_ref, gq_ref, gk_ref, segq_ref, segk_ref)
    tm = q.shape[0]
    qT_ref[0] = (q * (C_HEAD_DIM ** -0.5 * LOG2E)).T.astype(MXU_DTYPE).reshape(C_HEADS, C_HEAD_DIM, tm)
    kn_ref[0] = k.astype(MXU_DTYPE)
    kT_ref[0] = k.T
    vT_ref[0] = v.T
    qi_hi, qi_lo = _split_hi_lo((qi * (IDX_DIM ** -0.5)).T)
    for hh in range(IDX_HEADS):
        rows = slice(hh * IDX_DIM, (hh + 1) * IDX_DIM)
        qiT_ref[0, hh, 0:IDX_DIM] = qi_hi[rows]
        qiT_ref[0, hh, IDX_DIM:2 * IDX_DIM] = qi_lo[rows]
        qiT_ref[0, hh, 2 * IDX_DIM:] = qi_hi[rows]
    tail_t = tail.T
    kiT_ref[0] = tail_t[:IDX_DIM]
    wiT_ref[0] = tail_t[IDX_DIM:IDX_DIM + IDX_HEADS]
    ki_hi, ki_lo = _split_hi_lo(tail[:, :IDX_DIM])
    ki3_ref[0] = jnp.concatenate([ki_hi, ki_hi, ki_lo], axis=1)


def _proj_c_prompt(x, g, w, gq, gk, segq, segk, batch, seq, slot, n_slots, prev):
    nt = seq // TOKEN_TILE
    tm = TOKEN_TILE
    alias_specs, aliases = _slot_alias(prev, 7, 5)

    def stacked(width):
        return (jax.ShapeDtypeStruct((n_slots, batch, width, seq), F32),
                pl.BlockSpec((None, 1, width, tm), lambda b, i: (slot, b, 0, i)))

    (kT_s, kT_b), (vT_s, vT_b), (kiT_s, kiT_b) = stacked(C_KV_WIDTH), stacked(C_KV_WIDTH), stacked(IDX_DIM)
    return pl.pallas_call(
        _proj_c_prompt_kernel,
        out_shape=(jax.ShapeDtypeStruct((batch, C_HEADS, C_HEAD_DIM, seq), MXU_DTYPE),
                   jax.ShapeDtypeStruct((batch, seq, C_KV_WIDTH), MXU_DTYPE),
                   jax.ShapeDtypeStruct((batch, IDX_HEADS, 3 * IDX_DIM, seq), MXU_DTYPE),
                   jax.ShapeDtypeStruct((batch, IDX_HEADS, seq), F32),
                   jax.ShapeDtypeStruct((batch, seq, 3 * IDX_DIM), MXU_DTYPE),
                   kT_s, vT_s, kiT_s),
        grid=(batch, nt),
        in_specs=_proj_c_in_specs(lambda b, i: (b * nt + i, 0)) + alias_specs,
        out_specs=(pl.BlockSpec((1, C_HEADS, C_HEAD_DIM, tm), lambda b, i: (b, 0, 0, i)),
                   pl.BlockSpec((1, tm, C_KV_WIDTH), lambda b, i: (b, i, 0)),
                   pl.BlockSpec((1, IDX_HEADS, 3 * IDX_DIM, tm), lambda b, i: (b, 0, 0, i)),
                   pl.BlockSpec((1, IDX_HEADS, tm), lambda b, i: (b, 0, i)),
                   pl.BlockSpec((1, tm, 3 * IDX_DIM), lambda b, i: (b, i, 0)),
                   kT_b, vT_b, kiT_b),
        input_output_aliases=aliases,
        compiler_params=pltpu.CompilerParams(
            dimension_semantics=("parallel", "parallel"), vmem_limit_bytes=VMEM_LIMIT_BYTES),
        name="proj_c_prompt",
    )(x, g, w, gq, gk, segq, segk, *(prev or ()))


def _out_proj_kernel(n_parts, x_ref, *rest):
    parts, w_ref, o_ref = rest[:n_parts], rest[n_parts], rest[-1]
    acc = x_ref[...]
    off = 0
    for part in parts:
        width = part.shape[1]
        acc = acc + jnp.dot(part[...].astype(MXU_DTYPE), w_ref[off:off + width, :],
                            preferred_element_type=F32)
        off += width
    o_ref[...] = acc


def _out_proj(x, parts, w, row0, prev=None):
    rows = parts[0].shape[0]
    first = row0 // TOKEN_TILE
    x_tile = pl.BlockSpec((TOKEN_TILE, D_MODEL), lambda i: (first + i, 0))
    alias_specs, aliases = _slot_alias(None if prev is None else (prev,), 2 + len(parts), 0)
    return pl.pallas_call(
        functools.partial(_out_proj_kernel, len(parts)),
        out_shape=jax.ShapeDtypeStruct(x.shape, F32),
        grid=(rows // TOKEN_TILE,),
        in_specs=[x_tile]
        + [pl.BlockSpec((TOKEN_TILE, p.shape[1]), lambda i: (i, 0)) for p in parts]
        + [pl.BlockSpec((D_MODEL, D_MODEL), lambda i: (0, 0), pipeline_mode=pl.Buffered(1))]
        + alias_specs,
        out_specs=x_tile,
        input_output_aliases=aliases,
        compiler_params=pltpu.CompilerParams(
            dimension_semantics=("parallel",), vmem_limit_bytes=VMEM_LIMIT_BYTES),
        name="out_proj",
    )(x, *parts, w, *(() if prev is None else (prev,)))


ATT_TILE = 128
ATT_TK = 128
ATT_TQ = 256
SB_TK = 256
SB_LOCKSTEP = 2
SB_EXIT = -104.0


def _sb_prompt_kernel(qT_ref, k_ref, vT_ref, upper_ref, o_ref):
    tk, tq = SB_TK, ATT_TQ
    qb = pl.program_id(1)
    row = lax.broadcasted_iota(jnp.int32, (tk, tq), 0)
    lane = lax.broadcasted_iota(jnp.int32, (tk, tq), 1)
    q_idx = qb * tq + lane
    upper = upper_ref[...]
    last_tile = qb * (tq // tk) + (tq // tk - 1)

    def head_step(h, ks, valid, carry, acc):
        hs = slice(h * SB_HEAD_DIM, (h + 1) * SB_HEAD_DIM)
        kb = k_ref[0, pl.ds(ks, tk), hs]
        z = jnp.dot(kb, qT_ref[0, hs, :], preferred_element_type=F32)
        sp = jnp.maximum(z, 0.0) + jnp.log(1.0 + jnp.exp(-jnp.abs(z)))
        lr = jnp.where(valid, -sp, 0.0)
        hi, lo = _split_hi_lo(lr)
        between = (jnp.dot(upper, hi, preferred_element_type=F32)
                   + jnp.dot(upper, lo, preferred_element_type=F32) + carry)
        a = jnp.where(valid, jnp.exp(z - sp + between), 0.0)
        vb = vT_ref[0, hs, pl.ds(ks, tk)].astype(MXU_DTYPE)
        acc = acc + jnp.dot(vb, a.astype(MXU_DTYPE), preferred_element_type=F32)
        return carry + jnp.sum(lr, axis=0, keepdims=True), acc

    for h0 in range(0, SB_HEADS, SB_LOCKSTEP):
        heads = range(h0, h0 + SB_LOCKSTEP)

        def body(state, heads=heads):
            j, _, carries, accs = state
            ks = pl.multiple_of(j * tk, tk)
            valid = (ks + row) < q_idx
            out = [head_step(h, ks, valid, c, a) for h, c, a in zip(heads, carries, accs)]
            carries = tuple(o[0] for o in out)
            top = functools.reduce(jnp.maximum, [jnp.max(c) for c in carries])
            return j - 1, top, carries, tuple(o[1] for o in out)

        def cond(state):
            j, top, _, _ = state
            return jnp.logical_and(j >= 0, top > SB_EXIT)

        init = (last_tile, jnp.float32(0.0),
                tuple(jnp.zeros((1, tq), F32) for _ in heads),
                tuple(jnp.zeros((SB_HEAD_DIM, tq), F32) for _ in heads))
        _, _, _, accs = lax.while_loop(cond, body, init)
        o_ref[0, :, h0 * SB_HEAD_DIM:(h0 + SB_LOCKSTEP) * SB_HEAD_DIM] = jnp.concatenate(accs, axis=0).T


def _strict_upper(n):
    i = np.arange(n)
    return jnp.asarray((i[None, :] > i[:, None]).astype(np.float32), MXU_DTYPE)


def _strict_lower(n):
    i = np.arange(n)
    return jnp.asarray((i[None, :] < i[:, None]).astype(np.float32), MXU_DTYPE)


def _sb_prompt(qT, k, vT_all, slot):
    b, width, t = qT.shape
    assert (SB_LOCKSTEP * SB_HEAD_DIM) % 128 == 0
    return pl.pallas_call(
        _sb_prompt_kernel,
        out_shape=jax.ShapeDtypeStruct((b, t, width), F32),
        grid=(b, t // ATT_TQ),
        in_specs=[
            pl.BlockSpec((1, width, ATT_TQ), lambda i, j: (i, 0, j)),
            pl.BlockSpec((1, t, width), lambda i, j: (i, 0, 0)),
            pl.BlockSpec((None, 1, width, t), lambda i, j: (slot, i, 0, 0)),
            pl.BlockSpec((SB_TK, SB_TK), lambda i, j: (0, 0)),
        ],
        out_specs=pl.BlockSpec((1, ATT_TQ, width), lambda i, j: (i, j, 0)),
        compiler_params=pltpu.CompilerParams(
            dimension_semantics=("parallel", "arbitrary"), vmem_limit_bytes=VMEM_LIMIT_BYTES),
        name="sb_prompt",
    )(qT, k, vT_all, _strict_upper(SB_TK))


INT_MIN = -2 ** 31
COUNT_CHUNK = 4 * ATT_TK
LOG2E = math.log2(math.e)
C_TQ = 256
REL_NEAR_TILES = C_TQ // ATT_TK + 1
REL_LAST_BUCKET_FROM = math.ceil(REL_MAX_EXACT * (REL_MAX_DIST / REL_MAX_EXACT) ** (
    (REL_BUCKETS - 1 - REL_MAX_EXACT) / (REL_BUCKETS - REL_MAX_EXACT)))
assert ATT_TK + 1 >= REL_LAST_BUCKET_FROM
SUM_ROWS = 16


def _bucket_of(dist):
    n = jnp.maximum(dist, 0)
    nf = jnp.maximum(n, 1).astype(F32)
    large = REL_MAX_EXACT + (jnp.log(nf / REL_MAX_EXACT) / math.log(REL_MAX_DIST / REL_MAX_EXACT)
                             * (REL_BUCKETS - REL_MAX_EXACT)).astype(jnp.int32)
    return jnp.where(n < REL_MAX_EXACT, n, jnp.minimum(large, REL_BUCKETS - 1))


def _bias_tile_kernel(dec_seq, relb_ref, o_ref, s_ref):
    krow = lax.broadcasted_iota(jnp.int32, (ATT_TK, C_TQ), 0)
    qlane = lax.broadcasted_iota(jnp.int32, (ATT_TK, C_TQ), 1)
    for c in range(REL_NEAR_TILES):
        bucket = _bucket_of((1 - c) * ATT_TK + qlane - krow)
        for h in range(C_HEADS):
            tile = jnp.zeros((ATT_TK, C_TQ), F32)
            for b in range(REL_BUCKETS):
                tile = jnp.where(bucket == b, relb_ref[b, h], tile)
            o_ref[h, c] = (tile - relb_ref[REL_BUCKETS - 1, h]) * LOG2E
    t = ATT_TILE
    row = lax.broadcasted_iota(jnp.int32, (t, t), 0)
    lane = lax.broadcasted_iota(jnp.int32, (t, t), 1)
    row_head = row >> int(math.log2(dec_seq))
    row_t = row & (dec_seq - 1)
    buckets = (_bucket_of(t + row_t - lane), _bucket_of(row_t - lane),
               jnp.full((t, t), REL_BUCKETS - 1, jnp.int32))
    tiles = [jnp.zeros((t, t), F32) for _ in buckets]
    for b in range(REL_BUCKETS):
        by_head = jnp.zeros((t, t), F32)
        for h in range(C_HEADS):
            by_head = jnp.where(row_head == h, relb_ref[b, h], by_head)
        tiles = [jnp.where(bk == b, by_head, tl) for bk, tl in zip(buckets, tiles)]
    for c, tl in enumerate(tiles):
        s_ref[c] = tl


def _bias_tiles(rel_bias, dec_seq):
    assert C_HEADS * dec_seq == ATT_TILE
    return pl.pallas_call(
        functools.partial(_bias_tile_kernel, dec_seq),
        out_shape=(jax.ShapeDtypeStruct((C_HEADS, REL_NEAR_TILES, ATT_TK, C_TQ), F32),
                   jax.ShapeDtypeStruct((3, ATT_TILE, ATT_TILE), F32)),
        in_specs=[pl.BlockSpec(memory_space=pltpu.SMEM)],
        name="rel_bias_tiles",
    )(rel_bias)


def _sortable(s):
    bits = lax.bitcast_convert_type(s, jnp.int32)
    return bits ^ ((bits >> 31) & 0x7FFFFFFF)


def _c_prompt_kernel(topk, qiT_ref, wiT_ref, ki_ref, qT_ref, k_ref, vT_ref, btile_ref,
                     lower_ref, o_ref, key_ref, thr_ref, m_ref, l_ref, acc_ref):
    tk, tq = ATT_TK, C_TQ
    qb = pl.program_id(1)
    n_tiles = (qb + 1) * (tq // tk)
    n_chunks = (n_tiles * tk + COUNT_CHUNK - 1) // COUNT_CHUNK
    row = lax.broadcasted_iota(jnp.int32, (tk, tq), 0)
    lane = lax.broadcasted_iota(jnp.int32, (tk, tq), 1)
    q_idx = qb * tq + lane
    w = wiT_ref[0] * (IDX_HEADS ** -0.5)

    def score_tile(ks):
        kib = ki_ref[0, pl.ds(ks, tk), :]
        s = jnp.zeros((tk, tq), F32)
        for hh in range(IDX_HEADS):
            d = jnp.dot(kib, qiT_ref[0, hh], preferred_element_type=F32)
            s = s + jnp.maximum(d, 0.0) * w[hh:hh + 1, :]
        s = jnp.where(ks + row <= q_idx, s, -jnp.inf)
        key_ref[pl.ds(ks, tk), :] = _sortable(s)

    def score_chunk(c, _):
        for i in range(COUNT_CHUNK // tk):
            score_tile(pl.multiple_of(c * COUNT_CHUNK + i * tk, tk))
        return 0

    lax.fori_loop(0, n_chunks, score_chunk, 0)

    def count(pred, thr):
        def chunk(c, cnt):
            base = pl.multiple_of(c * COUNT_CHUNK, COUNT_CHUNK)
            ind = jnp.where(pred(key_ref[pl.ds(base, COUNT_CHUNK), :], thr), 1, 0)
            return cnt + jnp.sum(ind.reshape(COUNT_CHUNK // 8, 8, tq), axis=0)
        cnt = lax.fori_loop(0, n_chunks, chunk, jnp.zeros((8, tq), jnp.int32))
        return jnp.sum(cnt, axis=0, keepdims=True)

    def bit_step(i, t_u):
        cand_u = t_u | lax.shift_left(jnp.int32(1), 31 - i)
        cnt = count(lambda x, thr: x >= thr, cand_u ^ INT_MIN)
        return jnp.where(cnt >= topk, cand_u, t_u)

    thr = lax.fori_loop(0, 32, bit_step, jnp.zeros((1, tq), jnp.int32)) ^ INT_MIN
    n_ge = count(lambda x, t: x >= t, thr)
    thr_ref[...] = thr

    @pl.when(jnp.max(jnp.abs(n_ge - topk)) > 0)
    def _():
        need = (topk - count(lambda x, t: x > t, thr)).astype(F32)

        def select_block(j, seen):
            ks = pl.multiple_of(j * tk, tk)
            blk = key_ref[pl.ds(ks, tk), :]
            eq = jnp.where(blk == thr, 1.0, 0.0)
            rank = jnp.dot(lower_ref[...], eq.astype(MXU_DTYPE), preferred_element_type=F32) + seen
            tie = jnp.where(rank < need, eq, 0.0)
            sel = jnp.where(blk > thr, 1.0, tie)
            sel = jnp.where(ks + row <= q_idx, sel, 0.0)
            key_ref[pl.ds(ks, tk), :] = sel.astype(jnp.int32)
            return seen + jnp.sum(eq, axis=0, keepdims=True)

        lax.fori_loop(0, n_tiles, select_block, jnp.zeros((1, tq), F32))
        thr_ref[...] = jnp.ones((1, tq), jnp.int32)

    m_ref[...] = jnp.full(m_ref.shape, NEG, F32)
    l_ref[...] = jnp.zeros(l_ref.shape, F32)
    acc_ref[...] = jnp.zeros(acc_ref.shape, F32)
    ones = jnp.ones((SUM_ROWS, tk), MXU_DTYPE)

    def attend(j, near):
        ks = pl.multiple_of(j * tk, tk)
        sel = key_ref[pl.ds(ks, tk), :] >= thr_ref[...]
        for n in range(C_KV_HEADS):
            ns = slice(n * C_HEAD_DIM, (n + 1) * C_HEAD_DIM)
            kb = k_ref[0, pl.ds(ks, tk), ns]
            vb = vT_ref[0, ns, pl.ds(ks, tk)].astype(MXU_DTYPE)
            for g in range(C_GROUP):
                h = n * C_GROUP + g
                lg = jnp.dot(kb, qT_ref[0, h], preferred_element_type=F32)
                if near is not None:
                    lg = lg + btile_ref[h, near]
                lg = jnp.where(sel, lg, NEG)
                m_old = m_ref[h]
                m_new = jnp.maximum(m_old, jnp.max(lg, axis=0, keepdims=True))
                p = jnp.exp2(lg - m_new).astype(MXU_DTYPE)
                alpha = jnp.exp2(m_old - m_new)
                l_ref[h] = alpha * l_ref[h] + jnp.dot(ones, p, preferred_element_type=F32)
                acc_ref[h] = alpha * acc_ref[h] + jnp.dot(vb, p, preferred_element_type=F32)
                m_ref[h] = m_new

    def far_tile(j, _):
        attend(j, None)
        return 0

    first_near = n_tiles - REL_NEAR_TILES
    lax.fori_loop(0, jnp.maximum(first_near, 0), far_tile, 0)
    for c in range(REL_NEAR_TILES):
        if c == 0:
            pl.when(first_near >= 0)(functools.partial(attend, first_near, 0))
        else:
            attend(first_near + c, c)
    for h in range(0, C_HEADS, 2):
        pair = jnp.concatenate([acc_ref[h] / l_ref[h][0:1], acc_ref[h + 1] / l_ref[h + 1][0:1]], axis=0)
        o_ref[0, :, h * C_HEAD_DIM:(h + 2) * C_HEAD_DIM] = pair.T


def _c_prompt(qT, k, vT_all, slot, qiT, wiT, ki3, btiles):
    b, _, _, t = qT.shape
    topk = min(TOPK_MAX, t // 4)
    assert t % COUNT_CHUNK == 0 and COUNT_CHUNK > topk and t % C_TQ == 0
    tq, tk = C_TQ, ATT_TK
    return pl.pallas_call(
        functools.partial(_c_prompt_kernel, topk),
        out_shape=jax.ShapeDtypeStruct((b, t, C_WIDTH), F32),
        grid=(b, t // tq),
        in_specs=[
            pl.BlockSpec((1, IDX_HEADS, 3 * IDX_DIM, tq), lambda i, j: (i, 0, 0, j)),
            pl.BlockSpec((1, IDX_HEADS, tq), lambda i, j: (i, 0, j)),
            pl.BlockSpec((1, t, 3 * IDX_DIM), lambda i, j: (i, 0, 0)),
            pl.BlockSpec((1, C_HEADS, C_HEAD_DIM, tq), lambda i, j: (i, 0, 0, j)),
            pl.BlockSpec((1, t, C_KV_WIDTH), lambda i, j: (i, 0, 0)),
            pl.BlockSpec((None, 1, C_KV_WIDTH, t), lambda i, j: (slot, i, 0, 0)),
            pl.BlockSpec((C_HEADS, REL_NEAR_TILES, tk, tq), lambda i, j: (0, 0, 0, 0),
                         pipeline_mode=pl.Buffered(1)),
            pl.BlockSpec((tk, tk), lambda i, j: (0, 0)),
        ],
        out_specs=pl.BlockSpec((1, tq, C_WIDTH), lambda i, j: (i, j, 0)),
        scratch_shapes=[
            pltpu.VMEM((t, tq), jnp.int32),
            pltpu.VMEM((1, tq), jnp.int32),
            pltpu.VMEM((C_HEADS, 1, tq), F32),
            pltpu.VMEM((C_HEADS, SUM_ROWS, tq), F32),
            pltpu.VMEM((C_HEADS, C_HEAD_DIM, tq), F32),
        ],
        compiler_params=pltpu.CompilerParams(
            dimension_semantics=("parallel", "arbitrary"), vmem_limit_bytes=VMEM_LIMIT_BYTES),
        name="c_prompt",
    )(qiT, wiT, ki3, qT, k, vT_all, btiles, _strict_lower(tk))


_NT = (((1,), (1,)), ((), ()))
_NN = (((1,), (0,)), ((), ()))
SAMPLE_RADIX_BITS = 4


def _pad_rows(x, rows):
    return jnp.concatenate([x, jnp.zeros((rows - x.shape[0], x.shape[1]), x.dtype)], axis=0)


def _prefix_and_total(n, strict_before):
    i = np.arange(n)
    tri = (i[:, None] < i[None, :]) if strict_before else (i[:, None] > i[None, :])
    return jnp.asarray(np.concatenate([tri, np.ones((n, n), bool)], axis=1).astype(np.float32), MXU_DTYPE)


def _sb_sample_kernel(n_pages, resume, pt_ref, q_ref, kn_ref, vn_ref, hmask_ref, sufx_ref, *rest):
    k_pages, v_pages = rest[:n_pages], rest[n_pages:2 * n_pages]
    rest = rest[2 * n_pages:]
    if resume:
        carry_in_ref, acc_in_ref, o_ref, carry_ref, acc_ref = rest
    else:
        o_ref, carry_out_ref, acc_out_ref = rest
        carry_ref, acc_ref = carry_out_ref.at[0], acc_out_ref.at[0]
    t = q_ref.shape[1]
    rows, tk = SB_HEADS * t, PAGE_SIZE
    hmask = hmask_ref[...]
    q = q_ref[0] * (SB_HEAD_DIM ** -0.5)
    qbd = (jnp.concatenate([q] * SB_HEADS, axis=0) * hmask).astype(MXU_DTYPE)

    def step(z, weighted_values, valid):
        sp = jnp.maximum(z, 0.0) + jnp.log(1.0 + jnp.exp(-jnp.abs(z)))
        lr = -sp if valid is None else jnp.where(valid, -sp, 0.0)
        hi, lo = _split_hi_lo(lr)
        both = (jnp.dot(hi, sufx_ref[...], preferred_element_type=F32)
                + jnp.dot(lo, sufx_ref[...], preferred_element_type=F32))
        a = jnp.exp(z - sp + both[:, :tk] + carry_ref[...])
        if valid is not None:
            a = jnp.where(valid, a, 0.0)
        acc_ref[...] += weighted_values(a.astype(MXU_DTYPE))
        carry_ref[...] += both[:, tk:]

    if resume:
        carry_ref[...] = carry_in_ref[0]
        acc_ref[...] = acc_in_ref[0]
    else:
        carry_ref[...] = jnp.zeros(carry_ref.shape, F32)
        acc_ref[...] = jnp.zeros(acc_ref.shape, F32)
        row_t = lax.broadcasted_iota(jnp.int32, (rows, tk), 0) & (t - 1)
        lane = lax.broadcasted_iota(jnp.int32, (rows, tk), 1)
        kn = _pad_rows(kn_ref[0], tk).astype(MXU_DTYPE)
        vn = _pad_rows(vn_ref[0], tk).astype(MXU_DTYPE)
        step(lax.dot_general(qbd, kn, _NT, preferred_element_type=F32),
             lambda a: jnp.dot(a, vn, preferred_element_type=F32), lane < row_t)
    for p in range(n_pages):
        @pl.when(jnp.max(carry_ref[...]) > SB_EXIT)
        def _(p=p):
            kT = k_pages[p][...].reshape(SB_WIDTH, tk).astype(MXU_DTYPE)
            vT = v_pages[p][...].reshape(SB_WIDTH, tk).astype(MXU_DTYPE)
            step(jnp.dot(qbd, kT, preferred_element_type=F32),
                 lambda a: lax.dot_general(a, vT, _NT, preferred_element_type=F32), None)

    acc = acc_ref[...] * hmask
    y = acc[0:t]
    for h in range(1, SB_HEADS):
        y = y + acc[h * t:(h + 1) * t]
    o_ref[0] = y


def _feature_major(cache):
    nd = cache.ndim
    return cache.transpose((0, 1) + tuple(range(3, nd)) + (2,))


def _page_specs(pages, layer, page_shape):
    zeros = (0,) * len(page_shape)
    return [pl.BlockSpec((None, None) + page_shape, lambda i, pt, p=p: (layer, pt[i, p]) + zeros)
            for p in pages]


SB_NEAR_PAGES = 2


def _sb_sample_pass(pages, state, q, k, v, ck, cv, layer, page_table):
    b, t, _ = q.shape
    rows = SB_HEADS * t
    resume = state is not None
    page = (SB_HEADS, SB_HEAD_DIM, PAGE_SIZE)
    hmask = jnp.asarray((np.arange(rows)[:, None] // t == np.arange(SB_WIDTH)[None, :] // SB_HEAD_DIM)
                        .astype(np.float32))
    tok = pl.BlockSpec((1, t, SB_WIDTH), lambda i, pt: (i, 0, 0))
    const = lambda i, pt: (0, 0)
    carry = (jax.ShapeDtypeStruct((b, rows, PAGE_SIZE), F32),
             pl.BlockSpec((1, rows, PAGE_SIZE), lambda i, pt: (i, 0, 0)))
    acc = (jax.ShapeDtypeStruct((b, rows, SB_WIDTH), F32),
           pl.BlockSpec((1, rows, SB_WIDTH), lambda i, pt: (i, 0, 0)))
    y_shape = jax.ShapeDtypeStruct((b, t, SB_WIDTH), F32)
    return pl.pallas_call(
        functools.partial(_sb_sample_kernel, len(pages), resume),
        out_shape=y_shape if resume else (y_shape, carry[0], acc[0]),
        grid_spec=pltpu.PrefetchScalarGridSpec(
            num_scalar_prefetch=1,
            grid=(b,),
            in_specs=[tok, tok, tok,
                      pl.BlockSpec((rows, SB_WIDTH), const),
                      pl.BlockSpec((PAGE_SIZE, 2 * PAGE_SIZE), const)]
            + _page_specs(pages, layer, page) + _page_specs(pages, layer, page)
            + ([carry[1], acc[1]] if resume else []),
            out_specs=tok if resume else (tok, carry[1], acc[1]),
            scratch_shapes=([pltpu.VMEM((rows, PAGE_SIZE), F32), pltpu.VMEM((rows, SB_WIDTH), F32)]
                            if resume else []),
        ),
        compiler_params=pltpu.CompilerParams(
            dimension_semantics=("arbitrary",), vmem_limit_bytes=VMEM_LIMIT_BYTES),
        name="sb_sample_resume" if resume else "sb_sample",
    )(page_table, q, k, v, hmask, _prefix_and_total(PAGE_SIZE, False),
      *([ck] * len(pages)), *([cv] * len(pages)), *(state or ()))


def _sb_sample(q, k, v, cache_k, cache_v, layer, page_table):
    t = q.shape[1]
    assert t & (t - 1) == 0 and t <= PAGE_SIZE
    ck, cv = _feature_major(cache_k), _feature_major(cache_v)
    newest_first = tuple(reversed(range(page_table.shape[1])))
    near, far = newest_first[:SB_NEAR_PAGES], newest_first[SB_NEAR_PAGES:]
    args = (q, k, v, ck, cv, layer, page_table)
    y, carry, acc = _sb_sample_pass(near, None, *args)
    if not far:
        return y
    return lax.cond(jnp.max(carry) > SB_EXIT,
                    lambda: _sb_sample_pass(far, (carry, acc), *args), lambda: y)


def _c_sample_kernel(n_pages, topk, pt_ref, qi_ref, w_ref, q_ref, kin_ref, kn_ref, vn_ref,
                     sbias_ref, prex_ref, *rest):
    idx_pages, k_pages, v_pages = rest[:n_pages], rest[n_pages:2 * n_pages], rest[2 * n_pages:3 * n_pages]
    o_ref, sc_ref, lg_ref = rest[3 * n_pages:]
    t = kin_ref.shape[1]
    tk = PAGE_SIZE
    nb = n_pages + 1
    ih = IDX_HEADS * t
    rows = C_HEADS * t
    row8 = lax.broadcasted_iota(jnp.int32, (t, tk), 0)
    lane8 = lax.broadcasted_iota(jnp.int32, (t, tk), 1)
    new_valid = lane8 <= row8
    qi_cat = qi_ref[0]
    w = w_ref[0] * (IDX_HEADS ** -0.5)

    def scores(idx_blk, valid):
        dims = _NT if valid is not None else _NN
        hi, lo = _split_hi_lo(idx_blk)
        s = lax.dot_general(qi_cat, hi, dims, preferred_element_type=F32)
        s = s[:ih] + s[ih:] + lax.dot_general(qi_cat[:ih], lo, dims, preferred_element_type=F32)
        r = jnp.maximum(s, 0.0) * w
        sc = r[0:t]
        for hh in range(1, IDX_HEADS):
            sc = sc + r[hh * t:(hh + 1) * t]
        if valid is not None:
            sc = jnp.where(valid, sc, -jnp.inf)
        return _sortable(sc)

    for p in range(n_pages):
        sc_ref[:, p * tk:(p + 1) * tk] = scores(idx_pages[p][...], None)
    sc_ref[:, n_pages * tk:] = scores(_pad_rows(kin_ref[0], tk), new_valid)

    def count(pred, thr):
        return jnp.sum(jnp.where(pred(sc_ref[...], thr), 1.0, 0.0), axis=1, keepdims=True)

    def digit_step(i, t_u):
        shift = 32 - SAMPLE_RADIX_BITS * (i + 1)
        digit = jnp.zeros((t, 1), jnp.int32)
        for v in range(1, 2 ** SAMPLE_RADIX_BITS):
            cand_u = t_u | lax.shift_left(jnp.int32(v), shift)
            cnt = count(lambda x, thr: x >= thr, cand_u ^ INT_MIN)
            digit = digit + jnp.where(cnt >= topk, 1, 0)
        return t_u | lax.shift_left(digit, shift)

    thr = lax.fori_loop(0, 32 // SAMPLE_RADIX_BITS, digit_step, jnp.zeros((t, 1), jnp.int32)) ^ INT_MIN
    need = topk - count(lambda x, th: x > th, thr)

    seen = jnp.zeros((t, tk), F32)
    for blk in range(nb):
        x = sc_ref[:, blk * tk:(blk + 1) * tk]
        eq = jnp.where(x == thr, 1.0, 0.0)
        both = jnp.dot(eq.astype(MXU_DTYPE), prex_ref[...], preferred_element_type=F32)
        tie = jnp.where(both[:, :tk] + seen < need, eq, 0.0)
        sel = jnp.where(x > thr, 1.0, tie)
        if blk == n_pages:
            sel = jnp.where(new_valid, sel, 0.0)
        sc_ref[:, blk * tk:(blk + 1) * tk] = sel.astype(jnp.int32)
        seen = seen + both[:, tk:]

    q_rows = q_ref[0]
    grp = C_GROUP * t
    hd = C_HEAD_DIM

    def kv_block(pages, new_ref, blk, n):
        if blk < n_pages:
            return pages[blk][n].astype(MXU_DTYPE)
        return _pad_rows(new_ref[0][:, n * hd:(n + 1) * hd], tk).astype(MXU_DTYPE)

    m = jnp.full((rows, tk), NEG, F32)
    for blk in range(nb):
        bias = sbias_ref[2] if blk < n_pages - 1 else sbias_ref[blk - (n_pages - 1)]
        lg = jnp.concatenate(
            [lax.dot_general(q_rows[n * grp:(n + 1) * grp], kv_block(k_pages, kn_ref, blk, n),
                             _NN if blk < n_pages else _NT, preferred_element_type=F32)
             for n in range(C_KV_HEADS)], axis=0) + bias
        sel = jnp.concatenate([sc_ref[:, blk * tk:(blk + 1) * tk]] * C_HEADS, axis=0) != 0
        lg = jnp.where(sel, lg, NEG)
        lg_ref[:, blk * tk:(blk + 1) * tk] = lg
        m = jnp.maximum(m, lg)
    m_row = jnp.max(m, axis=1, keepdims=True)
    lsum = jnp.zeros((rows, tk), F32)
    accs = [jnp.zeros((grp, hd), F32) for _ in range(C_KV_HEADS)]
    for blk in range(nb):
        p = jnp.exp(lg_ref[:, blk * tk:(blk + 1) * tk] - m_row)
        lsum = lsum + p
        p = p.astype(MXU_DTYPE)
        accs = [acc + lax.dot_general(p[n * grp:(n + 1) * grp], kv_block(v_pages, vn_ref, blk, n),
                                      _NT if blk < n_pages else _NN, preferred_element_type=F32)
                for n, acc in enumerate(accs)]
    l_row = jnp.sum(lsum, axis=1, keepdims=True)
    for n, acc in enumerate(accs):
        o = acc / l_row[n * grp:(n + 1) * grp]
        for g in range(C_GROUP):
            o_ref[0, n * C_GROUP + g] = o[g * t:(g + 1) * t]


def _c_sample(q, k, v, qi, ki, wi, cache_k, cache_v, cache_idx, layer, page_table, sbias):
    b, t, _ = q.shape
    n_pages = page_table.shape[1]
    topk = min(TOPK_MAX, (n_pages * PAGE_SIZE + t) // 4)
    rows = C_HEADS * t
    ih = IDX_HEADS * t
    assert rows == ATT_TILE and t <= PAGE_SIZE
    q_rows = ((q * C_HEAD_DIM ** -0.5).reshape(b, t, C_HEADS, C_HEAD_DIM).transpose(0, 2, 1, 3)
              .reshape(b, rows, C_HEAD_DIM).astype(MXU_DTYPE))
    qi_rows = ((qi * IDX_DIM ** -0.5).reshape(b, t, IDX_HEADS, IDX_DIM).transpose(0, 2, 1, 3)
               .reshape(b, ih, IDX_DIM))
    qi_cat = jnp.concatenate(_split_hi_lo(qi_rows), axis=1)
    w_col = wi.transpose(0, 2, 1).reshape(b, ih, 1)
    kv_page = (C_KV_HEADS, C_HEAD_DIM, PAGE_SIZE)
    per_seq = lambda r, c: pl.BlockSpec((1, r, c), lambda i, pt: (i, 0, 0))
    const2 = lambda i, pt: (0, 0)
    o = pl.pallas_call(
        functools.partial(_c_sample_kernel, n_pages, topk),
        out_shape=jax.ShapeDtypeStruct((b, C_HEADS, t, C_HEAD_DIM), F32),
        grid_spec=pltpu.PrefetchScalarGridSpec(
            num_scalar_prefetch=1,
            grid=(b,),
            in_specs=[per_seq(2 * ih, IDX_DIM), per_seq(ih, 1), per_seq(rows, C_HEAD_DIM),
                      per_seq(t, IDX_DIM), per_seq(t, C_KV_WIDTH), per_seq(t, C_KV_WIDTH),
                      pl.BlockSpec((3, ATT_TILE, ATT_TILE), lambda i, pt: (0, 0, 0)),
                      pl.BlockSpec((PAGE_SIZE, 2 * PAGE_SIZE), const2)]
            + _page_specs(range(n_pages), layer, (IDX_DIM, PAGE_SIZE))
            + _page_specs(range(n_pages), layer, kv_page) + _page_specs(range(n_pages), layer, kv_page),
            out_specs=pl.BlockSpec((1, C_HEADS, t, C_HEAD_DIM), lambda i, pt: (i, 0, 0, 0)),
            scratch_shapes=[pltpu.VMEM((t, (n_pages + 1) * PAGE_SIZE), jnp.int32),
                            pltpu.VMEM((rows, (n_pages + 1) * PAGE_SIZE), F32)],
        ),
        compiler_params=pltpu.CompilerParams(
            dimension_semantics=("arbitrary",), vmem_limit_bytes=VMEM_LIMIT_BYTES),
        name="c_sample",
    )(page_table, qi_cat, w_col, q_rows, ki, k, v, sbias, _prefix_and_total(PAGE_SIZE, True),
      *([_feature_major(cache_idx)] * n_pages), *([_feature_major(cache_k)] * n_pages),
      *([_feature_major(cache_v)] * n_pages))
    return o.transpose(0, 2, 1, 3).reshape(b, t, C_WIDTH)


POOL_HALO = POOL_BUF + 1


def _pool_tile(ext, first_pos, w_ref, scale_ref):
    t = ext.shape[0] - POOL_HALO
    pos1 = lax.broadcasted_iota(jnp.int32, (t, POOL_GROUP), 0) + (first_pos + 1)
    outs = []
    for g, w in enumerate(POOL_WINDOWS):
        lanes = slice(g * POOL_GROUP, (g + 1) * POOL_GROUP)
        e = ext[:, lanes]
        s, span = e, 1
        while span < w:
            s = s + pltpu.roll(s, span, 0)
            span *= 2
        cnt = jnp.minimum(pos1, w).astype(F32)
        d = s[POOL_HALO:] / cnt - e[POOL_HALO:]
        y = jnp.dot(d.astype(MXU_DTYPE), w_ref[g], preferred_element_type=F32)
        outs.append(y * scale_ref[:, lanes])
    return jnp.concatenate(outs, axis=1)


def _pool_prompt_kernel(u_ref, halo_ref, w_ref, scale_ref, o_ref):
    i = pl.program_id(1)
    halo = jnp.where(i > 0, halo_ref[...], 0.0)
    ext = jnp.concatenate([halo, u_ref[...]], axis=0)
    o_ref[...] = _pool_tile(ext, i * u_ref.shape[0], w_ref, scale_ref)


def _pool_prompt(u, w_pool, pool_scale, batch, seq):
    assert max(POOL_WINDOWS) <= POOL_HALO and TOKEN_TILE % POOL_HALO == 0
    nt = seq // TOKEN_TILE
    per_tile = TOKEN_TILE // POOL_HALO
    return pl.pallas_call(
        _pool_prompt_kernel,
        out_shape=jax.ShapeDtypeStruct(u.shape, F32),
        grid=(batch, nt),
        in_specs=[
            pl.BlockSpec((TOKEN_TILE, POOL_WIDTH), lambda b, i: (b * nt + i, 0)),
            pl.BlockSpec((POOL_HALO, POOL_WIDTH),
                         lambda b, i: (jnp.maximum((b * nt + i) * per_tile - 1, 0), 0)),
            pl.BlockSpec(w_pool.shape, lambda b, i: (0, 0, 0)),
            pl.BlockSpec((1, POOL_WIDTH), lambda b, i: (0, 0)),
        ],
        out_specs=pl.BlockSpec((TOKEN_TILE, POOL_WIDTH), lambda b, i: (b * nt + i, 0)),
        compiler_params=pltpu.CompilerParams(
            dimension_semantics=("parallel", "arbitrary"), vmem_limit_bytes=VMEM_LIMIT_BYTES),
        name="pool_prompt",
    )(u, u, w_pool.astype(MXU_DTYPE), pool_scale[None])


POOL_SAMPLE_GROUP = 8


def _pool_sample_kernel(first_pos, ext_ref, w_ref, scale_ref, o_ref):
    for s in range(ext_ref.shape[0]):
        o_ref[s] = _pool_tile(ext_ref[s], first_pos, w_ref, scale_ref)


def _pool_sample(u_ext, first_pos, w_pool, pool_scale):
    n, rows, _ = u_ext.shape
    t = rows - POOL_HALO
    grp = POOL_SAMPLE_GROUP
    return pl.pallas_call(
        functools.partial(_pool_sample_kernel, first_pos),
        out_shape=jax.ShapeDtypeStruct((n, t, POOL_WIDTH), F32),
        grid=(n // grp,),
        in_specs=[
            pl.BlockSpec((grp, rows, POOL_WIDTH), lambda i: (i, 0, 0)),
            pl.BlockSpec(w_pool.shape, lambda i: (0, 0, 0)),
            pl.BlockSpec((1, POOL_WIDTH), lambda i: (0, 0)),
        ],
        out_specs=pl.BlockSpec((grp, t, POOL_WIDTH), lambda i: (i, 0, 0)),
        compiler_params=pltpu.CompilerParams(
            dimension_semantics=("parallel",), vmem_limit_bytes=VMEM_LIMIT_BYTES),
        name="pool_sample",
    )(u_ext, w_pool.astype(MXU_DTYPE), pool_scale[None])


def _segment_mean_matrix(width):
    seg = np.arange(width) // C_HEAD_DIM
    return jnp.asarray((seg[:, None] == seg[None, :]).astype(np.float32) / C_HEAD_DIM, MXU_DTYPE)


def kernel(x_prompt, x_sample, state_pool, cache_b_k, cache_b_v, cache_c_k, cache_c_v, cache_c_idx, page_table,
           g_ffn, w_ffn_gate, w_ffn_up, w_ffn_down, g_mix, w_in_ab, w_pool, pool_scale, w_out_ab,
           w_in_c, g_q, g_k, w_out_c, rel_bias):
    bp, tp, _ = x_prompt.shape
    bs, ts, _ = x_sample.shape
    mp, ms = bp * tp, bs * ts
    n_even, n_odd = (DEPTH + 1) // 2, DEPTH // 2
    past = page_table.shape[1] * PAGE_SIZE
    x = jnp.concatenate([x_prompt.reshape(mp, D_MODEL), x_sample.reshape(ms, D_MODEL)], axis=0)

    wg = w_ffn_gate.astype(MXU_DTYPE)
    wu = w_ffn_up.astype(MXU_DTYPE)
    wd = w_ffn_down.astype(MXU_DTYPE)
    w_ab = w_in_ab.astype(MXU_DTYPE)
    w_oab = w_out_ab.astype(MXU_DTYPE)
    w_c = jnp.pad(w_in_c, ((0, 0), (0, 0), (0, C_IN_PAD - C_IN))).astype(MXU_DTYPE)
    w_oc = w_out_c.astype(MXU_DTYPE)
    segq = _segment_mean_matrix(C_WIDTH)
    segk = _segment_mean_matrix(C_KV_WIDTH)
    btiles, sbias = _bias_tiles(rel_bias, ts)

    pool_p, pool_s, kbs, vbs, kcs, vcs, ics = [], [], [], [], [], [], []
    kv_b = kv_c = None
    for layer in range(DEPTH):
        j = layer // 2
        x = _ffn(x, g_ffn[layer, 0][None], wg[layer, 0], wu[layer, 0], wd[layer, 0])
        g = g_mix[layer][None]
        if layer % 2 == 0:
            u_p, qT_p, k_p, *kv_b = _proj_ab_prompt(x, g, w_ab[j], bp, tp, j, n_even, kv_b)
            u_s, q_s, k_s, v_s = (a.reshape(bs, ts, -1) for a in _proj_ab_sample(x, g, w_ab[j], mp, ms))
            u_ext = jnp.concatenate([jnp.zeros((bs, POOL_HALO - POOL_BUF, POOL_WIDTH), F32),
                                     state_pool[j], u_s], axis=1)
            parts_p = (_pool_prompt(u_p, w_pool[j], pool_scale[j], bp, tp),
                       _sb_prompt(qT_p, k_p, kv_b[1], j).reshape(mp, SB_WIDTH))
            parts_s = (_pool_sample(u_ext, past, w_pool[j], pool_scale[j]).reshape(ms, POOL_WIDTH),
                       _sb_sample(q_s, k_s, v_s, cache_b_k, cache_b_v, j, page_table).reshape(ms, SB_WIDTH))
            pool_p.append(u_p.reshape(bp, tp, POOL_WIDTH)[:, tp - POOL_BUF:])
            pool_s.append(u_ext[:, -POOL_BUF:])
            kbs.append(k_s.reshape(bs, ts, SB_HEADS, SB_HEAD_DIM))
            vbs.append(v_s.reshape(bs, ts, SB_HEADS, SB_HEAD_DIM))
            w_out = w_oab[j]
        else:
            gq = jnp.tile(g_q[j], C_HEADS)[None]
            gk = jnp.tile(g_k[j], C_KV_HEADS)[None]
            qT_p, k_p, qiT_p, wiT_p, ki3_p, *kv_c = _proj_c_prompt(
                x, g, w_c[j], gq, gk, segq, segk, bp, tp, j, n_odd, kv_c)
            q_s, k_s, v_s, qi_s, tail_s = (
                a.reshape(bs, ts, -1) for a in _proj_c_sample(x, g, w_c[j], gq, gk, segq, segk, mp, ms))
            ki_s = tail_s[..., :IDX_DIM]
            wi_s = tail_s[..., IDX_DIM:IDX_DIM + IDX_HEADS]
            parts_p = (_c_prompt(qT_p, k_p, kv_c[1], j, qiT_p, wiT_p, ki3_p, btiles).reshape(mp, C_WIDTH),)
            parts_s = (_c_sample(q_s, k_s, v_s, qi_s, ki_s, wi_s, cache_c_k, cache_c_v, cache_c_idx,
                                 j, page_table, sbias).reshape(ms, C_WIDTH),)
            kcs.append(k_s.reshape(bs, ts, C_KV_HEADS, C_HEAD_DIM))
            vcs.append(v_s.reshape(bs, ts, C_KV_HEADS, C_HEAD_DIM))
            ics.append(ki_s)
            w_out = w_oc[j]
        x_new = _out_proj(x, parts_p, w_out, 0)
        x = _out_proj(x, parts_s, w_out, mp, prev=x_new)
        x = _ffn(x, g_ffn[layer, 1][None], wg[layer, 1], wu[layer, 1], wd[layer, 1])

    def token_major(aT, heads):
        n, b, width, t = aT.shape
        return aT.reshape(n, b, heads, width // heads, t).transpose(0, 1, 4, 2, 3)

    kbT, vbT = kv_b
    kcT, vcT, icT = kv_c
    return (x[:mp].reshape(bp, tp, D_MODEL), x[mp:].reshape(bs, ts, D_MODEL),
            jnp.stack(pool_p), jnp.stack(pool_s),
            token_major(kbT, SB_HEADS), token_major(vbT, SB_HEADS), jnp.stack(kbs), jnp.stack(vbs),
            token_major(kcT, C_KV_HEADS), token_major(vcT, C_KV_HEADS), icT.transpose(0, 1, 3, 2),
            jnp.stack(kcs), jnp.stack(vcs), jnp.stack(ics))
```

```python
import functools
import math

import jax
import jax.numpy as jnp
import numpy as np
from jax import lax
from jax.experimental import pallas as pl
from jax.experimental.pallas import tpu as pltpu

F32 = jnp.float32
BF16 = jnp.bfloat16
MXU_DTYPE = BF16

D_MODEL = 1024
DEPTH = 4
D_FF = 2816
POOL_WINDOWS = (2, 4, 8, 16)
POOL_WIDTH = 512
POOL_GROUP = 128
POOL_BUF = 15
SB_HEADS = 8
SB_HEAD_DIM = 64
SB_WIDTH = 512
AB_IN = POOL_WIDTH + 3 * SB_WIDTH
C_HEAD_DIM = 64
C_HEADS = 16
C_KV_HEADS = 4
C_GROUP = 4
C_WIDTH = 1024
C_KV_WIDTH = 256
IDX_HEADS = 8
IDX_DIM = 64
TOPK_MAX = 256
C_IN = 2120
C_IN_PAD = 2176
REL_BUCKETS = 32
REL_MAX_EXACT = 16
REL_MAX_DIST = 128
Q_BLOCK = 128
PAGE_SIZE = 128
EPS = 1e-6
NEG = -1e30

VMEM_LIMIT_BYTES = 56 * 1024 * 1024
FF_CHUNK = 256
TOKEN_TILE = 512


def _rms(x, g):
    ms = jnp.mean(x * x, axis=-1, keepdims=True)
    return x * lax.rsqrt(ms + EPS) * g


def _ffn_kernel(x_ref, g_ref, wg_ref, wu_ref, wd_ref, *rest):
    o_ref = rest[-1]
    x = x_ref[...]
    h = _rms(x, g_ref[...]).astype(MXU_DTYPE)
    acc = x
    for c in range(D_FF // FF_CHUNK):
        sl = slice(c * FF_CHUNK, (c + 1) * FF_CHUNK)
        gate = jnp.dot(h, wg_ref[:, sl], preferred_element_type=F32)
        up = jnp.dot(h, wu_ref[:, sl], preferred_element_type=F32)
        act = (0.5 * gate * jax.nn.sigmoid(gate) * up).astype(MXU_DTYPE)
        acc = acc + jnp.dot(act, wd_ref[sl, :], preferred_element_type=F32)
    o_ref[...] = acc


def _ffn(x, g, wg, wu, wd, in_row0=0, rows=None, out_rows=None, out_row0=0, prev=None):
    rows = x.shape[0] if rows is None else rows
    out_rows = rows if out_rows is None else out_rows
    first_in, first_out = in_row0 // TOKEN_TILE, out_row0 // TOKEN_TILE
    resident = dict(pipeline_mode=pl.Buffered(1))
    alias_specs, aliases = _slot_alias(None if prev is None else (prev,), 5, 0)
    return pl.pallas_call(
        _ffn_kernel,
        out_shape=jax.ShapeDtypeStruct((out_rows, D_MODEL), F32),
        grid=(rows // TOKEN_TILE,),
        in_specs=[
            pl.BlockSpec((TOKEN_TILE, D_MODEL), lambda i: (first_in + i, 0)),
            pl.BlockSpec((1, D_MODEL), lambda i: (0, 0)),
            pl.BlockSpec((D_MODEL, D_FF), lambda i: (0, 0), **resident),
            pl.BlockSpec((D_MODEL, D_FF), lambda i: (0, 0), **resident),
            pl.BlockSpec((D_FF, D_MODEL), lambda i: (0, 0), **resident),
        ] + alias_specs,
        out_specs=pl.BlockSpec((TOKEN_TILE, D_MODEL), lambda i: (first_out + i, 0)),
        input_output_aliases=aliases,
        compiler_params=pltpu.CompilerParams(
            dimension_semantics=("parallel",), vmem_limit_bytes=VMEM_LIMIT_BYTES),
        name="ffn_half",
    )(x, g, wg, wu, wd, *(() if prev is None else (prev,)))


def _proj_ab_sample_kernel(x_ref, g_ref, w_ref, u_ref, q_ref, k_ref, v_ref):
    h = _rms(x_ref[...], g_ref[...]).astype(MXU_DTYPE)
    p = jnp.dot(h, w_ref[...], preferred_element_type=F32)
    u_ref[...] = p[:, :POOL_WIDTH]
    q_ref[...] = p[:, POOL_WIDTH:POOL_WIDTH + SB_WIDTH]
    k_ref[...] = p[:, POOL_WIDTH + SB_WIDTH:POOL_WIDTH + 2 * SB_WIDTH]
    v_ref[...] = p[:, POOL_WIDTH + 2 * SB_WIDTH:]


def _proj_ab_sample(x, g, w, row0, rows):
    first = row0 // TOKEN_TILE
    out = jax.ShapeDtypeStruct((rows, SB_WIDTH), F32)
    spec = pl.BlockSpec((TOKEN_TILE, SB_WIDTH), lambda i: (i, 0))
    return pl.pallas_call(
        _proj_ab_sample_kernel,
        out_shape=(out, out, out, out),
        grid=(rows // TOKEN_TILE,),
        in_specs=[
            pl.BlockSpec((TOKEN_TILE, D_MODEL), lambda i: (first + i, 0)),
            pl.BlockSpec((1, D_MODEL), lambda i: (0, 0)),
            pl.BlockSpec((D_MODEL, AB_IN), lambda i: (0, 0), pipeline_mode=pl.Buffered(1)),
        ],
        out_specs=(spec, spec, spec, spec),
        compiler_params=pltpu.CompilerParams(
            dimension_semantics=("parallel",), vmem_limit_bytes=VMEM_LIMIT_BYTES),
        name="proj_ab_sample",
    )(x, g, w)


def _proj_ab_prompt_kernel(x_ref, g_ref, w_ref, *rest):
    u_ref, qT_ref, kn_ref, kT_ref, vT_ref = rest[-5:]
    h = _rms(x_ref[...], g_ref[...]).astype(MXU_DTYPE)
    p = jnp.dot(h, w_ref[...], preferred_element_type=F32)
    u_ref[...] = p[:, :POOL_WIDTH]
    q = p[:, POOL_WIDTH:POOL_WIDTH + SB_WIDTH] * (SB_HEAD_DIM ** -0.5)
    k = p[:, POOL_WIDTH + SB_WIDTH:POOL_WIDTH + 2 * SB_WIDTH]
    v = p[:, POOL_WIDTH + 2 * SB_WIDTH:]
    qT_ref[0] = q.T.astype(MXU_DTYPE)
    kn_ref[0] = k.astype(MXU_DTYPE)
    kT_ref[0] = k.T
    vT_ref[0] = v.T


def _slot_alias(prev, first_input, first_output):
    if prev is None:
        return [], {}
    specs = [pl.BlockSpec(memory_space=pl.ANY)] * len(prev)
    return specs, {first_input + n: first_output + n for n in range(len(prev))}


def _proj_ab_prompt(x, g, w, batch, seq, slot, n_slots, prev):
    nt = seq // TOKEN_TILE
    alias_specs, aliases = _slot_alias(prev, 3, 3)
    stacked = jax.ShapeDtypeStruct((n_slots, batch, SB_WIDTH, seq), F32)
    stacked_spec = pl.BlockSpec((None, 1, SB_WIDTH, TOKEN_TILE), lambda b, i: (slot, b, 0, i))
    return pl.pallas_call(
        _proj_ab_prompt_kernel,
        out_shape=(jax.ShapeDtypeStruct((batch * seq, POOL_WIDTH), F32),
                   jax.ShapeDtypeStruct((batch, SB_WIDTH, seq), MXU_DTYPE),
                   jax.ShapeDtypeStruct((batch, seq, SB_WIDTH), MXU_DTYPE),
                   stacked, stacked),
        grid=(batch, nt),
        in_specs=[
            pl.BlockSpec((TOKEN_TILE, D_MODEL), lambda b, i: (b * nt + i, 0)),
            pl.BlockSpec((1, D_MODEL), lambda b, i: (0, 0)),
            pl.BlockSpec((D_MODEL, AB_IN), lambda b, i: (0, 0), pipeline_mode=pl.Buffered(1)),
        ] + alias_specs,
        out_specs=(pl.BlockSpec((TOKEN_TILE, POOL_WIDTH), lambda b, i: (b * nt + i, 0)),
                   pl.BlockSpec((1, SB_WIDTH, TOKEN_TILE), lambda b, i: (b, 0, i)),
                   pl.BlockSpec((1, TOKEN_TILE, SB_WIDTH), lambda b, i: (b, i, 0)),
                   stacked_spec, stacked_spec),
        input_output_aliases=aliases,
        compiler_params=pltpu.CompilerParams(
            dimension_semantics=("parallel", "parallel"), vmem_limit_bytes=VMEM_LIMIT_BYTES),
        name="proj_ab_prompt",
    )(x, g, w, *(prev or ()))


def _split_hi_lo(x):
    hi = x.astype(MXU_DTYPE)
    lo = (x - hi.astype(F32)).astype(MXU_DTYPE)
    return hi, lo


def _head_rms(x, seg_mean, seg_expand, gain):
    hi, lo = _split_hi_lo(x * x)
    ms = jnp.dot(hi, seg_mean, preferred_element_type=F32) + jnp.dot(lo, seg_mean, preferred_element_type=F32)
    hi, lo = _split_hi_lo(ms)
    ms = jnp.dot(hi, seg_expand, preferred_element_type=F32) + jnp.dot(lo, seg_expand, preferred_element_type=F32)
    return x * lax.rsqrt(ms + EPS) * gain


SEG_LANES = 128
C_OFF_K = C_WIDTH
C_OFF_V = C_OFF_K + C_KV_WIDTH
C_OFF_QI = C_OFF_V + C_KV_WIDTH
C_OFF_TAIL = C_OFF_QI + IDX_HEADS * IDX_DIM
C_TAIL = C_IN_PAD - C_OFF_TAIL


def _proj_c_parts(x_ref, g_ref, w_ref, gq_ref, gk_ref, segq_ref, segk_ref):
    h = _rms(x_ref[...], g_ref[...]).astype(MXU_DTYPE)
    p = jnp.dot(h, w_ref[...], preferred_element_type=F32)
    q = _head_rms(p[:, :C_OFF_K], segq_ref[...], segk_ref[...], gq_ref[...])
    k = _head_rms(p[:, C_OFF_K:C_OFF_V], segq_ref[:C_KV_WIDTH, :], segk_ref[:, :C_KV_WIDTH], gk_ref[...])
    return q, k, p[:, C_OFF_V:C_OFF_QI], p[:, C_OFF_QI:C_OFF_TAIL], p[:, C_OFF_TAIL:]


def _proj_c_sample_kernel(x_ref, g_ref, w_ref, gq_ref, gk_ref, segq_ref, segk_ref,
                          q_ref, k_ref, v_ref, qi_ref, tail_ref):
    q, k, v, qi, tail = _proj_c_parts(x_ref, g_ref, w_ref, gq_ref, gk_ref, segq_ref, segk_ref)
    q_ref[...] = q
    k_ref[...] = k
    v_ref[...] = v
    qi_ref[...] = qi
    tail_ref[...] = tail


def _proj_c_in_specs(row_block):
    const = lambda *_: (0, 0)
    return [
        pl.BlockSpec((TOKEN_TILE, D_MODEL), row_block),
        pl.BlockSpec((1, D_MODEL), const),
        pl.BlockSpec((D_MODEL, C_IN_PAD), const, pipeline_mode=pl.Buffered(1)),
        pl.BlockSpec((1, C_WIDTH), const),
        pl.BlockSpec((1, C_KV_WIDTH), const),
        pl.BlockSpec((C_WIDTH, SEG_LANES), const),
        pl.BlockSpec((SEG_LANES, C_WIDTH), const),
    ]


def _proj_c_sample(x, g, w, gq, gk, segq, segk, row0, rows):
    first = row0 // TOKEN_TILE
    widths = (C_WIDTH, C_KV_WIDTH, C_KV_WIDTH, IDX_HEADS * IDX_DIM, C_TAIL)
    return pl.pallas_call(
        _proj_c_sample_kernel,
        out_shape=tuple(jax.ShapeDtypeStruct((rows, n), F32) for n in widths),
        grid=(rows // TOKEN_TILE,),
        in_specs=_proj_c_in_specs(lambda i: (first + i, 0)),
        out_specs=tuple(pl.BlockSpec((TOKEN_TILE, n), lambda i: (i, 0)) for n in widths),
        compiler_params=pltpu.CompilerParams(
            dimension_semantics=("parallel",), vmem_limit_bytes=VMEM_LIMIT_BYTES),
        name="proj_c_sample",
    )(x, g, w, gq, gk, segq, segk)


def _proj_c_prompt_kernel(x_ref, g_ref, w_ref, gq_ref, gk_ref, segq_ref, segk_ref, *rest):
    qT_ref, kn_ref, qiT_ref, wiT_ref, ki3_ref, kT_ref, vT_ref, kiT_ref = rest[-8:]
    q, k, v, qi, tail = _proj_c_parts(x_ref, g_ref, w_ref, gq_ref, gk_ref, segq_ref, segk_ref)
    tm = q.shape[0]
    qT_ref[0] = (q * (C_HEAD_DIM ** -0.5 * LOG2E)).T.astype(MXU_DTYPE).reshape(C_HEADS, C_HEAD_DIM, tm)
    kn_ref[0] = k.astype(MXU_DTYPE)
    kT_ref[0] = k.T
    vT_ref[0] = v.T
    qi_hi, qi_lo = _split_hi_lo((qi * (IDX_DIM ** -0.5)).T)
    for hh in range(IDX_HEADS):
        rows = slice(hh * IDX_DIM, (hh + 1) * IDX_DIM)
        qiT_ref[0, hh, 0:IDX_DIM] = qi_hi[rows]
        qiT_ref[0, hh, IDX_DIM:2 * IDX_DIM] = qi_lo[rows]
        qiT_ref[0, hh, 2 * IDX_DIM:] = qi_hi[rows]
    tail_t = tail.T
    kiT_ref[0] = tail_t[:IDX_DIM]
    wiT_ref[0] = tail_t[IDX_DIM:IDX_DIM + IDX_HEADS]
    ki_hi, ki_lo = _split_hi_lo(tail[:, :IDX_DIM])
    ki3_ref[0] = jnp.concatenate([ki_hi, ki_hi, ki_lo], axis=1)


def _proj_c_prompt(x, g, w, gq, gk, segq, segk, batch, seq, slot, n_slots, prev):
    nt = seq // TOKEN_TILE
    tm = TOKEN_TILE
    alias_specs, aliases = _slot_alias(prev, 7, 5)

    def stacked(width):
        return (jax.ShapeDtypeStruct((n_slots, batch, width, seq), F32),
                pl.BlockSpec((None, 1, width, tm), lambda b, i: (slot, b, 0, i)))

    (kT_s, kT_b), (vT_s, vT_b), (kiT_s, kiT_b) = stacked(C_KV_WIDTH), stacked(C_KV_WIDTH), stacked(IDX_DIM)
    return pl.pallas_call(
        _proj_c_prompt_kernel,
        out_shape=(jax.ShapeDtypeStruct((batch, C_HEADS, C_HEAD_DIM, seq), MXU_DTYPE),
                   jax.ShapeDtypeStruct((batch, seq, C_KV_WIDTH), MXU_DTYPE),
                   jax.ShapeDtypeStruct((batch, IDX_HEADS, 3 * IDX_DIM, seq), MXU_DTYPE),
                   jax.ShapeDtypeStruct((batch, IDX_HEADS, seq), F32),
                   jax.ShapeDtypeStruct((batch, seq, 3 * IDX_DIM), MXU_DTYPE),
                   kT_s, vT_s, kiT_s),
        grid=(batch, nt),
        in_specs=_proj_c_in_specs(lambda b, i: (b * nt + i, 0)) + alias_specs,
        out_specs=(pl.BlockSpec((1, C_HEADS, C_HEAD_DIM, tm), lambda b, i: (b, 0, 0, i)),
                   pl.BlockSpec((1, tm, C_KV_WIDTH), lambda b, i: (b, i, 0)),
                   pl.BlockSpec((1, IDX_HEADS, 3 * IDX_DIM, tm), lambda b, i: (b, 0, 0, i)),
                   pl.BlockSpec((1, IDX_HEADS, tm), lambda b, i: (b, 0, i)),
                   pl.BlockSpec((1, tm, 3 * IDX_DIM), lambda b, i: (b, i, 0)),
                   kT_b, vT_b, kiT_b),
        input_output_aliases=aliases,
        compiler_params=pltpu.CompilerParams(
            dimension_semantics=("parallel", "parallel"), vmem_limit_bytes=VMEM_LIMIT_BYTES),
        name="proj_c_prompt",
    )(x, g, w, gq, gk, segq, segk, *(prev or ()))


def _out_proj_kernel(n_parts, x_ref, *rest):
    parts, w_ref, o_ref = rest[:n_parts], rest[n_parts], rest[-1]
    acc = x_ref[...]
    off = 0
    for part in parts:
        width = part.shape[1]
        acc = acc + jnp.dot(part[...].astype(MXU_DTYPE), w_ref[off:off + width, :],
                            preferred_element_type=F32)
        off += width
    o_ref[...] = acc


def _out_proj(x, parts, w, row0, prev=None):
    rows = parts[0].shape[0]
    first = row0 // TOKEN_TILE
    x_tile = pl.BlockSpec((TOKEN_TILE, D_MODEL), lambda i: (first + i, 0))
    alias_specs, aliases = _slot_alias(None if prev is None else (prev,), 2 + len(parts), 0)
    return pl.pallas_call(
        functools.partial(_out_proj_kernel, len(parts)),
        out_shape=jax.ShapeDtypeStruct(x.shape, F32),
        grid=(rows // TOKEN_TILE,),
        in_specs=[x_tile]
        + [pl.BlockSpec((TOKEN_TILE, p.shape[1]), lambda i: (i, 0)) for p in parts]
        + [pl.BlockSpec((D_MODEL, D_MODEL), lambda i: (0, 0), pipeline_mode=pl.Buffered(1))]
        + alias_specs,
        out_specs=x_tile,
        input_output_aliases=aliases,
        compiler_params=pltpu.CompilerParams(
            dimension_semantics=("parallel",), vmem_limit_bytes=VMEM_LIMIT_BYTES),
        name="out_proj",
    )(x, *parts, w, *(() if prev is None else (prev,)))


ATT_TILE = 128
ATT_TK = 128
ATT_TQ = 256
SB_TK = 256
SB_LOCKSTEP = 2
SB_EXIT = -104.0


def _sb_prompt_kernel(qT_ref, k_ref, vT_ref, upper_ref, o_ref):
    tk, tq = SB_TK, ATT_TQ
    qb = pl.program_id(1)
    row = lax.broadcasted_iota(jnp.int32, (tk, tq), 0)
    lane = lax.broadcasted_iota(jnp.int32, (tk, tq), 1)
    q_idx = qb * tq + lane
    upper = upper_ref[...]
    last_tile = qb * (tq // tk) + (tq // tk - 1)

    def head_step(h, ks, valid, carry, acc):
        hs = slice(h * SB_HEAD_DIM, (h + 1) * SB_HEAD_DIM)
        kb = k_ref[0, pl.ds(ks, tk), hs]
        z = jnp.dot(kb, qT_ref[0, hs, :], preferred_element_type=F32)
        sp = jnp.maximum(z, 0.0) + jnp.log(1.0 + jnp.exp(-jnp.abs(z)))
        lr = jnp.where(valid, -sp, 0.0)
        hi, lo = _split_hi_lo(lr)
        between = (jnp.dot(upper, hi, preferred_element_type=F32)
                   + jnp.dot(upper, lo, preferred_element_type=F32) + carry)
        a = jnp.where(valid, jnp.exp(z - sp + between), 0.0)
        vb = vT_ref[0, hs, pl.ds(ks, tk)].astype(MXU_DTYPE)
        acc = acc + jnp.dot(vb, a.astype(MXU_DTYPE), preferred_element_type=F32)
        return carry + jnp.sum(lr, axis=0, keepdims=True), acc

    for h0 in range(0, SB_HEADS, SB_LOCKSTEP):
        heads = range(h0, h0 + SB_LOCKSTEP)

        def body(state, heads=heads):
            j, _, carries, accs = state
            ks = pl.multiple_of(j * tk, tk)
            valid = (ks + row) < q_idx
            out = [head_step(h, ks, valid, c, a) for h, c, a in zip(heads, carries, accs)]
            carries = tuple(o[0] for o in out)
            top = functools.reduce(jnp.maximum, [jnp.max(c) for c in carries])
            return j - 1, top, carries, tuple(o[1] for o in out)

        def cond(state):
            j, top, _, _ = state
            return jnp.logical_and(j >= 0, top > SB_EXIT)

        init = (last_tile, jnp.float32(0.0),
                tuple(jnp.zeros((1, tq), F32) for _ in heads),
                tuple(jnp.zeros((SB_HEAD_DIM, tq), F32) for _ in heads))
        _, _, _, accs = lax.while_loop(cond, body, init)
        o_ref[0, :, h0 * SB_HEAD_DIM:(h0 + SB_LOCKSTEP) * SB_HEAD_DIM] = jnp.concatenate(accs, axis=0).T


def _strict_upper(n):
    i = np.arange(n)
    return jnp.asarray((i[None, :] > i[:, None]).astype(np.float32), MXU_DTYPE)


def _strict_lower(n):
    i = np.arange(n)
    return jnp.asarray((i[None, :] < i[:, None]).astype(np.float32), MXU_DTYPE)


def _sb_prompt(qT, k, vT_all, slot):
    b, width, t = qT.shape
    assert (SB_LOCKSTEP * SB_HEAD_DIM) % 128 == 0
    return pl.pallas_call(
        _sb_prompt_kernel,
        out_shape=jax.ShapeDtypeStruct((b, t, width), F32),
        grid=(b, t // ATT_TQ),
        in_specs=[
            pl.BlockSpec((1, width, ATT_TQ), lambda i, j: (i, 0, j)),
            pl.BlockSpec((1, t, width), lambda i, j: (i, 0, 0)),
            pl.BlockSpec((None, 1, width, t), lambda i, j: (slot, i, 0, 0)),
            pl.BlockSpec((SB_TK, SB_TK), lambda i, j: (0, 0)),
        ],
        out_specs=pl.BlockSpec((1, ATT_TQ, width), lambda i, j: (i, j, 0)),
        compiler_params=pltpu.CompilerParams(
            dimension_semantics=("parallel", "arbitrary"), vmem_limit_bytes=VMEM_LIMIT_BYTES),
        name="sb_prompt",
    )(qT, k, vT_all, _strict_upper(SB_TK))


INT_MIN = -2 ** 31
COUNT_CHUNK = 4 * ATT_TK
LOG2E = math.log2(math.e)
C_TQ = 256
REL_NEAR_TILES = C_TQ // ATT_TK + 1
REL_LAST_BUCKET_FROM = math.ceil(REL_MAX_EXACT * (REL_MAX_DIST / REL_MAX_EXACT) ** (
    (REL_BUCKETS - 1 - REL_MAX_EXACT) / (REL_BUCKETS - REL_MAX_EXACT)))
assert ATT_TK + 1 >= REL_LAST_BUCKET_FROM
SUM_ROWS = 16


def _bucket_of(dist):
    n = jnp.maximum(dist, 0)
    nf = jnp.maximum(n, 1).astype(F32)
    large = REL_MAX_EXACT + (jnp.log(nf / REL_MAX_EXACT) / math.log(REL_MAX_DIST / REL_MAX_EXACT)
                             * (REL_BUCKETS - REL_MAX_EXACT)).astype(jnp.int32)
    return jnp.where(n < REL_MAX_EXACT, n, jnp.minimum(large, REL_BUCKETS - 1))


def _bias_tile_kernel(dec_seq, relb_ref, o_ref, s_ref):
    krow = lax.broadcasted_iota(jnp.int32, (ATT_TK, C_TQ), 0)
    qlane = lax.broadcasted_iota(jnp.int32, (ATT_TK, C_TQ), 1)
    for c in range(REL_NEAR_TILES):
        bucket = _bucket_of((1 - c) * ATT_TK + qlane - krow)
        for h in range(C_HEADS):
            tile = jnp.zeros((ATT_TK, C_TQ), F32)
            for b in range(REL_BUCKETS):
                tile = jnp.where(bucket == b, relb_ref[b, h], tile)
            o_ref[h, c] = (tile - relb_ref[REL_BUCKETS - 1, h]) * LOG2E
    t = ATT_TILE
    row = lax.broadcasted_iota(jnp.int32, (t, t), 0)
    lane = lax.broadcasted_iota(jnp.int32, (t, t), 1)
    row_head = row >> int(math.log2(dec_seq))
    row_t = row & (dec_seq - 1)
    buckets = (_bucket_of(t + row_t - lane), _bucket_of(row_t - lane),
               jnp.full((t, t), REL_BUCKETS - 1, jnp.int32))
    tiles = [jnp.zeros((t, t), F32) for _ in buckets]
    for b in range(REL_BUCKETS):
        by_head = jnp.zeros((t, t), F32)
        for h in range(C_HEADS):
            by_head = jnp.where(row_head == h, relb_ref[b, h], by_head)
        tiles = [jnp.where(bk == b, by_head, tl) for bk, tl in zip(buckets, tiles)]
    for c, tl in enumerate(tiles):
        s_ref[c] = tl


def _bias_tiles(rel_bias, dec_seq):
    assert C_HEADS * dec_seq == ATT_TILE
    return pl.pallas_call(
        functools.partial(_bias_tile_kernel, dec_seq),
        out_shape=(jax.ShapeDtypeStruct((C_HEADS, REL_NEAR_TILES, ATT_TK, C_TQ), F32),
                   jax.ShapeDtypeStruct((3, ATT_TILE, ATT_TILE), F32)),
        in_specs=[pl.BlockSpec(memory_space=pltpu.SMEM)],
        name="rel_bias_tiles",
    )(rel_bias)


def _sortable(s):
    bits = lax.bitcast_convert_type(s, jnp.int32)
    return bits ^ ((bits >> 31) & 0x7FFFFFFF)


def _c_prompt_kernel(topk, qiT_ref, wiT_ref, ki_ref, qT_ref, k_ref, vT_ref, btile_ref,
                     lower_ref, o_ref, key_ref, thr_ref, m_ref, l_ref, acc_ref):
    tk, tq = ATT_TK, C_TQ
    qb = pl.program_id(1)
    n_tiles = (qb + 1) * (tq // tk)
    n_chunks = (n_tiles * tk + COUNT_CHUNK - 1) // COUNT_CHUNK
    row = lax.broadcasted_iota(jnp.int32, (tk, tq), 0)
    lane = lax.broadcasted_iota(jnp.int32, (tk, tq), 1)
    q_idx = qb * tq + lane
    w = wiT_ref[0] * (IDX_HEADS ** -0.5)

    def score_tile(ks):
        kib = ki_ref[0, pl.ds(ks, tk), :]
        s = jnp.zeros((tk, tq), F32)
        for hh in range(IDX_HEADS):
            d = jnp.dot(kib, qiT_ref[0, hh], preferred_element_type=F32)
            s = s + jnp.maximum(d, 0.0) * w[hh:hh + 1, :]
        s = jnp.where(ks + row <= q_idx, s, -jnp.inf)
        key_ref[pl.ds(ks, tk), :] = _sortable(s)

    def score_chunk(c, _):
        for i in range(COUNT_CHUNK // tk):
            score_tile(pl.multiple_of(c * COUNT_CHUNK + i * tk, tk))
        return 0

    lax.fori_loop(0, n_chunks, score_chunk, 0)

    def count(pred, thr):
        def chunk(c, cnt):
            base = pl.multiple_of(c * COUNT_CHUNK, COUNT_CHUNK)
            ind = jnp.where(pred(key_ref[pl.ds(base, COUNT_CHUNK), :], thr), 1, 0)
            return cnt + jnp.sum(ind.reshape(COUNT_CHUNK // 8, 8, tq), axis=0)
        cnt = lax.fori_loop(0, n_chunks, chunk, jnp.zeros((8, tq), jnp.int32))
        return jnp.sum(cnt, axis=0, keepdims=True)

    def bit_step(i, t_u):
        cand_u = t_u | lax.shift_left(jnp.int32(1), 31 - i)
        cnt = count(lambda x, thr: x >= thr, cand_u ^ INT_MIN)
        return jnp.where(cnt >= topk, cand_u, t_u)

    thr = lax.fori_loop(0, 32, bit_step, jnp.zeros((1, tq), jnp.int32)) ^ INT_MIN
    n_ge = count(lambda x, t: x >= t, thr)
    thr_ref[...] = thr

    @pl.when(jnp.max(jnp.abs(n_ge - topk)) > 0)
    def _():
        need = (topk - count(lambda x, t: x > t, thr)).astype(F32)

        def select_block(j, seen):
            ks = pl.multiple_of(j * tk, tk)
            blk = key_ref[pl.ds(ks, tk), :]
            eq = jnp.where(blk == thr, 1.0, 0.0)
            rank = jnp.dot(lower_ref[...], eq.astype(MXU_DTYPE), preferred_element_type=F32) + seen
            tie = jnp.where(rank < need, eq, 0.0)
            sel = jnp.where(blk > thr, 1.0, tie)
            sel = jnp.where(ks + row <= q_idx, sel, 0.0)
            key_ref[pl.ds(ks, tk), :] = sel.astype(jnp.int32)
            return seen + jnp.sum(eq, axis=0, keepdims=True)

        lax.fori_loop(0, n_tiles, select_block, jnp.zeros((1, tq), F32))
        thr_ref[...] = jnp.ones((1, tq), jnp.int32)

    m_ref[...] = jnp.full(m_ref.shape, NEG, F32)
    l_ref[...] = jnp.zeros(l_ref.shape, F32)
    acc_ref[...] = jnp.zeros(acc_ref.shape, F32)
    ones = jnp.ones((SUM_ROWS, tk), MXU_DTYPE)

    def attend(j, near):
        ks = pl.multiple_of(j * tk, tk)
        sel = key_ref[pl.ds(ks, tk), :] >= thr_ref[...]
        for n in range(C_KV_HEADS):
            ns = slice(n * C_HEAD_DIM, (n + 1) * C_HEAD_DIM)
            kb = k_ref[0, pl.ds(ks, tk), ns]
            vb = vT_ref[0, ns, pl.ds(ks, tk)].astype(MXU_DTYPE)
            for g in range(C_GROUP):
                h = n * C_GROUP + g
                lg = jnp.dot(kb, qT_ref[0, h], preferred_element_type=F32)
                if near is not None:
                    lg = lg + btile_ref[h, near]
                lg = jnp.where(sel, lg, NEG)
                m_old = m_ref[h]
                m_new = jnp.maximum(m_old, jnp.max(lg, axis=0, keepdims=True))
                p = jnp.exp2(lg - m_new).astype(MXU_DTYPE)
                alpha = jnp.exp2(m_old - m_new)
                l_ref[h] = alpha * l_ref[h] + jnp.dot(ones, p, preferred_element_type=F32)
                acc_ref[h] = alpha * acc_ref[h] + jnp.dot(vb, p, preferred_element_type=F32)
                m_ref[h] = m_new

    def far_tile(j, _):
        attend(j, None)
        return 0

    first_near = n_tiles - REL_NEAR_TILES
    lax.fori_loop(0, jnp.maximum(first_near, 0), far_tile, 0)
    for c in range(REL_NEAR_TILES):
        if c == 0:
            pl.when(first_near >= 0)(functools.partial(attend, first_near, 0))
        else:
            attend(first_near + c, c)
    for h in range(0, C_HEADS, 2):
        pair = jnp.concatenate([acc_ref[h] / l_ref[h][0:1], acc_ref[h + 1] / l_ref[h + 1][0:1]], axis=0)
        o_ref[0, :, h * C_HEAD_DIM:(h + 2) * C_HEAD_DIM] = pair.T


def _c_prompt(qT, k, vT_all, slot, qiT, wiT, ki3, btiles):
    b, _, _, t = qT.shape
    topk = min(TOPK_MAX, t // 4)
    assert t % COUNT_CHUNK == 0 and COUNT_CHUNK > topk and t % C_TQ == 0
    tq, tk = C_TQ, ATT_TK
    return pl.pallas_call(
        functools.partial(_c_prompt_kernel, topk),
        out_shape=jax.ShapeDtypeStruct((b, t, C_WIDTH), F32),
        grid=(b, t // tq),
        in_specs=[
            pl.BlockSpec((1, IDX_HEADS, 3 * IDX_DIM, tq), lambda i, j: (i, 0, 0, j)),
            pl.BlockSpec((1, IDX_HEADS, tq), lambda i, j: (i, 0, j)),
            pl.BlockSpec((1, t, 3 * IDX_DIM), lambda i, j: (i, 0, 0)),
            pl.BlockSpec((1, C_HEADS, C_HEAD_DIM, tq), lambda i, j: (i, 0, 0, j)),
            pl.BlockSpec((1, t, C_KV_WIDTH), lambda i, j: (i, 0, 0)),
            pl.BlockSpec((None, 1, C_KV_WIDTH, t), lambda i, j: (slot, i, 0, 0)),
            pl.BlockSpec((C_HEADS, REL_NEAR_TILES, tk, tq), lambda i, j: (0, 0, 0, 0),
                         pipeline_mode=pl.Buffered(1)),
            pl.BlockSpec((tk, tk), lambda i, j: (0, 0)),
        ],
        out_specs=pl.BlockSpec((1, tq, C_WIDTH), lambda i, j: (i, j, 0)),
        scratch_shapes=[
            pltpu.VMEM((t, tq), jnp.int32),
            pltpu.VMEM((1, tq), jnp.int32),
            pltpu.VMEM((C_HEADS, 1, tq), F32),
            pltpu.VMEM((C_HEADS, SUM_ROWS, tq), F32),
            pltpu.VMEM((C_HEADS, C_HEAD_DIM, tq), F32),
        ],
        compiler_params=pltpu.CompilerParams(
            dimension_semantics=("parallel", "arbitrary"), vmem_limit_bytes=VMEM_LIMIT_BYTES),
        name="c_prompt",
    )(qiT, wiT, ki3, qT, k, vT_all, btiles, _strict_lower(tk))


_NT = (((1,), (1,)), ((), ()))
_NN = (((1,), (0,)), ((), ()))
SAMPLE_RADIX_BITS = 4


def _pad_rows(x, rows):
    return jnp.concatenate([x, jnp.zeros((rows - x.shape[0], x.shape[1]), x.dtype)], axis=0)


def _prefix_and_total(n, strict_before):
    i = np.arange(n)
    tri = (i[:, None] < i[None, :]) if strict_before else (i[:, None] > i[None, :])
    return jnp.asarray(np.concatenate([tri, np.ones((n, n), bool)], axis=1).astype(np.float32), MXU_DTYPE)


def _sb_sample_kernel(n_pages, resume, pt_ref, q_ref, kn_ref, vn_ref, hmask_ref, sufx_ref, *rest):
    k_pages, v_pages = rest[:n_pages], rest[n_pages:2 * n_pages]
    rest = rest[2 * n_pages:]
    if resume:
        carry_in_ref, acc_in_ref, o_ref, carry_ref, acc_ref = rest
    else:
        o_ref, carry_out_ref, acc_out_ref = rest
        carry_ref, acc_ref = carry_out_ref.at[0], acc_out_ref.at[0]
    t = q_ref.shape[1]
    rows, tk = SB_HEADS * t, PAGE_SIZE
    hmask = hmask_ref[...]
    q = q_ref[0] * (SB_HEAD_DIM ** -0.5)
    qbd = (jnp.concatenate([q] * SB_HEADS, axis=0) * hmask).astype(MXU_DTYPE)

    def step(z, weighted_values, valid):
        sp = jnp.maximum(z, 0.0) + jnp.log(1.0 + jnp.exp(-jnp.abs(z)))
        lr = -sp if valid is None else jnp.where(valid, -sp, 0.0)
        hi, lo = _split_hi_lo(lr)
        both = (jnp.dot(hi, sufx_ref[...], preferred_element_type=F32)
                + jnp.dot(lo, sufx_ref[...], preferred_element_type=F32))
        a = jnp.exp(z - sp + both[:, :tk] + carry_ref[...])
        if valid is not None:
            a = jnp.where(valid, a, 0.0)
        acc_ref[...] += weighted_values(a.astype(MXU_DTYPE))
        carry_ref[...] += both[:, tk:]

    if resume:
        carry_ref[...] = carry_in_ref[0]
        acc_ref[...] = acc_in_ref[0]
    else:
        carry_ref[...] = jnp.zeros(carry_ref.shape, F32)
        acc_ref[...] = jnp.zeros(acc_ref.shape, F32)
        row_t = lax.broadcasted_iota(jnp.int32, (rows, tk), 0) & (t - 1)
        lane = lax.broadcasted_iota(jnp.int32, (rows, tk), 1)
        kn = _pad_rows(kn_ref[0], tk).astype(MXU_DTYPE)
        vn = _pad_rows(vn_ref[0], tk).astype(MXU_DTYPE)
        step(lax.dot_general(qbd, kn, _NT, preferred_element_type=F32),
             lambda a: jnp.dot(a, vn, preferred_element_type=F32), lane < row_t)
    for p in range(n_pages):
        @pl.when(jnp.max(carry_ref[...]) > SB_EXIT)
        def _(p=p):
            kT = k_pages[p][...].reshape(SB_WIDTH, tk).astype(MXU_DTYPE)
            vT = v_pages[p][...].reshape(SB_WIDTH, tk).astype(MXU_DTYPE)
            step(jnp.dot(qbd, kT, preferred_element_type=F32),
                 lambda a: lax.dot_general(a, vT, _NT, preferred_element_type=F32), None)

    acc = acc_ref[...] * hmask
    y = acc[0:t]
    for h in range(1, SB_HEADS):
        y = y + acc[h * t:(h + 1) * t]
    o_ref[0] = y


def _feature_major(cache):
    nd = cache.ndim
    return cache.transpose((0, 1) + tuple(range(3, nd)) + (2,))


def _page_specs(pages, layer, page_shape):
    zeros = (0,) * len(page_shape)
    return [pl.BlockSpec((None, None) + page_shape, lambda i, pt, p=p: (layer, pt[i, p]) + zeros)
            for p in pages]


SB_NEAR_PAGES = 2


def _sb_sample_pass(pages, state, q, k, v, ck, cv, layer, page_table):
    b, t, _ = q.shape
    rows = SB_HEADS * t
    resume = state is not None
    page = (SB_HEADS, SB_HEAD_DIM, PAGE_SIZE)
    hmask = jnp.asarray((np.arange(rows)[:, None] // t == np.arange(SB_WIDTH)[None, :] // SB_HEAD_DIM)
                        .astype(np.float32))
    tok = pl.BlockSpec((1, t, SB_WIDTH), lambda i, pt: (i, 0, 0))
    const = lambda i, pt: (0, 0)
    carry = (jax.ShapeDtypeStruct((b, rows, PAGE_SIZE), F32),
             pl.BlockSpec((1, rows, PAGE_SIZE), lambda i, pt: (i, 0, 0)))
    acc = (jax.ShapeDtypeStruct((b, rows, SB_WIDTH), F32),
           pl.BlockSpec((1, rows, SB_WIDTH), lambda i, pt: (i, 0, 0)))
    y_shape = jax.ShapeDtypeStruct((b, t, SB_WIDTH), F32)
    return pl.pallas_call(
        functools.partial(_sb_sample_kernel, len(pages), resume),
        out_shape=y_shape if resume else (y_shape, carry[0], acc[0]),
        grid_spec=pltpu.PrefetchScalarGridSpec(
            num_scalar_prefetch=1,
            grid=(b,),
            in_specs=[tok, tok, tok,
                      pl.BlockSpec((rows, SB_WIDTH), const),
                      pl.BlockSpec((PAGE_SIZE, 2 * PAGE_SIZE), const)]
            + _page_specs(pages, layer, page) + _page_specs(pages, layer, page)
            + ([carry[1], acc[1]] if resume else []),
            out_specs=tok if resume else (tok, carry[1], acc[1]),
            scratch_shapes=([pltpu.VMEM((rows, PAGE_SIZE), F32), pltpu.VMEM((rows, SB_WIDTH), F32)]
                            if resume else []),
        ),
        compiler_params=pltpu.CompilerParams(
            dimension_semantics=("arbitrary",), vmem_limit_bytes=VMEM_LIMIT_BYTES),
        name="sb_sample_resume" if resume else "sb_sample",
    )(page_table, q, k, v, hmask, _prefix_and_total(PAGE_SIZE, False),
      *([ck] * len(pages)), *([cv] * len(pages)), *(state or ()))


def _sb_sample(q, k, v, cache_k, cache_v, layer, page_table):
    t = q.shape[1]
    assert t & (t - 1) == 0 and t <= PAGE_SIZE
    ck, cv = _feature_major(cache_k), _feature_major(cache_v)
    newest_first = tuple(reversed(range(page_table.shape[1])))
    near, far = newest_first[:SB_NEAR_PAGES], newest_first[SB_NEAR_PAGES:]
    args = (q, k, v, ck, cv, layer, page_table)
    y, carry, acc = _sb_sample_pass(near, None, *args)
    if not far:
        return y
    return lax.cond(jnp.max(carry) > SB_EXIT,
                    lambda: _sb_sample_pass(far, (carry, acc), *args), lambda: y)


def _c_sample_kernel(n_pages, topk, pt_ref, qi_ref, w_ref, q_ref, kin_ref, kn_ref, vn_ref,
                     sbias_ref, prex_ref, *rest):
    idx_pages, k_pages, v_pages = rest[:n_pages], rest[n_pages:2 * n_pages], rest[2 * n_pages:3 * n_pages]
    o_ref, sc_ref, lg_ref = rest[3 * n_pages:]
    t = kin_ref.shape[1]
    tk = PAGE_SIZE
    nb = n_pages + 1
    ih = IDX_HEADS * t
    rows = C_HEADS * t
    row8 = lax.broadcasted_iota(jnp.int32, (t, tk), 0)
    lane8 = lax.broadcasted_iota(jnp.int32, (t, tk), 1)
    new_valid = lane8 <= row8
    qi_cat = qi_ref[0]
    w = w_ref[0] * (IDX_HEADS ** -0.5)

    def scores(idx_blk, valid):
        dims = _NT if valid is not None else _NN
        hi, lo = _split_hi_lo(idx_blk)
        s = lax.dot_general(qi_cat, hi, dims, preferred_element_type=F32)
        s = s[:ih] + s[ih:] + lax.dot_general(qi_cat[:ih], lo, dims, preferred_element_type=F32)
        r = jnp.maximum(s, 0.0) * w
        sc = r[0:t]
        for hh in range(1, IDX_HEADS):
            sc = sc + r[hh * t:(hh + 1) * t]
        if valid is not None:
            sc = jnp.where(valid, sc, -jnp.inf)
        return _sortable(sc)

    for p in range(n_pages):
        sc_ref[:, p * tk:(p + 1) * tk] = scores(idx_pages[p][...], None)
    sc_ref[:, n_pages * tk:] = scores(_pad_rows(kin_ref[0], tk), new_valid)

    def count(pred, thr):
        return jnp.sum(jnp.where(pred(sc_ref[...], thr), 1.0, 0.0), axis=1, keepdims=True)

    def digit_step(i, t_u):
        shift = 32 - SAMPLE_RADIX_BITS * (i + 1)
        digit = jnp.zeros((t, 1), jnp.int32)
        for v in range(1, 2 ** SAMPLE_RADIX_BITS):
            cand_u = t_u | lax.shift_left(jnp.int32(v), shift)
            cnt = count(lambda x, thr: x >= thr, cand_u ^ INT_MIN)
            digit = digit + jnp.where(cnt >= topk, 1, 0)
        return t_u | lax.shift_left(digit, shift)

    thr = lax.fori_loop(0, 32 // SAMPLE_RADIX_BITS, digit_step, jnp.zeros((t, 1), jnp.int32)) ^ INT_MIN
    need = topk - count(lambda x, th: x > th, thr)

    seen = jnp.zeros((t, tk), F32)
    for blk in range(nb):
        x = sc_ref[:, blk * tk:(blk + 1) * tk]
        eq = jnp.where(x == thr, 1.0, 0.0)
        both = jnp.dot(eq.astype(MXU_DTYPE), prex_ref[...], preferred_element_type=F32)
        tie = jnp.where(both[:, :tk] + seen < need, eq, 0.0)
        sel = jnp.where(x > thr, 1.0, tie)
        if blk == n_pages:
            sel = jnp.where(new_valid, sel, 0.0)
        sc_ref[:, blk * tk:(blk + 1) * tk] = sel.astype(jnp.int32)
        seen = seen + both[:, tk:]

    q_rows = q_ref[0]
    grp = C_GROUP * t
    hd = C_HEAD_DIM

    def kv_block(pages, new_ref, blk, n):
        if blk < n_pages:
            return pages[blk][n].astype(MXU_DTYPE)
        return _pad_rows(new_ref[0][:, n * hd:(n + 1) * hd], tk).astype(MXU_DTYPE)

    m = jnp.full((rows, tk), NEG, F32)
    for blk in range(nb):
        bias = sbias_ref[2] if blk < n_pages - 1 else sbias_ref[blk - (n_pages - 1)]
        lg = jnp.concatenate(
            [lax.dot_general(q_rows[n * grp:(n + 1) * grp], kv_block(k_pages, kn_ref, blk, n),
                             _NN if blk < n_pages else _NT, preferred_element_type=F32)
             for n in range(C_KV_HEADS)], axis=0) + bias
        sel = jnp.concatenate([sc_ref[:, blk * tk:(blk + 1) * tk]] * C_HEADS, axis=0) != 0
        lg = jnp.where(sel, lg, NEG)
        lg_ref[:, blk * tk:(blk + 1) * tk] = lg
        m = jnp.maximum(m, lg)
    m_row = jnp.max(m, axis=1, keepdims=True)
    lsum = jnp.zeros((rows, tk), F32)
    accs = [jnp.zeros((grp, hd), F32) for _ in range(C_KV_HEADS)]
    for blk in range(nb):
        p = jnp.exp(lg_ref[:, blk * tk:(blk + 1) * tk] - m_row)
        lsum = lsum + p
        p = p.astype(MXU_DTYPE)
        accs = [acc + lax.dot_general(p[n * grp:(n + 1) * grp], kv_block(v_pages, vn_ref, blk, n),
                                      _NT if blk < n_pages else _NN, preferred_element_type=F32)
                for n, acc in enumerate(accs)]
    l_row = jnp.sum(lsum, axis=1, keepdims=True)
    for n, acc in enumerate(accs):
        o = acc / l_row[n * grp:(n + 1) * grp]
        for g in range(C_GROUP):
            o_ref[0, n * C_GROUP + g] = o[g * t:(g + 1) * t]


def _c_sample(q, k, v, qi, ki, wi, cache_k, cache_v, cache_idx, layer, page_table, sbias):
    b, t, _ = q.shape
    n_pages = page_table.shape[1]
    topk = min(TOPK_MAX, (n_pages * PAGE_SIZE + t) // 4)
    rows = C_HEADS * t
    ih = IDX_HEADS * t
    assert rows == ATT_TILE and t <= PAGE_SIZE
    q_rows = ((q * C_HEAD_DIM ** -0.5).reshape(b, t, C_HEADS, C_HEAD_DIM).transpose(0, 2, 1, 3)
              .reshape(b, rows, C_HEAD_DIM).astype(MXU_DTYPE))
    qi_rows = ((qi * IDX_DIM ** -0.5).reshape(b, t, IDX_HEADS, IDX_DIM).transpose(0, 2, 1, 3)
               .reshape(b, ih, IDX_DIM))
    qi_cat = jnp.concatenate(_split_hi_lo(qi_rows), axis=1)
    w_col = wi.transpose(0, 2, 1).reshape(b, ih, 1)
    kv_page = (C_KV_HEADS, C_HEAD_DIM, PAGE_SIZE)
    per_seq = lambda r, c: pl.BlockSpec((1, r, c), lambda i, pt: (i, 0, 0))
    const2 = lambda i, pt: (0, 0)
    o = pl.pallas_call(
        functools.partial(_c_sample_kernel, n_pages, topk),
        out_shape=jax.ShapeDtypeStruct((b, C_HEADS, t, C_HEAD_DIM), F32),
        grid_spec=pltpu.PrefetchScalarGridSpec(
            num_scalar_prefetch=1,
            grid=(b,),
            in_specs=[per_seq(2 * ih, IDX_DIM), per_seq(ih, 1), per_seq(rows, C_HEAD_DIM),
                      per_seq(t, IDX_DIM), per_seq(t, C_KV_WIDTH), per_seq(t, C_KV_WIDTH),
                      pl.BlockSpec((3, ATT_TILE, ATT_TILE), lambda i, pt: (0, 0, 0)),
                      pl.BlockSpec((PAGE_SIZE, 2 * PAGE_SIZE), const2)]
            + _page_specs(range(n_pages), layer, (IDX_DIM, PAGE_SIZE))
            + _page_specs(range(n_pages), layer, kv_page) + _page_specs(range(n_pages), layer, kv_page),
            out_specs=pl.BlockSpec((1, C_HEADS, t, C_HEAD_DIM), lambda i, pt: (i, 0, 0, 0)),
            scratch_shapes=[pltpu.VMEM((t, (n_pages + 1) * PAGE_SIZE), jnp.int32),
                            pltpu.VMEM((rows, (n_pages + 1) * PAGE_SIZE), F32)],
        ),
        compiler_params=pltpu.CompilerParams(
            dimension_semantics=("arbitrary",), vmem_limit_bytes=VMEM_LIMIT_BYTES),
        name="c_sample",
    )(page_table, qi_cat, w_col, q_rows, ki, k, v, sbias, _prefix_and_total(PAGE_SIZE, True),
      *([_feature_major(cache_idx)] * n_pages), *([_feature_major(cache_k)] * n_pages),
      *([_feature_major(cache_v)] * n_pages))
    return o.transpose(0, 2, 1, 3).reshape(b, t, C_WIDTH)


POOL_HALO = POOL_BUF + 1


def _pool_tile(ext, first_pos, w_ref, scale_ref):
    t = ext.shape[0] - POOL_HALO
    pos1 = lax.broadcasted_iota(jnp.int32, (t, POOL_GROUP), 0) + (first_pos + 1)
    outs = []
    for g, w in enumerate(POOL_WINDOWS):
        lanes = slice(g * POOL_GROUP, (g + 1) * POOL_GROUP)
        e = ext[:, lanes]
        s, span = e, 1
        while span < w:
            s = s + pltpu.roll(s, span, 0)
            span *= 2
        cnt = jnp.minimum(pos1, w).astype(F32)
        d = s[POOL_HALO:] / cnt - e[POOL_HALO:]
        y = jnp.dot(d.astype(MXU_DTYPE), w_ref[g], preferred_element_type=F32)
        outs.append(y * scale_ref[:, lanes])
    return jnp.concatenate(outs, axis=1)


def _pool_prompt_kernel(u_ref, halo_ref, w_ref, scale_ref, o_ref):
    i = pl.program_id(1)
    halo = jnp.where(i > 0, halo_ref[...], 0.0)
    ext = jnp.concatenate([halo, u_ref[...]], axis=0)
    o_ref[...] = _pool_tile(ext, i * u_ref.shape[0], w_ref, scale_ref)


def _pool_prompt(u, w_pool, pool_scale, batch, seq):
    assert max(POOL_WINDOWS) <= POOL_HALO and TOKEN_TILE % POOL_HALO == 0
    nt = seq // TOKEN_TILE
    per_tile = TOKEN_TILE // POOL_HALO
    return pl.pallas_call(
        _pool_prompt_kernel,
        out_shape=jax.ShapeDtypeStruct(u.shape, F32),
        grid=(batch, nt),
        in_specs=[
            pl.BlockSpec((TOKEN_TILE, POOL_WIDTH), lambda b, i: (b * nt + i, 0)),
            pl.BlockSpec((POOL_HALO, POOL_WIDTH),
                         lambda b, i: (jnp.maximum((b * nt + i) * per_tile - 1, 0), 0)),
            pl.BlockSpec(w_pool.shape, lambda b, i: (0, 0, 0)),
            pl.BlockSpec((1, POOL_WIDTH), lambda b, i: (0, 0)),
        ],
        out_specs=pl.BlockSpec((TOKEN_TILE, POOL_WIDTH), lambda b, i: (b * nt + i, 0)),
        compiler_params=pltpu.CompilerParams(
            dimension_semantics=("parallel", "arbitrary"), vmem_limit_bytes=VMEM_LIMIT_BYTES),
        name="pool_prompt",
    )(u, u, w_pool.astype(MXU_DTYPE), pool_scale[None])


POOL_SAMPLE_GROUP = 8


def _pool_sample_kernel(first_pos, ext_ref, w_ref, scale_ref, o_ref):
    for s in range(ext_ref.shape[0]):
        o_ref[s] = _pool_tile(ext_ref[s], first_pos, w_ref, scale_ref)


def _pool_sample(u_ext, first_pos, w_pool, pool_scale):
    n, rows, _ = u_ext.shape
    t = rows - POOL_HALO
    grp = POOL_SAMPLE_GROUP
    return pl.pallas_call(
        functools.partial(_pool_sample_kernel, first_pos),
        out_shape=jax.ShapeDtypeStruct((n, t, POOL_WIDTH), F32),
        grid=(n // grp,),
        in_specs=[
            pl.BlockSpec((grp, rows, POOL_WIDTH), lambda i: (i, 0, 0)),
            pl.BlockSpec(w_pool.shape, lambda i: (0, 0, 0)),
            pl.BlockSpec((1, POOL_WIDTH), lambda i: (0, 0)),
        ],
        out_specs=pl.BlockSpec((grp, t, POOL_WIDTH), lambda i: (i, 0, 0)),
        compiler_params=pltpu.CompilerParams(
            dimension_semantics=("parallel",), vmem_limit_bytes=VMEM_LIMIT_BYTES),
        name="pool_sample",
    )(u_ext, w_pool.astype(MXU_DTYPE), pool_scale[None])


def _segment_matrices():
    member = (np.arange(C_WIDTH)[:, None] // C_HEAD_DIM == np.arange(SEG_LANES)[None, :]).astype(np.float32)
    return jnp.asarray(member / C_HEAD_DIM, MXU_DTYPE), jnp.asarray(member.T, MXU_DTYPE)


def kernel(x_prompt, x_sample, state_pool, cache_b_k, cache_b_v, cache_c_k, cache_c_v, cache_c_idx, page_table,
           g_ffn, w_ffn_gate, w_ffn_up, w_ffn_down, g_mix, w_in_ab, w_pool, pool_scale, w_out_ab,
           w_in_c, g_q, g_k, w_out_c, rel_bias):
    bp, tp, _ = x_prompt.shape
    bs, ts, _ = x_sample.shape
    mp, ms = bp * tp, bs * ts
    n_even, n_odd = (DEPTH + 1) // 2, DEPTH // 2
    past = page_table.shape[1] * PAGE_SIZE

    wg = w_ffn_gate.astype(MXU_DTYPE)
    wu = w_ffn_up.astype(MXU_DTYPE)
    wd = w_ffn_down.astype(MXU_DTYPE)
    w_ab = w_in_ab.astype(MXU_DTYPE)
    w_oab = w_out_ab.astype(MXU_DTYPE)
    w_c = jnp.pad(w_in_c, ((0, 0), (0, 0), (0, C_IN_PAD - C_IN))).astype(MXU_DTYPE)
    w_oc = w_out_c.astype(MXU_DTYPE)
    segq, segk = _segment_matrices()
    btiles, sbias = _bias_tiles(rel_bias, ts)

    pool_p, pool_s, kbs, vbs, kcs, vcs, ics = [], [], [], [], [], [], []
    kv_b = kv_c = None
    for layer in range(DEPTH):
        j = layer // 2
        ffn0 = (g_ffn[layer, 0][None], wg[layer, 0], wu[layer, 0], wd[layer, 0])
        if layer == 0:
            x = _ffn(x_prompt.reshape(mp, D_MODEL), *ffn0, out_rows=mp + ms)
            x = _ffn(x_sample.reshape(ms, D_MODEL), *ffn0, out_rows=mp + ms, out_row0=mp, prev=x)
        else:
            x = _ffn(x, *ffn0)
        g = g_mix[layer][None]
        if layer % 2 == 0:
            u_p, qT_p, k_p, *kv_b = _proj_ab_prompt(x, g, w_ab[j], bp, tp, j, n_even, kv_b)
            u_s, q_s, k_s, v_s = (a.reshape(bs, ts, -1) for a in _proj_ab_sample(x, g, w_ab[j], mp, ms))
            u_ext = jnp.concatenate([jnp.zeros((bs, POOL_HALO - POOL_BUF, POOL_WIDTH), F32),
                                     state_pool[j], u_s], axis=1)
            parts_p = (_pool_prompt(u_p, w_pool[j], pool_scale[j], bp, tp),
                       _sb_prompt(qT_p, k_p, kv_b[1], j).reshape(mp, SB_WIDTH))
            parts_s = (_pool_sample(u_ext, past, w_pool[j], pool_scale[j]).reshape(ms, POOL_WIDTH),
                       _sb_sample(q_s, k_s, v_s, cache_b_k, cache_b_v, j, page_table).reshape(ms, SB_WIDTH))
            pool_p.append(u_p.reshape(bp, tp, POOL_WIDTH)[:, tp - POOL_BUF:])
            pool_s.append(u_ext[:, -POOL_BUF:])
            kbs.append(k_s.reshape(bs, ts, SB_HEADS, SB_HEAD_DIM))
            vbs.append(v_s.reshape(bs, ts, SB_HEADS, SB_HEAD_DIM))
            w_out = w_oab[j]
        else:
            gq = jnp.tile(g_q[j], C_HEADS)[None]
            gk = jnp.tile(g_k[j], C_KV_HEADS)[None]
            qT_p, k_p, qiT_p, wiT_p, ki3_p, *kv_c = _proj_c_prompt(
                x, g, w_c[j], gq, gk, segq, segk, bp, tp, j, n_odd, kv_c)
            q_s, k_s, v_s, qi_s, tail_s = (
                a.reshape(bs, ts, -1) for a in _proj_c_sample(x, g, w_c[j], gq, gk, segq, segk, mp, ms))
            ki_s = tail_s[..., :IDX_DIM]
            wi_s = tail_s[..., IDX_DIM:IDX_DIM + IDX_HEADS]
            parts_p = (_c_prompt(qT_p, k_p, kv_c[1], j, qiT_p, wiT_p, ki3_p, btiles).reshape(mp, C_WIDTH),)
            parts_s = (_c_sample(q_s, k_s, v_s, qi_s, ki_s, wi_s, cache_c_k, cache_c_v, cache_c_idx,
                                 j, page_table, sbias).reshape(ms, C_WIDTH),)
            kcs.append(k_s.reshape(bs, ts, C_KV_HEADS, C_HEAD_DIM))
            vcs.append(v_s.reshape(bs, ts, C_KV_HEADS, C_HEAD_DIM))
            ics.append(ki_s)
            w_out = w_oc[j]
        x_new = _out_proj(x, parts_p, w_out, 0)
        x = _out_proj(x, parts_s, w_out, mp, prev=x_new)
        ffn1 = (g_ffn[layer, 1][None], wg[layer, 1], wu[layer, 1], wd[layer, 1])
        if layer == DEPTH - 1:
            y_prompt = _ffn(x, *ffn1, rows=mp)
            y_sample = _ffn(x, *ffn1, in_row0=mp, rows=ms)
        else:
            x = _ffn(x, *ffn1)

    def token_major(aT, heads):
        n, b, width, t = aT.shape
        return aT.reshape(n, b, heads, width // heads, t).transpose(0, 1, 4, 2, 3)

    kbT, vbT = kv_b
    kcT, vcT, icT = kv_c
    return (y_prompt.reshape(bp, tp, D_MODEL), y_sample.reshape(bs, ts, D_MODEL),
            jnp.stack(pool_p), jnp.stack(pool_s),
            token_major(kbT, SB_HEADS), token_major(vbT, SB_HEADS), jnp.stack(kbs), jnp.stack(vbs),
            token_major(kcT, C_KV_HEADS), token_major(vcT, C_KV_HEADS), icT.transpose(0, 1, 3, 2),
            jnp.stack(kcs), jnp.stack(vcs), jnp.stack(ics))
```

```python
import functools
import math

import jax
import jax.numpy as jnp
import numpy as np
from jax import lax
from jax.experimental import pallas as pl
from jax.experimental.pallas import tpu as pltpu

F32 = jnp.float32
BF16 = jnp.bfloat16
MXU_DTYPE = BF16

D_MODEL = 1024
DEPTH = 4
D_FF = 2816
POOL_WINDOWS = (2, 4, 8, 16)
POOL_WIDTH = 512
POOL_GROUP = 128
POOL_BUF = 15
SB_HEADS = 8
SB_HEAD_DIM = 64
SB_WIDTH = 512
AB_IN = POOL_WIDTH + 3 * SB_WIDTH
C_HEAD_DIM = 64
C_HEADS = 16
C_KV_HEADS = 4
C_GROUP = 4
C_WIDTH = 1024
C_KV_WIDTH = 256
IDX_HEADS = 8
IDX_DIM = 64
TOPK_MAX = 256
C_IN = 2120
C_IN_PAD = 2176
REL_BUCKETS = 32
REL_MAX_EXACT = 16
REL_MAX_DIST = 128
PAGE_SIZE = 128
EPS = 1e-6
NEG = -1e30

VMEM_LIMIT_BYTES = 56 * 1024 * 1024
FF_CHUNK = 256
TOKEN_TILE = 512


def _rms(x, g):
    ms = jnp.mean(x * x, axis=-1, keepdims=True)
    return x * lax.rsqrt(ms + EPS) * g


def _ffn_kernel(x_ref, g_ref, wg_ref, wu_ref, wd_ref, *rest):
    o_ref = rest[-1]
    x = x_ref[...]
    h = _rms(x, g_ref[...]).astype(MXU_DTYPE)
    acc = x
    for c in range(D_FF // FF_CHUNK):
        sl = slice(c * FF_CHUNK, (c + 1) * FF_CHUNK)
        gate = jnp.dot(h, wg_ref[:, sl], preferred_element_type=F32)
        up = jnp.dot(h, wu_ref[:, sl], preferred_element_type=F32)
        act = (0.5 * gate * jax.nn.sigmoid(gate) * up).astype(MXU_DTYPE)
        acc = acc + jnp.dot(act, wd_ref[sl, :], preferred_element_type=F32)
    o_ref[...] = acc


def _ffn(x, g, wg, wu, wd, in_row0=0, rows=None, out_rows=None, out_row0=0, prev=None):
    rows = x.shape[0] if rows is None else rows
    out_rows = rows if out_rows is None else out_rows
    first_in, first_out = in_row0 // TOKEN_TILE, out_row0 // TOKEN_TILE
    resident = dict(pipeline_mode=pl.Buffered(1))
    alias_specs, aliases = _slot_alias(None if prev is None else (prev,), 5, 0)
    return pl.pallas_call(
        _ffn_kernel,
        out_shape=jax.ShapeDtypeStruct((out_rows, D_MODEL), F32),
        grid=(rows // TOKEN_TILE,),
        in_specs=[
            pl.BlockSpec((TOKEN_TILE, D_MODEL), lambda i: (first_in + i, 0)),
            pl.BlockSpec((1, D_MODEL), lambda i: (0, 0)),
            pl.BlockSpec((D_MODEL, D_FF), lambda i: (0, 0), **resident),
            pl.BlockSpec((D_MODEL, D_FF), lambda i: (0, 0), **resident),
            pl.BlockSpec((D_FF, D_MODEL), lambda i: (0, 0), **resident),
        ] + alias_specs,
        out_specs=pl.BlockSpec((TOKEN_TILE, D_MODEL), lambda i: (first_out + i, 0)),
        input_output_aliases=aliases,
        compiler_params=pltpu.CompilerParams(
            dimension_semantics=("parallel",), vmem_limit_bytes=VMEM_LIMIT_BYTES),
        name="ffn_half",
    )(x, g, wg, wu, wd, *(() if prev is None else (prev,)))


def _proj_ab_sample_kernel(x_ref, g_ref, w_ref, u_ref, q_ref, k_ref, v_ref):
    h = _rms(x_ref[...], g_ref[...]).astype(MXU_DTYPE)
    p = jnp.dot(h, w_ref[...], preferred_element_type=F32)
    u_ref[...] = p[:, :POOL_WIDTH]
    q_ref[...] = p[:, POOL_WIDTH:POOL_WIDTH + SB_WIDTH]
    k_ref[...] = p[:, POOL_WIDTH + SB_WIDTH:POOL_WIDTH + 2 * SB_WIDTH]
    v_ref[...] = p[:, POOL_WIDTH + 2 * SB_WIDTH:]


def _proj_ab_sample(x, g, w, row0, rows):
    first = row0 // TOKEN_TILE
    out = jax.ShapeDtypeStruct((rows, SB_WIDTH), F32)
    spec = pl.BlockSpec((TOKEN_TILE, SB_WIDTH), lambda i: (i, 0))
    return pl.pallas_call(
        _proj_ab_sample_kernel,
        out_shape=(out, out, out, out),
        grid=(rows // TOKEN_TILE,),
        in_specs=[
            pl.BlockSpec((TOKEN_TILE, D_MODEL), lambda i: (first + i, 0)),
            pl.BlockSpec((1, D_MODEL), lambda i: (0, 0)),
            pl.BlockSpec((D_MODEL, AB_IN), lambda i: (0, 0), pipeline_mode=pl.Buffered(1)),
        ],
        out_specs=(spec, spec, spec, spec),
        compiler_params=pltpu.CompilerParams(
            dimension_semantics=("parallel",), vmem_limit_bytes=VMEM_LIMIT_BYTES),
        name="proj_ab_sample",
    )(x, g, w)


def _proj_ab_prompt_kernel(x_ref, g_ref, w_ref, *rest):
    u_ref, qT_ref, kn_ref, kT_ref, vT_ref = rest[-5:]
    h = _rms(x_ref[...], g_ref[...]).astype(MXU_DTYPE)
    p = jnp.dot(h, w_ref[...], preferred_element_type=F32)
    u_ref[...] = p[:, :POOL_WIDTH]
    q = p[:, POOL_WIDTH:POOL_WIDTH + SB_WIDTH] * (SB_HEAD_DIM ** -0.5)
    k = p[:, POOL_WIDTH + SB_WIDTH:POOL_WIDTH + 2 * SB_WIDTH]
    v = p[:, POOL_WIDTH + 2 * SB_WIDTH:]
    qT_ref[0] = q.T.astype(MXU_DTYPE)
    kn_ref[0] = k.astype(MXU_DTYPE)
    kT_ref[0] = k.T
    vT_ref[0] = v.T


def _slot_alias(prev, first_input, first_output):
    if prev is None:
        return [], {}
    specs = [pl.BlockSpec(memory_space=pl.ANY)] * len(prev)
    return specs, {first_input + n: first_output + n for n in range(len(prev))}


def _proj_ab_prompt(x, g, w, batch, seq, slot, n_slots, prev):
    nt = seq // TOKEN_TILE
    alias_specs, aliases = _slot_alias(prev, 3, 3)
    stacked = jax.ShapeDtypeStruct((n_slots, batch, SB_WIDTH, seq), F32)
    stacked_spec = pl.BlockSpec((None, 1, SB_WIDTH, TOKEN_TILE), lambda b, i: (slot, b, 0, i))
    return pl.pallas_call(
        _proj_ab_prompt_kernel,
        out_shape=(jax.ShapeDtypeStruct((batch * seq, POOL_WIDTH), F32),
                   jax.ShapeDtypeStruct((batch, SB_WIDTH, seq), MXU_DTYPE),
                   jax.ShapeDtypeStruct((batch, seq, SB_WIDTH), MXU_DTYPE),
                   stacked, stacked),
        grid=(batch, nt),
        in_specs=[
            pl.BlockSpec((TOKEN_TILE, D_MODEL), lambda b, i: (b * nt + i, 0)),
            pl.BlockSpec((1, D_MODEL), lambda b, i: (0, 0)),
            pl.BlockSpec((D_MODEL, AB_IN), lambda b, i: (0, 0), pipeline_mode=pl.Buffered(1)),
        ] + alias_specs,
        out_specs=(pl.BlockSpec((TOKEN_TILE, POOL_WIDTH), lambda b, i: (b * nt + i, 0)),
                   pl.BlockSpec((1, SB_WIDTH, TOKEN_TILE), lambda b, i: (b, 0, i)),
                   pl.BlockSpec((1, TOKEN_TILE, SB_WIDTH), lambda b, i: (b, i, 0)),
                   stacked_spec, stacked_spec),
        input_output_aliases=aliases,
        compiler_params=pltpu.CompilerParams(
            dimension_semantics=("parallel", "parallel"), vmem_limit_bytes=VMEM_LIMIT_BYTES),
        name="proj_ab_prompt",
    )(x, g, w, *(prev or ()))


def _split_hi_lo(x):
    hi = x.astype(MXU_DTYPE)
    lo = (x - hi.astype(F32)).astype(MXU_DTYPE)
    return hi, lo


def _head_rms(x, seg_mean, seg_expand, gain):
    hi, lo = _split_hi_lo(x * x)
    ms = jnp.dot(hi, seg_mean, preferred_element_type=F32) + jnp.dot(lo, seg_mean, preferred_element_type=F32)
    hi, lo = _split_hi_lo(ms)
    ms = jnp.dot(hi, seg_expand, preferred_element_type=F32) + jnp.dot(lo, seg_expand, preferred_element_type=F32)
    return x * lax.rsqrt(ms + EPS) * gain


SEG_LANES = 128
C_OFF_K = C_WIDTH
C_OFF_V = C_OFF_K + C_KV_WIDTH
C_OFF_QI = C_OFF_V + C_KV_WIDTH
C_OFF_TAIL = C_OFF_QI + IDX_HEADS * IDX_DIM
C_TAIL = C_IN_PAD - C_OFF_TAIL


def _proj_c_parts(x_ref, g_ref, w_ref, gq_ref, gk_ref, segq_ref, segk_ref):
    h = _rms(x_ref[...], g_ref[...]).astype(MXU_DTYPE)
    p = jnp.dot(h, w_ref[...], preferred_element_type=F32)
    q = _head_rms(p[:, :C_OFF_K], segq_ref[...], segk_ref[...], gq_ref[...])
    k = _head_rms(p[:, C_OFF_K:C_OFF_V], segq_ref[:C_KV_WIDTH, :], segk_ref[:, :C_KV_WIDTH], gk_ref[...])
    return q, k, p[:, C_OFF_V:C_OFF_QI], p[:, C_OFF_QI:C_OFF_TAIL], p[:, C_OFF_TAIL:]


def _proj_c_sample_kernel(x_ref, g_ref, w_ref, gq_ref, gk_ref, segq_ref, segk_ref,
                          q_ref, k_ref, v_ref, qi_ref, tail_ref):
    q, k, v, qi, tail = _proj_c_parts(x_ref, g_ref, w_ref, gq_ref, gk_ref, segq_ref, segk_ref)
    q_ref[...] = q
    k_ref[...] = k
    v_ref[...] = v
    qi_ref[...] = qi
    tail_ref[...] = tail


def _proj_c_in_specs(row_block):
    const = lambda *_: (0, 0)
    return [
        pl.BlockSpec((TOKEN_TILE, D_MODEL), row_block),
        pl.BlockSpec((1, D_MODEL), const),
        pl.BlockSpec((D_MODEL, C_IN_PAD), const, pipeline_mode=pl.Buffered(1)),
        pl.BlockSpec((1, C_WIDTH), const),
        pl.BlockSpec((1, C_KV_WIDTH), const),
        pl.BlockSpec((C_WIDTH, SEG_LANES), const),
        pl.BlockSpec((SEG_LANES, C_WIDTH), const),
    ]


def _proj_c_sample(x, g, w, gq, gk, segq, segk, row0, rows):
    first = row0 // TOKEN_TILE
    widths = (C_WIDTH, C_KV_WIDTH, C_KV_WIDTH, IDX_HEADS * IDX_DIM, C_TAIL)
    return pl.pallas_call(
        _proj_c_sample_kernel,
        out_shape=tuple(jax.ShapeDtypeStruct((rows, n), F32) for n in widths),
        grid=(rows // TOKEN_TILE,),
        in_specs=_proj_c_in_specs(lambda i: (first + i, 0)),
        out_specs=tuple(pl.BlockSpec((TOKEN_TILE, n), lambda i: (i, 0)) for n in widths),
        compiler_params=pltpu.CompilerParams(
            dimension_semantics=("parallel",), vmem_limit_bytes=VMEM_LIMIT_BYTES),
        name="proj_c_sample",
    )(x, g, w, gq, gk, segq, segk)


def _proj_c_prompt_kernel(x_ref, g_ref, w_ref, gq_ref, gk_ref, segq_ref, segk_ref, *rest):
    qT_ref, kn_ref, qiT_ref, wiT_ref, ki3_ref, kT_ref, vT_ref, kiT_ref = rest[-8:]
    q, k, v, qi, tail = _proj_c_parts(x_ref, g_ref, w_ref, gq_ref, gk_ref, segq_ref, segk_ref)
    tm = q.shape[0]
    qT_ref[0] = (q * (C_HEAD_DIM ** -0.5 * LOG2E)).T.astype(MXU_DTYPE).reshape(C_HEADS, C_HEAD_DIM, tm)
    kn_ref[0] = k.astype(MXU_DTYPE)
    kT_ref[0] = k.T
    vT_ref[0] = v.T
    qi_hi, qi_lo = _split_hi_lo((qi * (IDX_DIM ** -0.5)).T)
    for hh in range(IDX_HEADS):
        rows = slice(hh * IDX_DIM, (hh + 1) * IDX_DIM)
        qiT_ref[0, hh, 0:IDX_DIM] = qi_hi[rows]
        qiT_ref[0, hh, IDX_DIM:2 * IDX_DIM] = qi_lo[rows]
        qiT_ref[0, hh, 2 * IDX_DIM:] = qi_hi[rows]
    tail_t = tail.T
    kiT_ref[0] = tail_t[:IDX_DIM]
    wiT_ref[0] = tail_t[IDX_DIM:IDX_DIM + IDX_HEADS]
    ki_hi, ki_lo = _split_hi_lo(tail[:, :IDX_DIM])
    ki3_ref[0] = jnp.concatenate([ki_hi, ki_hi, ki_lo], axis=1)


def _proj_c_prompt(x, g, w, gq, gk, segq, segk, batch, seq, slot, n_slots, prev):
    nt = seq // TOKEN_TILE
    tm = TOKEN_TILE
    alias_specs, aliases = _slot_alias(prev, 7, 5)

    def stacked(width):
        return (jax.ShapeDtypeStruct((n_slots, batch, width, seq), F32),
                pl.BlockSpec((None, 1, width, tm), lambda b, i: (slot, b, 0, i)))

    (kT_s, kT_b), (vT_s, vT_b), (kiT_s, kiT_b) = stacked(C_KV_WIDTH), stacked(C_KV_WIDTH), stacked(IDX_DIM)
    return pl.pallas_call(
        _proj_c_prompt_kernel,
        out_shape=(jax.ShapeDtypeStruct((batch, C_HEADS, C_HEAD_DIM, seq), MXU_DTYPE),
                   jax.ShapeDtypeStruct((batch, seq, C_KV_WIDTH), MXU_DTYPE),
                   jax.ShapeDtypeStruct((batch, IDX_HEADS, 3 * IDX_DIM, seq), MXU_DTYPE),
                   jax.ShapeDtypeStruct((batch, IDX_HEADS, seq), F32),
                   jax.ShapeDtypeStruct((batch, seq, 3 * IDX_DIM), MXU_DTYPE),
                   kT_s, vT_s, kiT_s),
        grid=(batch, nt),
        in_specs=_proj_c_in_specs(lambda b, i: (b * nt + i, 0)) + alias_specs,
        out_specs=(pl.BlockSpec((1, C_HEADS, C_HEAD_DIM, tm), lambda b, i: (b, 0, 0, i)),
                   pl.BlockSpec((1, tm, C_KV_WIDTH), lambda b, i: (b, i, 0)),
                   pl.BlockSpec((1, IDX_HEADS, 3 * IDX_DIM, tm), lambda b, i: (b, 0, 0, i)),
                   pl.BlockSpec((1, IDX_HEADS, tm), lambda b, i: (b, 0, i)),
                   pl.BlockSpec((1, tm, 3 * IDX_DIM), lambda b, i: (b, i, 0)),
                   kT_b, vT_b, kiT_b),
        input_output_aliases=aliases,
        compiler_params=pltpu.CompilerParams(
            dimension_semantics=("parallel", "parallel"), vmem_limit_bytes=VMEM_LIMIT_BYTES),
        name="proj_c_prompt",
    )(x, g, w, gq, gk, segq, segk, *(prev or ()))


def _out_proj_kernel(n_parts, x_ref, *rest):
    parts, w_ref, o_ref = rest[:n_parts], rest[n_parts], rest[-1]
    acc = x_ref[...]
    off = 0
    for part in parts:
        width = part.shape[1]
        acc = acc + jnp.dot(part[...].astype(MXU_DTYPE), w_ref[off:off + width, :],
                            preferred_element_type=F32)
        off += width
    o_ref[...] = acc


def _out_proj(x, parts, w, row0):
    rows = parts[0].shape[0]
    first = row0 // TOKEN_TILE
    x_tile = pl.BlockSpec((TOKEN_TILE, D_MODEL), lambda i: (first + i, 0))
    return pl.pallas_call(
        functools.partial(_out_proj_kernel, len(parts)),
        out_shape=jax.ShapeDtypeStruct(x.shape, F32),
        grid=(rows // TOKEN_TILE,),
        in_specs=[x_tile]
        + [pl.BlockSpec((TOKEN_TILE, p.shape[1]), lambda i: (i, 0)) for p in parts]
        + [pl.BlockSpec((D_MODEL, D_MODEL), lambda i: (0, 0), pipeline_mode=pl.Buffered(1))],
        out_specs=x_tile,
        input_output_aliases={0: 0},
        compiler_params=pltpu.CompilerParams(
            dimension_semantics=("parallel",), vmem_limit_bytes=VMEM_LIMIT_BYTES),
        name="out_proj",
    )(x, *parts, w)


ATT_TILE = 128
ATT_TK = 128
ATT_TQ = 256
SB_TK = 256
SB_LOCKSTEP = 2
SB_EXIT = -104.0


def _sb_prompt_kernel(qT_ref, k_ref, vT_ref, upper_ref, o_ref):
    tk, tq = SB_TK, ATT_TQ
    qb = pl.program_id(1)
    row = lax.broadcasted_iota(jnp.int32, (tk, tq), 0)
    lane = lax.broadcasted_iota(jnp.int32, (tk, tq), 1)
    q_idx = qb * tq + lane
    upper = upper_ref[...]
    last_tile = qb * (tq // tk) + (tq // tk - 1)

    def head_step(h, ks, valid, carry, acc):
        hs = slice(h * SB_HEAD_DIM, (h + 1) * SB_HEAD_DIM)
        kb = k_ref[0, pl.ds(ks, tk), hs]
        z = jnp.dot(kb, qT_ref[0, hs, :], preferred_element_type=F32)
        sp = jnp.maximum(z, 0.0) + jnp.log(1.0 + jnp.exp(-jnp.abs(z)))
        lr = jnp.where(valid, -sp, 0.0)
        hi, lo = _split_hi_lo(lr)
        between = (jnp.dot(upper, hi, preferred_element_type=F32)
                   + jnp.dot(upper, lo, preferred_element_type=F32) + carry)
        a = jnp.where(valid, jnp.exp(z - sp + between), 0.0)
        vb = vT_ref[0, hs, pl.ds(ks, tk)].astype(MXU_DTYPE)
        acc = acc + jnp.dot(vb, a.astype(MXU_DTYPE), preferred_element_type=F32)
        return carry + jnp.sum(lr, axis=0, keepdims=True), acc

    for h0 in range(0, SB_HEADS, SB_LOCKSTEP):
        heads = range(h0, h0 + SB_LOCKSTEP)

        def body(state, heads=heads):
            j, _, carries, accs = state
            ks = pl.multiple_of(j * tk, tk)
            valid = (ks + row) < q_idx
            out = [head_step(h, ks, valid, c, a) for h, c, a in zip(heads, carries, accs)]
            carries = tuple(o[0] for o in out)
            top = functools.reduce(jnp.maximum, [jnp.max(c) for c in carries])
            return j - 1, top, carries, tuple(o[1] for o in out)

        def cond(state):
            j, top, _, _ = state
            return jnp.logical_and(j >= 0, top > SB_EXIT)

        init = (last_tile, jnp.float32(0.0),
                tuple(jnp.zeros((1, tq), F32) for _ in heads),
                tuple(jnp.zeros((SB_HEAD_DIM, tq), F32) for _ in heads))
        _, _, _, accs = lax.while_loop(cond, body, init)
        o_ref[0, :, h0 * SB_HEAD_DIM:(h0 + SB_LOCKSTEP) * SB_HEAD_DIM] = jnp.concatenate(accs, axis=0).T


def _strict_upper(n):
    i = np.arange(n)
    return jnp.asarray((i[None, :] > i[:, None]).astype(np.float32), MXU_DTYPE)


def _strict_lower(n):
    i = np.arange(n)
    return jnp.asarray((i[None, :] < i[:, None]).astype(np.float32), MXU_DTYPE)


def _sb_prompt(qT, k, vT_all, slot):
    b, width, t = qT.shape
    assert (SB_LOCKSTEP * SB_HEAD_DIM) % 128 == 0
    return pl.pallas_call(
        _sb_prompt_kernel,
        out_shape=jax.ShapeDtypeStruct((b, t, width), F32),
        grid=(b, t // ATT_TQ),
        in_specs=[
            pl.BlockSpec((1, width, ATT_TQ), lambda i, j: (i, 0, j)),
            pl.BlockSpec((1, t, width), lambda i, j: (i, 0, 0)),
            pl.BlockSpec((None, 1, width, t), lambda i, j: (slot, i, 0, 0)),
            pl.BlockSpec((SB_TK, SB_TK), lambda i, j: (0, 0)),
        ],
        out_specs=pl.BlockSpec((1, ATT_TQ, width), lambda i, j: (i, j, 0)),
        compiler_params=pltpu.CompilerParams(
            dimension_semantics=("parallel", "arbitrary"), vmem_limit_bytes=VMEM_LIMIT_BYTES),
        name="sb_prompt",
    )(qT, k, vT_all, _strict_upper(SB_TK))


INT_MIN = -2 ** 31
COUNT_CHUNK = 4 * ATT_TK
LOG2E = math.log2(math.e)
C_TQ = 256
REL_NEAR_TILES = C_TQ // ATT_TK + 1
REL_LAST_BUCKET_FROM = math.ceil(REL_MAX_EXACT * (REL_MAX_DIST / REL_MAX_EXACT) ** (
    (REL_BUCKETS - 1 - REL_MAX_EXACT) / (REL_BUCKETS - REL_MAX_EXACT)))
assert ATT_TK + 1 >= REL_LAST_BUCKET_FROM
SUM_ROWS = 16


def _bucket_of(dist):
    n = jnp.maximum(dist, 0)
    nf = jnp.maximum(n, 1).astype(F32)
    large = REL_MAX_EXACT + (jnp.log(nf / REL_MAX_EXACT) / math.log(REL_MAX_DIST / REL_MAX_EXACT)
                             * (REL_BUCKETS - REL_MAX_EXACT)).astype(jnp.int32)
    return jnp.where(n < REL_MAX_EXACT, n, jnp.minimum(large, REL_BUCKETS - 1))


def _bias_tile_kernel(dec_seq, relb_ref, o_ref, s_ref):
    krow = lax.broadcasted_iota(jnp.int32, (ATT_TK, C_TQ), 0)
    qlane = lax.broadcasted_iota(jnp.int32, (ATT_TK, C_TQ), 1)
    for c in range(REL_NEAR_TILES):
        bucket = _bucket_of((1 - c) * ATT_TK + qlane - krow)
        for h in range(C_HEADS):
            tile = jnp.zeros((ATT_TK, C_TQ), F32)
            for b in range(REL_BUCKETS):
                tile = jnp.where(bucket == b, relb_ref[b, h], tile)
            o_ref[h, c] = (tile - relb_ref[REL_BUCKETS - 1, h]) * LOG2E
    t = ATT_TILE
    row = lax.broadcasted_iota(jnp.int32, (t, t), 0)
    lane = lax.broadcasted_iota(jnp.int32, (t, t), 1)
    row_head = row >> int(math.log2(dec_seq))
    row_t = row & (dec_seq - 1)
    buckets = (_bucket_of(t + row_t - lane), _bucket_of(row_t - lane),
               jnp.full((t, t), REL_BUCKETS - 1, jnp.int32))
    tiles = [jnp.zeros((t, t), F32) for _ in buckets]
    for b in range(REL_BUCKETS):
        by_head = jnp.zeros((t, t), F32)
        for h in range(C_HEADS):
            by_head = jnp.where(row_head == h, relb_ref[b, h], by_head)
        tiles = [jnp.where(bk == b, by_head, tl) for bk, tl in zip(buckets, tiles)]
    for c, tl in enumerate(tiles):
        s_ref[c] = tl


def _bias_tiles(rel_bias, dec_seq):
    assert C_HEADS * dec_seq == ATT_TILE
    return pl.pallas_call(
        functools.partial(_bias_tile_kernel, dec_seq),
        out_shape=(jax.ShapeDtypeStruct((C_HEADS, REL_NEAR_TILES, ATT_TK, C_TQ), F32),
                   jax.ShapeDtypeStruct((3, ATT_TILE, ATT_TILE), F32)),
        in_specs=[pl.BlockSpec(memory_space=pltpu.SMEM)],
        name="rel_bias_tiles",
    )(rel_bias)


def _sortable(s):
    bits = lax.bitcast_convert_type(s, jnp.int32)
    return bits ^ ((bits >> 31) & 0x7FFFFFFF)


def _c_prompt_kernel(topk, qiT_ref, wiT_ref, ki_ref, qT_ref, k_ref, vT_ref, btile_ref,
                     lower_ref, o_ref, key_ref, thr_ref, m_ref, l_ref, acc_ref):
    tk, tq = ATT_TK, C_TQ
    qb = pl.program_id(1)
    n_tiles = (qb + 1) * (tq // tk)
    n_chunks = (n_tiles * tk + COUNT_CHUNK - 1) // COUNT_CHUNK
    row = lax.broadcasted_iota(jnp.int32, (tk, tq), 0)
    lane = lax.broadcasted_iota(jnp.int32, (tk, tq), 1)
    q_idx = qb * tq + lane
    w = wiT_ref[0] * (IDX_HEADS ** -0.5)

    def score_tile(ks):
        kib = ki_ref[0, pl.ds(ks, tk), :]
        s = jnp.zeros((tk, tq), F32)
        for hh in range(IDX_HEADS):
            d = jnp.dot(kib, qiT_ref[0, hh], preferred_element_type=F32)
            s = s + jnp.maximum(d, 0.0) * w[hh:hh + 1, :]
        s = jnp.where(ks + row <= q_idx, s, -jnp.inf)
        key_ref[pl.ds(ks, tk), :] = _sortable(s)

    def score_chunk(c, _):
        for i in range(COUNT_CHUNK // tk):
            score_tile(pl.multiple_of(c * COUNT_CHUNK + i * tk, tk))
        return 0

    lax.fori_loop(0, n_chunks, score_chunk, 0)

    def count(pred, thr):
        def chunk(c, cnt):
            base = pl.multiple_of(c * COUNT_CHUNK, COUNT_CHUNK)
            ind = jnp.where(pred(key_ref[pl.ds(base, COUNT_CHUNK), :], thr), 1, 0)
            return cnt + jnp.sum(ind.reshape(COUNT_CHUNK // 8, 8, tq), axis=0)
        cnt = lax.fori_loop(0, n_chunks, chunk, jnp.zeros((8, tq), jnp.int32))
        return jnp.sum(cnt, axis=0, keepdims=True)

    def bit_step(i, t_u):
        cand_u = t_u | lax.shift_left(jnp.int32(1), 31 - i)
        cnt = count(lambda x, thr: x >= thr, cand_u ^ INT_MIN)
        return jnp.where(cnt >= topk, cand_u, t_u)

    thr = lax.fori_loop(0, 32, bit_step, jnp.zeros((1, tq), jnp.int32)) ^ INT_MIN
    n_ge = count(lambda x, t: x >= t, thr)
    thr_ref[...] = thr

    @pl.when(jnp.max(jnp.abs(n_ge - topk)) > 0)
    def _():
        need = (topk - count(lambda x, t: x > t, thr)).astype(F32)

        def select_block(j, seen):
            ks = pl.multiple_of(j * tk, tk)
            blk = key_ref[pl.ds(ks, tk), :]
            eq = jnp.where(blk == thr, 1.0, 0.0)
            rank = jnp.dot(lower_ref[...], eq.astype(MXU_DTYPE), preferred_element_type=F32) + seen
            tie = jnp.where(rank < need, eq, 0.0)
            sel = jnp.where(blk > thr, 1.0, tie)
            sel = jnp.where(ks + row <= q_idx, sel, 0.0)
            key_ref[pl.ds(ks, tk), :] = sel.astype(jnp.int32)
            return seen + jnp.sum(eq, axis=0, keepdims=True)

        lax.fori_loop(0, n_tiles, select_block, jnp.zeros((1, tq), F32))
        thr_ref[...] = jnp.ones((1, tq), jnp.int32)

    m_ref[...] = jnp.full(m_ref.shape, NEG, F32)
    l_ref[...] = jnp.zeros(l_ref.shape, F32)
    acc_ref[...] = jnp.zeros(acc_ref.shape, F32)
    ones = jnp.ones((SUM_ROWS, tk), MXU_DTYPE)

    def attend(j, near):
        ks = pl.multiple_of(j * tk, tk)
        sel = key_ref[pl.ds(ks, tk), :] >= thr_ref[...]
        for n in range(C_KV_HEADS):
            ns = slice(n * C_HEAD_DIM, (n + 1) * C_HEAD_DIM)
            kb = k_ref[0, pl.ds(ks, tk), ns]
            vb = vT_ref[0, ns, pl.ds(ks, tk)].astype(MXU_DTYPE)
            for g in range(C_GROUP):
                h = n * C_GROUP + g
                lg = jnp.dot(kb, qT_ref[0, h], preferred_element_type=F32)
                if near is not None:
                    lg = lg + btile_ref[h, near]
                lg = jnp.where(sel, lg, NEG)
                m_old = m_ref[h]
                m_new = jnp.maximum(m_old, jnp.max(lg, axis=0, keepdims=True))
                p = jnp.exp2(lg - m_new).astype(MXU_DTYPE)
                alpha = jnp.exp2(m_old - m_new)
                l_ref[h] = alpha * l_ref[h] + jnp.dot(ones, p, preferred_element_type=F32)
                acc_ref[h] = alpha * acc_ref[h] + jnp.dot(vb, p, preferred_element_type=F32)
                m_ref[h] = m_new

    def far_tile(j, _):
        attend(j, None)
        return 0

    first_near = n_tiles - REL_NEAR_TILES
    lax.fori_loop(0, jnp.maximum(first_near, 0), far_tile, 0)
    for c in range(REL_NEAR_TILES):
        if c == 0:
            pl.when(first_near >= 0)(functools.partial(attend, first_near, 0))
        else:
            attend(first_near + c, c)
    for h in range(0, C_HEADS, 2):
        pair = jnp.concatenate([acc_ref[h] / l_ref[h][0:1], acc_ref[h + 1] / l_ref[h + 1][0:1]], axis=0)
        o_ref[0, :, h * C_HEAD_DIM:(h + 2) * C_HEAD_DIM] = pair.T


def _c_prompt(qT, k, vT_all, slot, qiT, wiT, ki3, btiles):
    b, _, _, t = qT.shape
    topk = min(TOPK_MAX, t // 4)
    assert t % COUNT_CHUNK == 0 and COUNT_CHUNK > topk and t % C_TQ == 0
    tq, tk = C_TQ, ATT_TK
    return pl.pallas_call(
        functools.partial(_c_prompt_kernel, topk),
        out_shape=jax.ShapeDtypeStruct((b, t, C_WIDTH), F32),
        grid=(b, t // tq),
        in_specs=[
            pl.BlockSpec((1, IDX_HEADS, 3 * IDX_DIM, tq), lambda i, j: (i, 0, 0, j)),
            pl.BlockSpec((1, IDX_HEADS, tq), lambda i, j: (i, 0, j)),
            pl.BlockSpec((1, t, 3 * IDX_DIM), lambda i, j: (i, 0, 0)),
            pl.BlockSpec((1, C_HEADS, C_HEAD_DIM, tq), lambda i, j: (i, 0, 0, j)),
            pl.BlockSpec((1, t, C_KV_WIDTH), lambda i, j: (i, 0, 0)),
            pl.BlockSpec((None, 1, C_KV_WIDTH, t), lambda i, j: (slot, i, 0, 0)),
            pl.BlockSpec((C_HEADS, REL_NEAR_TILES, tk, tq), lambda i, j: (0, 0, 0, 0),
                         pipeline_mode=pl.Buffered(1)),
            pl.BlockSpec((tk, tk), lambda i, j: (0, 0)),
        ],
        out_specs=pl.BlockSpec((1, tq, C_WIDTH), lambda i, j: (i, j, 0)),
        scratch_shapes=[
            pltpu.VMEM((t, tq), jnp.int32),
            pltpu.VMEM((1, tq), jnp.int32),
            pltpu.VMEM((C_HEADS, 1, tq), F32),
            pltpu.VMEM((C_HEADS, SUM_ROWS, tq), F32),
            pltpu.VMEM((C_HEADS, C_HEAD_DIM, tq), F32),
        ],
        compiler_params=pltpu.CompilerParams(
            dimension_semantics=("parallel", "arbitrary"), vmem_limit_bytes=VMEM_LIMIT_BYTES),
        name="c_prompt",
    )(qiT, wiT, ki3, qT, k, vT_all, btiles, _strict_lower(tk))


_NT = (((1,), (1,)), ((), ()))
_NN = (((1,), (0,)), ((), ()))
SAMPLE_RADIX_BITS = 4


def _pad_rows(x, rows):
    return jnp.concatenate([x, jnp.zeros((rows - x.shape[0], x.shape[1]), x.dtype)], axis=0)


def _prefix_and_total(n, strict_before):
    i = np.arange(n)
    tri = (i[:, None] < i[None, :]) if strict_before else (i[:, None] > i[None, :])
    return jnp.asarray(np.concatenate([tri, np.ones((n, n), bool)], axis=1).astype(np.float32), MXU_DTYPE)


def _sb_sample_kernel(n_pages, resume, pt_ref, q_ref, kn_ref, vn_ref, hmask_ref, sufx_ref, *rest):
    k_pages, v_pages = rest[:n_pages], rest[n_pages:2 * n_pages]
    rest = rest[2 * n_pages:]
    if resume:
        carry_in_ref, acc_in_ref, o_ref, carry_ref, acc_ref = rest
    else:
        o_ref, carry_out_ref, acc_out_ref = rest
        carry_ref, acc_ref = carry_out_ref.at[0], acc_out_ref.at[0]
    t = q_ref.shape[1]
    rows, tk = SB_HEADS * t, PAGE_SIZE
    hmask = hmask_ref[...]
    q = q_ref[0] * (SB_HEAD_DIM ** -0.5)
    qbd = (jnp.concatenate([q] * SB_HEADS, axis=0) * hmask).astype(MXU_DTYPE)

    def step(z, weighted_values, valid):
        sp = jnp.maximum(z, 0.0) + jnp.log(1.0 + jnp.exp(-jnp.abs(z)))
        lr = -sp if valid is None else jnp.where(valid, -sp, 0.0)
        hi, lo = _split_hi_lo(lr)
        both = (jnp.dot(hi, sufx_ref[...], preferred_element_type=F32)
                + jnp.dot(lo, sufx_ref[...], preferred_element_type=F32))
        a = jnp.exp(z - sp + both[:, :tk] + carry_ref[...])
        if valid is not None:
            a = jnp.where(valid, a, 0.0)
        acc_ref[...] += weighted_values(a.astype(MXU_DTYPE))
        carry_ref[...] += both[:, tk:]

    if resume:
        carry_ref[...] = carry_in_ref[0]
        acc_ref[...] = acc_in_ref[0]
    else:
        carry_ref[...] = jnp.zeros(carry_ref.shape, F32)
        acc_ref[...] = jnp.zeros(acc_ref.shape, F32)
        row_t = lax.broadcasted_iota(jnp.int32, (rows, tk), 0) & (t - 1)
        lane = lax.broadcasted_iota(jnp.int32, (rows, tk), 1)
        kn = _pad_rows(kn_ref[0], tk).astype(MXU_DTYPE)
        vn = _pad_rows(vn_ref[0], tk).astype(MXU_DTYPE)
        step(lax.dot_general(qbd, kn, _NT, preferred_element_type=F32),
             lambda a: jnp.dot(a, vn, preferred_element_type=F32), lane < row_t)
    for p in range(n_pages):
        @pl.when(jnp.max(carry_ref[...]) > SB_EXIT)
        def _(p=p):
            kT = k_pages[p][...].reshape(SB_WIDTH, tk).astype(MXU_DTYPE)
            vT = v_pages[p][...].reshape(SB_WIDTH, tk).astype(MXU_DTYPE)
            step(jnp.dot(qbd, kT, preferred_element_type=F32),
                 lambda a: lax.dot_general(a, vT, _NT, preferred_element_type=F32), None)

    acc = acc_ref[...] * hmask
    y = acc[0:t]
    for h in range(1, SB_HEADS):
        y = y + acc[h * t:(h + 1) * t]
    o_ref[0] = y


def _feature_major(cache):
    nd = cache.ndim
    return cache.transpose((0, 1) + tuple(range(3, nd)) + (2,))


def _page_specs(pages, layer, page_shape):
    zeros = (0,) * len(page_shape)
    return [pl.BlockSpec((None, None) + page_shape, lambda i, pt, p=p: (layer, pt[i, p]) + zeros)
            for p in pages]


SB_NEAR_PAGES = 2


def _sb_sample_pass(pages, state, q, k, v, ck, cv, layer, page_table):
    b, t, _ = q.shape
    rows = SB_HEADS * t
    resume = state is not None
    page = (SB_HEADS, SB_HEAD_DIM, PAGE_SIZE)
    hmask = jnp.asarray((np.arange(rows)[:, None] // t == np.arange(SB_WIDTH)[None, :] // SB_HEAD_DIM)
                        .astype(np.float32))
    tok = pl.BlockSpec((1, t, SB_WIDTH), lambda i, pt: (i, 0, 0))
    const = lambda i, pt: (0, 0)
    carry = (jax.ShapeDtypeStruct((b, rows, PAGE_SIZE), F32),
             pl.BlockSpec((1, rows, PAGE_SIZE), lambda i, pt: (i, 0, 0)))
    acc = (jax.ShapeDtypeStruct((b, rows, SB_WIDTH), F32),
           pl.BlockSpec((1, rows, SB_WIDTH), lambda i, pt: (i, 0, 0)))
    y_shape = jax.ShapeDtypeStruct((b, t, SB_WIDTH), F32)
    return pl.pallas_call(
        functools.partial(_sb_sample_kernel, len(pages), resume),
        out_shape=y_shape if resume else (y_shape, carry[0], acc[0]),
        grid_spec=pltpu.PrefetchScalarGridSpec(
            num_scalar_prefetch=1,
            grid=(b,),
            in_specs=[tok, tok, tok,
                      pl.BlockSpec((rows, SB_WIDTH), const),
                      pl.BlockSpec((PAGE_SIZE, 2 * PAGE_SIZE), const)]
            + _page_specs(pages, layer, page) + _page_specs(pages, layer, page)
            + ([carry[1], acc[1]] if resume else []),
            out_specs=tok if resume else (tok, carry[1], acc[1]),
            scratch_shapes=([pltpu.VMEM((rows, PAGE_SIZE), F32), pltpu.VMEM((rows, SB_WIDTH), F32)]
                            if resume else []),
        ),
        compiler_params=pltpu.CompilerParams(
            dimension_semantics=("arbitrary",), vmem_limit_bytes=VMEM_LIMIT_BYTES),
        name="sb_sample_resume" if resume else "sb_sample",
    )(page_table, q, k, v, hmask, _prefix_and_total(PAGE_SIZE, False),
      *([ck] * len(pages)), *([cv] * len(pages)), *(state or ()))


def _sb_sample(q, k, v, cache_k, cache_v, layer, page_table):
    t = q.shape[1]
    assert t & (t - 1) == 0 and t <= PAGE_SIZE
    ck, cv = _feature_major(cache_k), _feature_major(cache_v)
    newest_first = tuple(reversed(range(page_table.shape[1])))
    near, far = newest_first[:SB_NEAR_PAGES], newest_first[SB_NEAR_PAGES:]
    args = (q, k, v, ck, cv, layer, page_table)
    y, carry, acc = _sb_sample_pass(near, None, *args)
    if not far:
        return y
    return lax.cond(jnp.max(carry) > SB_EXIT,
                    lambda: _sb_sample_pass(far, (carry, acc), *args), lambda: y)


def _c_sample_kernel(n_pages, topk, pt_ref, qi_ref, w_ref, q_ref, kin_ref, kn_ref, vn_ref,
                     sbias_ref, prex_ref, *rest):
    idx_pages, k_pages, v_pages = rest[:n_pages], rest[n_pages:2 * n_pages], rest[2 * n_pages:3 * n_pages]
    o_ref, sc_ref, lg_ref = rest[3 * n_pages:]
    t = kin_ref.shape[1]
    tk = PAGE_SIZE
    nb = n_pages + 1
    ih = IDX_HEADS * t
    rows = C_HEADS * t
    row8 = lax.broadcasted_iota(jnp.int32, (t, tk), 0)
    lane8 = lax.broadcasted_iota(jnp.int32, (t, tk), 1)
    new_valid = lane8 <= row8
    qi_cat = qi_ref[0]
    w = w_ref[0] * (IDX_HEADS ** -0.5)

    def scores(idx_blk, valid):
        dims = _NT if valid is not None else _NN
        hi, lo = _split_hi_lo(idx_blk)
        s = lax.dot_general(qi_cat, hi, dims, preferred_element_type=F32)
        s = s[:ih] + s[ih:] + lax.dot_general(qi_cat[:ih], lo, dims, preferred_element_type=F32)
        r = jnp.maximum(s, 0.0) * w
        sc = r[0:t]
        for hh in range(1, IDX_HEADS):
            sc = sc + r[hh * t:(hh + 1) * t]
        if valid is not None:
            sc = jnp.where(valid, sc, -jnp.inf)
        return _sortable(sc)

    for p in range(n_pages):
        sc_ref[:, p * tk:(p + 1) * tk] = scores(idx_pages[p][...], None)
    sc_ref[:, n_pages * tk:] = scores(_pad_rows(kin_ref[0], tk), new_valid)

    def count(pred, thr):
        return jnp.sum(jnp.where(pred(sc_ref[...], thr), 1.0, 0.0), axis=1, keepdims=True)

    def digit_step(i, t_u):
        shift = 32 - SAMPLE_RADIX_BITS * (i + 1)
        digit = jnp.zeros((t, 1), jnp.int32)
        for v in range(1, 2 ** SAMPLE_RADIX_BITS):
            cand_u = t_u | lax.shift_left(jnp.int32(v), shift)
            cnt = count(lambda x, thr: x >= thr, cand_u ^ INT_MIN)
            digit = digit + jnp.where(cnt >= topk, 1, 0)
        return t_u | lax.shift_left(digit, shift)

    thr = lax.fori_loop(0, 32 // SAMPLE_RADIX_BITS, digit_step, jnp.zeros((t, 1), jnp.int32)) ^ INT_MIN
    need = topk - count(lambda x, th: x > th, thr)

    seen = jnp.zeros((t, tk), F32)
    for blk in range(nb):
        x = sc_ref[:, blk * tk:(blk + 1) * tk]
        eq = jnp.where(x == thr, 1.0, 0.0)
        both = jnp.dot(eq.astype(MXU_DTYPE), prex_ref[...], preferred_element_type=F32)
        tie = jnp.where(both[:, :tk] + seen < need, eq, 0.0)
        sel = jnp.where(x > thr, 1.0, tie)
        if blk == n_pages:
            sel = jnp.where(new_valid, sel, 0.0)
        sc_ref[:, blk * tk:(blk + 1) * tk] = sel.astype(jnp.int32)
        seen = seen + both[:, tk:]

    q_rows = q_ref[0]
    grp = C_GROUP * t
    hd = C_HEAD_DIM

    def kv_block(pages, new_ref, blk, n):
        if blk < n_pages:
            return pages[blk][n].astype(MXU_DTYPE)
        return _pad_rows(new_ref[0][:, n * hd:(n + 1) * hd], tk).astype(MXU_DTYPE)

    m = jnp.full((rows, tk), NEG, F32)
    for blk in range(nb):
        bias = sbias_ref[2] if blk < n_pages - 1 else sbias_ref[blk - (n_pages - 1)]
        lg = jnp.concatenate(
            [lax.dot_general(q_rows[n * grp:(n + 1) * grp], kv_block(k_pages, kn_ref, blk, n),
                             _NN if blk < n_pages else _NT, preferred_element_type=F32)
             for n in range(C_KV_HEADS)], axis=0) + bias
        sel = jnp.concatenate([sc_ref[:, blk * tk:(blk + 1) * tk]] * C_HEADS, axis=0) != 0
        lg = jnp.where(sel, lg, NEG)
        lg_ref[:, blk * tk:(blk + 1) * tk] = lg
        m = jnp.maximum(m, lg)
    m_row = jnp.max(m, axis=1, keepdims=True)
    lsum = jnp.zeros((rows, tk), F32)
    accs = [jnp.zeros((grp, hd), F32) for _ in range(C_KV_HEADS)]
    for blk in range(nb):
        p = jnp.exp(lg_ref[:, blk * tk:(blk + 1) * tk] - m_row)
        lsum = lsum + p
        p = p.astype(MXU_DTYPE)
        accs = [acc + lax.dot_general(p[n * grp:(n + 1) * grp], kv_block(v_pages, vn_ref, blk, n),
                                      _NT if blk < n_pages else _NN, preferred_element_type=F32)
                for n, acc in enumerate(accs)]
    l_row = jnp.sum(lsum, axis=1, keepdims=True)
    for n, acc in enumerate(accs):
        o = acc / l_row[n * grp:(n + 1) * grp]
        for g in range(C_GROUP):
            o_ref[0, n * C_GROUP + g] = o[g * t:(g + 1) * t]


def _c_sample(q, k, v, qi, ki, wi, cache_k, cache_v, cache_idx, layer, page_table, sbias):
    b, t, _ = q.shape
    n_pages = page_table.shape[1]
    topk = min(TOPK_MAX, (n_pages * PAGE_SIZE + t) // 4)
    rows = C_HEADS * t
    ih = IDX_HEADS * t
    assert rows == ATT_TILE and t <= PAGE_SIZE
    q_rows = ((q * C_HEAD_DIM ** -0.5).reshape(b, t, C_HEADS, C_HEAD_DIM).transpose(0, 2, 1, 3)
              .reshape(b, rows, C_HEAD_DIM).astype(MXU_DTYPE))
    qi_rows = ((qi * IDX_DIM ** -0.5).reshape(b, t, IDX_HEADS, IDX_DIM).transpose(0, 2, 1, 3)
               .reshape(b, ih, IDX_DIM))
    qi_cat = jnp.concatenate(_split_hi_lo(qi_rows), axis=1)
    w_col = wi.transpose(0, 2, 1).reshape(b, ih, 1)
    kv_page = (C_KV_HEADS, C_HEAD_DIM, PAGE_SIZE)
    per_seq = lambda r, c: pl.BlockSpec((1, r, c), lambda i, pt: (i, 0, 0))
    const2 = lambda i, pt: (0, 0)
    o = pl.pallas_call(
        functools.partial(_c_sample_kernel, n_pages, topk),
        out_shape=jax.ShapeDtypeStruct((b, C_HEADS, t, C_HEAD_DIM), F32),
        grid_spec=pltpu.PrefetchScalarGridSpec(
            num_scalar_prefetch=1,
            grid=(b,),
            in_specs=[per_seq(2 * ih, IDX_DIM), per_seq(ih, 1), per_seq(rows, C_HEAD_DIM),
                      per_seq(t, IDX_DIM), per_seq(t, C_KV_WIDTH), per_seq(t, C_KV_WIDTH),
                      pl.BlockSpec((3, ATT_TILE, ATT_TILE), lambda i, pt: (0, 0, 0)),
                      pl.BlockSpec((PAGE_SIZE, 2 * PAGE_SIZE), const2)]
            + _page_specs(range(n_pages), layer, (IDX_DIM, PAGE_SIZE))
            + _page_specs(range(n_pages), layer, kv_page) + _page_specs(range(n_pages), layer, kv_page),
            out_specs=pl.BlockSpec((1, C_HEADS, t, C_HEAD_DIM), lambda i, pt: (i, 0, 0, 0)),
            scratch_shapes=[pltpu.VMEM((t, (n_pages + 1) * PAGE_SIZE), jnp.int32),
                            pltpu.VMEM((rows, (n_pages + 1) * PAGE_SIZE), F32)],
        ),
        compiler_params=pltpu.CompilerParams(
            dimension_semantics=("arbitrary",), vmem_limit_bytes=VMEM_LIMIT_BYTES),
        name="c_sample",
    )(page_table, qi_cat, w_col, q_rows, ki, k, v, sbias, _prefix_and_total(PAGE_SIZE, True),
      *([_feature_major(cache_idx)] * n_pages), *([_feature_major(cache_k)] * n_pages),
      *([_feature_major(cache_v)] * n_pages))
    return o.transpose(0, 2, 1, 3).reshape(b, t, C_WIDTH)


POOL_HALO = POOL_BUF + 1


def _pool_tile(ext, first_pos, w_ref, scale_ref):
    t = ext.shape[0] - POOL_HALO
    pos1 = lax.broadcasted_iota(jnp.int32, (t, POOL_GROUP), 0) + (first_pos + 1)
    outs = []
    for g, w in enumerate(POOL_WINDOWS):
        lanes = slice(g * POOL_GROUP, (g + 1) * POOL_GROUP)
        e = ext[:, lanes]
        s, span = e, 1
        while span < w:
            s = s + pltpu.roll(s, span, 0)
            span *= 2
        cnt = jnp.minimum(pos1, w).astype(F32)
        d = s[POOL_HALO:] / cnt - e[POOL_HALO:]
        y = jnp.dot(d.astype(MXU_DTYPE), w_ref[g], preferred_element_type=F32)
        outs.append(y * scale_ref[:, lanes])
    return jnp.concatenate(outs, axis=1)


def _pool_prompt_kernel(u_ref, halo_ref, w_ref, scale_ref, o_ref):
    i = pl.program_id(1)
    halo = jnp.where(i > 0, halo_ref[...], 0.0)
    ext = jnp.concatenate([halo, u_ref[...]], axis=0)
    o_ref[...] = _pool_tile(ext, i * u_ref.shape[0], w_ref, scale_ref)


def _pool_prompt(u, w_pool, pool_scale, batch, seq):
    assert max(POOL_WINDOWS) <= POOL_HALO and TOKEN_TILE % POOL_HALO == 0
    nt = seq // TOKEN_TILE
    per_tile = TOKEN_TILE // POOL_HALO
    return pl.pallas_call(
        _pool_prompt_kernel,
        out_shape=jax.ShapeDtypeStruct(u.shape, F32),
        grid=(batch, nt),
        in_specs=[
            pl.BlockSpec((TOKEN_TILE, POOL_WIDTH), lambda b, i: (b * nt + i, 0)),
            pl.BlockSpec((POOL_HALO, POOL_WIDTH),
                         lambda b, i: (jnp.maximum((b * nt + i) * per_tile - 1, 0), 0)),
            pl.BlockSpec(w_pool.shape, lambda b, i: (0, 0, 0)),
            pl.BlockSpec((1, POOL_WIDTH), lambda b, i: (0, 0)),
        ],
        out_specs=pl.BlockSpec((TOKEN_TILE, POOL_WIDTH), lambda b, i: (b * nt + i, 0)),
        compiler_params=pltpu.CompilerParams(
            dimension_semantics=("parallel", "arbitrary"), vmem_limit_bytes=VMEM_LIMIT_BYTES),
        name="pool_prompt",
    )(u, u, w_pool.astype(MXU_DTYPE), pool_scale[None])


POOL_SAMPLE_GROUP = 8


def _pool_sample_kernel(first_pos, ext_ref, w_ref, scale_ref, o_ref):
    for s in range(ext_ref.shape[0]):
        o_ref[s] = _pool_tile(ext_ref[s], first_pos, w_ref, scale_ref)


def _pool_sample(u_ext, first_pos, w_pool, pool_scale):
    n, rows, _ = u_ext.shape
    t = rows - POOL_HALO
    grp = POOL_SAMPLE_GROUP
    return pl.pallas_call(
        functools.partial(_pool_sample_kernel, first_pos),
        out_shape=jax.ShapeDtypeStruct((n, t, POOL_WIDTH), F32),
        grid=(n // grp,),
        in_specs=[
            pl.BlockSpec((grp, rows, POOL_WIDTH), lambda i: (i, 0, 0)),
            pl.BlockSpec(w_pool.shape, lambda i: (0, 0, 0)),
            pl.BlockSpec((1, POOL_WIDTH), lambda i: (0, 0)),
        ],
        out_specs=pl.BlockSpec((grp, t, POOL_WIDTH), lambda i: (i, 0, 0)),
        compiler_params=pltpu.CompilerParams(
            dimension_semantics=("parallel",), vmem_limit_bytes=VMEM_LIMIT_BYTES),
        name="pool_sample",
    )(u_ext, w_pool.astype(MXU_DTYPE), pool_scale[None])


def _segment_matrices():
    member = (np.arange(C_WIDTH)[:, None] // C_HEAD_DIM == np.arange(SEG_LANES)[None, :]).astype(np.float32)
    return jnp.asarray(member / C_HEAD_DIM, MXU_DTYPE), jnp.asarray(member.T, MXU_DTYPE)


def kernel(x_prompt, x_sample, state_pool, cache_b_k, cache_b_v, cache_c_k, cache_c_v, cache_c_idx, page_table,
           g_ffn, w_ffn_gate, w_ffn_up, w_ffn_down, g_mix, w_in_ab, w_pool, pool_scale, w_out_ab,
           w_in_c, g_q, g_k, w_out_c, rel_bias):
    bp, tp, _ = x_prompt.shape
    bs, ts, _ = x_sample.shape
    mp, ms = bp * tp, bs * ts
    n_even, n_odd = (DEPTH + 1) // 2, DEPTH // 2
    past = page_table.shape[1] * PAGE_SIZE

    wg = w_ffn_gate.astype(MXU_DTYPE)
    wu = w_ffn_up.astype(MXU_DTYPE)
    wd = w_ffn_down.astype(MXU_DTYPE)
    w_ab = w_in_ab.astype(MXU_DTYPE)
    w_oab = w_out_ab.astype(MXU_DTYPE)
    w_c = jnp.pad(w_in_c, ((0, 0), (0, 0), (0, C_IN_PAD - C_IN))).astype(MXU_DTYPE)
    w_oc = w_out_c.astype(MXU_DTYPE)
    segq, segk = _segment_matrices()
    btiles, sbias = _bias_tiles(rel_bias, ts)

    pool_p, pool_s, kbs, vbs, kcs, vcs, ics = [], [], [], [], [], [], []
    kv_b = [jnp.zeros((n_even, bp, SB_WIDTH, tp), F32) for _ in range(2)]
    kv_c = [jnp.zeros((n_odd, bp, width, tp), F32) for width in (C_KV_WIDTH, C_KV_WIDTH, IDX_DIM)]
    for layer in range(DEPTH):
        j = layer // 2
        ffn0 = (g_ffn[layer, 0][None], wg[layer, 0], wu[layer, 0], wd[layer, 0])
        if layer == 0:
            x = jnp.zeros((mp + ms, D_MODEL), F32)
            x = _ffn(x_prompt.reshape(mp, D_MODEL), *ffn0, out_rows=mp + ms, prev=x)
            x = _ffn(x_sample.reshape(ms, D_MODEL), *ffn0, out_rows=mp + ms, out_row0=mp, prev=x)
        else:
            x = _ffn(x, *ffn0)
        g = g_mix[layer][None]
        if layer % 2 == 0:
            u_p, qT_p, k_p, *kv_b = _proj_ab_prompt(x, g, w_ab[j], bp, tp, j, n_even, kv_b)
            u_s, q_s, k_s, v_s = (a.reshape(bs, ts, -1) for a in _proj_ab_sample(x, g, w_ab[j], mp, ms))
            u_ext = jnp.concatenate([jnp.zeros((bs, POOL_HALO - POOL_BUF, POOL_WIDTH), F32),
                                     state_pool[j], u_s], axis=1)
            parts_p = (_pool_prompt(u_p, w_pool[j], pool_scale[j], bp, tp),
                       _sb_prompt(qT_p, k_p, kv_b[1], j).reshape(mp, SB_WIDTH))
            parts_s = (_pool_sample(u_ext, past, w_pool[j], pool_scale[j]).reshape(ms, POOL_WIDTH),
                       _sb_sample(q_s, k_s, v_s, cache_b_k, cache_b_v, j, page_table).reshape(ms, SB_WIDTH))
            pool_p.append(u_p.reshape(bp, tp, POOL_WIDTH)[:, tp - POOL_BUF:])
            pool_s.append(u_ext[:, -POOL_BUF:])
            kbs.append(k_s.reshape(bs, ts, SB_HEADS, SB_HEAD_DIM))
            vbs.append(v_s.reshape(bs, ts, SB_HEADS, SB_HEAD_DIM))
            w_out = w_oab[j]
        else:
            gq = jnp.tile(g_q[j], C_HEADS)[None]
            gk = jnp.tile(g_k[j], C_KV_HEADS)[None]
            qT_p, k_p, qiT_p, wiT_p, ki3_p, *kv_c = _proj_c_prompt(
                x, g, w_c[j], gq, gk, segq, segk, bp, tp, j, n_odd, kv_c)
            q_s, k_s, v_s, qi_s, tail_s = (
                a.reshape(bs, ts, -1) for a in _proj_c_sample(x, g, w_c[j], gq, gk, segq, segk, mp, ms))
            ki_s = tail_s[..., :IDX_DIM]
            wi_s = tail_s[..., IDX_DIM:IDX_DIM + IDX_HEADS]
            parts_p = (_c_prompt(qT_p, k_p, kv_c[1], j, qiT_p, wiT_p, ki3_p, btiles).reshape(mp, C_WIDTH),)
            parts_s = (_c_sample(q_s, k_s, v_s, qi_s, ki_s, wi_s, cache_c_k, cache_c_v, cache_c_idx,
                                 j, page_table, sbias).reshape(ms, C_WIDTH),)
            kcs.append(k_s.reshape(bs, ts, C_KV_HEADS, C_HEAD_DIM))
            vcs.append(v_s.reshape(bs, ts, C_KV_HEADS, C_HEAD_DIM))
            ics.append(ki_s)
            w_out = w_oc[j]
        x = _out_proj(x, parts_p, w_out, 0)
        x = _out_proj(x, parts_s, w_out, mp)
        ffn1 = (g_ffn[layer, 1][None], wg[layer, 1], wu[layer, 1], wd[layer, 1])
        if layer == DEPTH - 1:
            y_prompt = _ffn(x, *ffn1, rows=mp)
            y_sample = _ffn(x, *ffn1, in_row0=mp, rows=ms)
        else:
            x = _ffn(x, *ffn1)

    def token_major(aT, heads):
        n, b, width, t = aT.shape
        return aT.reshape(n, b, heads, width // heads, t).transpose(0, 1, 4, 2, 3)

    kbT, vbT = kv_b
    kcT, vcT, icT = kv_c
    return (y_prompt.reshape(bp, tp, D_MODEL), y_sample.reshape(bs, ts, D_MODEL),
            jnp.stack(pool_p), jnp.stack(pool_s),
            token_major(kbT, SB_HEADS), token_major(vbT, SB_HEADS), jnp.stack(kbs), jnp.stack(vbs),
            token_major(kcT, C_KV_HEADS), token_major(vcT, C_KV_HEADS), icT.transpose(0, 1, 3, 2),
            jnp.stack(kcs), jnp.stack(vcs), jnp.stack(ics))
```
